```python
import math
import jax, jax.numpy as jnp
from jax import lax
import numpy as np

D_MODEL = 1024
BATCH = 8
SEQ = 4096
DEPTH = 1

N_HEADS = 8
HEAD_DIM = 64
D_ATTN = N_HEADS * HEAD_DIM
BLOCK = 256
TOPK = 3
Q_CHUNK = 32
NUM_BUCKETS = 32
MAX_DISTANCE = 128
D_RNN = D_MODEL
RNN_BLOCKS = 16
RNN_BLOCK_DIM = D_RNN // RNN_BLOCKS
CONV_WIDTH = 4
LRU_C = 8.0
D_FF = 4 * D_MODEL
EPS = 1e-6
IN_SPLITS = (D_ATTN, D_ATTN, D_ATTN, D_RNN, D_RNN, D_MODEL, D_MODEL)
D_IN = sum(IN_SPLITS)

kernel_name = "hybrid_moba_rglru_gated_block"


def rmsnorm(x, w):
    xf = x.astype(jnp.float32)
    y = xf * lax.rsqrt(jnp.mean(jnp.square(xf), axis=-1, keepdims=True) + EPS)
    return (y * w.astype(jnp.float32)).astype(x.dtype)


def t5_bucket(rel):
    max_exact = NUM_BUCKETS // 2
    n = jnp.maximum(rel, 0)
    nf = jnp.maximum(n, 1).astype(jnp.float32)
    large = max_exact + (jnp.log(nf / max_exact) / math.log(MAX_DISTANCE / max_exact)
                         * (NUM_BUCKETS - max_exact)).astype(jnp.int32)
    large = jnp.minimum(large, NUM_BUCKETS - 1)
    return jnp.where(n < max_exact, n, large)


def moba_attention(q, k, v, rel_bias):
    B, H, S, Dh = q.shape
    nb = -(-S // BLOCK)
    s_pad = nb * BLOCK
    pad = ((0, 0), (0, 0), (0, s_pad - S), (0, 0))
    q = jnp.pad(q, pad)
    k = jnp.pad(k, pad)
    v = jnp.pad(v, pad)
    kb = k.reshape(B, H, nb, BLOCK, Dh)
    vb = v.reshape(B, H, nb, BLOCK, Dh)
    k_mean = jnp.mean(kb.astype(jnp.float32), axis=3)
    topk = min(TOPK, nb)
    scale = HEAD_DIM ** -0.5
    bias_tab = rel_bias.T
    b_idx = jnp.arange(B)[:, None, None, None]
    h_idx = jnp.arange(H)[None, :, None, None]
    blk_ids = jnp.arange(nb)
    offs = jnp.arange(BLOCK)
    slot_ids = jnp.arange(topk)

    def chunk(c):
        start = c * Q_CHUNK
        qc = lax.dynamic_slice_in_dim(q, start, Q_CHUNK, axis=2)
        q_pos = start + jnp.arange(Q_CHUNK)
        own = start // BLOCK
        gate = jnp.einsum('bhqd,bhnd->bhqn', qc.astype(jnp.float32), k_mean)
        gate = jnp.where(blk_ids < own, gate, -jnp.inf)
        _, sel = lax.top_k(gate, topk)
        sel_valid = slot_ids < own
        k_sel = kb[b_idx, h_idx, sel]
        v_sel = vb[b_idx, h_idx, sel]
        s_sel = jnp.einsum('bhqd,bhqjkd->bhqjk', qc, k_sel).astype(jnp.float32) * scale
        k_pos_sel = sel[..., None] * BLOCK + offs
        bucket_sel = t5_bucket(q_pos[:, None, None] - k_pos_sel)
        s_sel = s_sel + bias_tab[h_idx[..., None], bucket_sel]
        s_sel = jnp.where(sel_valid[:, None], s_sel, -jnp.inf)
        s_sel = s_sel.reshape(B, H, Q_CHUNK, topk * BLOCK)
        k_own = lax.dynamic_slice_in_dim(k, own * BLOCK, BLOCK, axis=2)
        v_own = lax.dynamic_slice_in_dim(v, own * BLOCK, BLOCK, axis=2)
        s_own = jnp.einsum('bhqd,bhkd->bhqk', qc, k_own).astype(jnp.float32) * scale
        rel_own = q_pos[:, None] - (own * BLOCK + offs)[None, :]
        s_own = s_own + bias_tab[:, t5_bucket(rel_own)]
        s_own = jnp.where(rel_own >= 0, s_own, -jnp.inf)
        p = jax.nn.softmax(jnp.concatenate([s_sel, s_own], axis=-1), axis=-1)
        p_sel = p[..., :topk * BLOCK].reshape(B, H, Q_CHUNK, topk, BLOCK).astype(v.dtype)
        p_own = p[..., topk * BLOCK:].astype(v.dtype)
        return (jnp.einsum('bhqjk,bhqjkd->bhqd', p_sel, v_sel)
                + jnp.einsum('bhqk,bhkd->bhqd', p_own, v_own))

    out = lax.map(chunk, jnp.arange(s_pad // Q_CHUNK))
    out = out.transpose(1, 2, 0, 3, 4).reshape(B, H, s_pad, Dh)
    return out[:, :, :S]


def causal_depthwise_conv(x, w, b):
    S = x.shape[1]
    xp = jnp.pad(x, ((0, 0), (CONV_WIDTH - 1, 0), (0, 0)))
    y = b + xp[:, 0:S] * w[0]
    for j in range(1, CONV_WIDTH):
        y = y + xp[:, j:j + S] * w[j]
    return y


def block_diag_linear(x, w, b):
    B, S, _ = x.shape
    xb = x.reshape(B, S, RNN_BLOCKS, RNN_BLOCK_DIM)
    y = jnp.einsum('bsnd,nde->bsne', xb, w).reshape(B, S, D_RNN)
    return y + b


def rg_lru(x, w_a, b_a, w_i, b_i, lam):
    r = jax.nn.sigmoid(block_diag_linear(x, w_a, b_a).astype(jnp.float32))
    i = jax.nn.sigmoid(block_diag_linear(x, w_i, b_i).astype(jnp.float32))
    log_a = -LRU_C * r * jax.nn.softplus(-lam.astype(jnp.float32))
    a = jnp.exp(log_a)
    u = jnp.sqrt(-jnp.expm1(2.0 * log_a)) * (i * x.astype(jnp.float32))

    def combine(left, right):
        a1, b1 = left
        a2, b2 = right
        return a1 * a2, a2 * b1 + b2

    _, h = lax.associative_scan(combine, (a, u), axis=1)
    return h.astype(x.dtype)


def setup_inputs(seed: int = 0) -> dict:
    key = jax.random.key(seed)
    ks = jax.random.split(key, 20)
    L, D = DEPTH, D_MODEL
    nrm = lambda k, shape, s: jax.random.normal(k, shape, jnp.float32) * s
    u = jax.random.uniform(ks[13], (L, D_RNN), jnp.float32, minval=0.9, maxval=0.999)
    sa = u ** (1.0 / LRU_C)
    lam = jnp.log(sa) - jnp.log1p(-sa)
    return {
        "x": nrm(ks[0], (BATCH, SEQ, D), 1.0),
        "norm1_w": 1.0 + nrm(ks[1], (L, D), 0.02),
        "w_in": nrm(ks[2], (L, D, D_IN), D ** -0.5),
        "b_gate": nrm(ks[3], (L, 2, D), 0.1),
        "q_norm_w": 1.0 + nrm(ks[4], (L, HEAD_DIM), 0.02),
        "k_norm_w": 1.0 + nrm(ks[5], (L, HEAD_DIM), 0.02),
        "rel_bias": nrm(ks[6], (NUM_BUCKETS, N_HEADS), 0.2),
        "conv_w": nrm(ks[7], (L, CONV_WIDTH, D_RNN), CONV_WIDTH ** -0.5),
        "conv_b": nrm(ks[8], (L, D_RNN), 0.02),
        "w_rg_a": nrm(ks[9], (L, RNN_BLOCKS, RNN_BLOCK_DIM, RNN_BLOCK_DIM), RNN_BLOCK_DIM ** -0.5),
        "b_rg_a": nrm(ks[10], (L, D_RNN), 0.1),
        "w_rg_i": nrm(ks[11], (L, RNN_BLOCKS, RNN_BLOCK_DIM, RNN_BLOCK_DIM), RNN_BLOCK_DIM ** -0.5),
        "b_rg_i": nrm(ks[12], (L, D_RNN), 0.1),
        "lru_lambda": lam,
        "w_proj_attn": nrm(ks[14], (L, D_ATTN, D), D_ATTN ** -0.5),
        "w_proj_rnn": nrm(ks[15], (L, D_RNN, D), D_RNN ** -0.5),
        "w_out": nrm(ks[16], (L, D, D), D ** -0.5),
        "norm2_w": 1.0 + nrm(ks[17], (L, D), 0.02),
        "w_ff1": nrm(ks[18], (L, D, D_FF), D ** -0.5),
        "w_ff2": nrm(ks[19], (L, D_FF, D), D_FF ** -0.5),
    }


def reference(x, norm1_w, w_in, b_gate, q_norm_w, k_norm_w, rel_bias, conv_w, conv_b,
              w_rg_a, b_rg_a, w_rg_i, b_rg_i, lru_lambda, w_proj_attn, w_proj_rnn,
              w_out, norm2_w, w_ff1, w_ff2):
    B, S, _ = x.shape
    split_points = [int(p) for p in np.cumsum(IN_SPLITS)[:-1]]
    for l in range(DEPTH):
        h = rmsnorm(x, norm1_w[l])
        z = h @ w_in[l]
        q, k, v, xr, yr, ga, gr = jnp.split(z, split_points, axis=-1)
        to_heads = lambda t: t.reshape(B, S, N_HEADS, HEAD_DIM).transpose(0, 2, 1, 3)
        qh = rmsnorm(to_heads(q), q_norm_w[l])
        kh = rmsnorm(to_heads(k), k_norm_w[l])
        vh = to_heads(v)
        o_attn = moba_attention(qh, kh, vh, rel_bias)
        o_attn = o_attn.transpose(0, 2, 1, 3).reshape(B, S, D_ATTN)
        xr = causal_depthwise_conv(xr, conv_w[l], conv_b[l])
        hr = rg_lru(xr, w_rg_a[l], b_rg_a[l], w_rg_i[l], b_rg_i[l], lru_lambda[l])
        o_rnn = hr * jax.nn.gelu(yr)
        g_attn = jax.nn.sigmoid(ga + b_gate[l, 0])
        g_rnn = jax.nn.sigmoid(gr + b_gate[l, 1])
        merged = g_attn * (o_attn @ w_proj_attn[l]) + g_rnn * (o_rnn @ w_proj_rnn[l])
        x = x + merged @ w_out[l]
        h2 = rmsnorm(x, norm2_w[l])
        x = x + jnp.square(jax.nn.relu(h2 @ w_ff1[l])) @ w_ff2[l]
    return x
```

```python
import functools
import math

import numpy as np
import jax
import jax.numpy as jnp
from jax import lax
from jax.experimental import pallas as pl
from jax.experimental.pallas import tpu as pltpu

N_HEADS = 8
HEAD_DIM = 64
D_ATTN = N_HEADS * HEAD_DIM
BLOCK = 256
TOPK = 3
NUM_BUCKETS = 32
MAX_DISTANCE = 128
RNN_BLOCKS = 16
RNN_BLOCK_DIM = 64
CONV_WIDTH = 4
LRU_C = 8.0
EPS = 1e-6
NEG = -1e30
HEADS_PER_STEP = 2
GROUP = 256
SUBLANES = 8
VMEM_LIMIT = 56 * 1024 * 1024

F32 = jnp.float32
BF16 = jnp.bfloat16
_NT = (((1,), (1,)), ((), ()))


def _const_spec(shape):
    nd = len(shape)
    return pl.BlockSpec(shape, lambda *_: (0,) * nd, pipeline_mode=pl.Buffered(1))


def _inproj_body(x_ref, n1_ref, wqT_ref, wk_ref, wvT_ref, wr_ref, qw_ref, kw_ref, g_ref,
                 qT_ref, k_ref, vT_ref, km_ref, rest_ref):
    tm = x_ref.shape[1]
    x = x_ref[0]
    ms = jnp.mean(x * x, axis=-1, keepdims=True)
    h = (x * lax.rsqrt(ms + EPS) * n1_ref[...]).astype(BF16)

    qT = lax.dot_general(wqT_ref[...], h, _NT, preferred_element_type=F32)
    q3 = qT.reshape(N_HEADS, HEAD_DIM, tm)
    qss = jnp.mean(q3 * q3, axis=1, keepdims=True)
    qn = q3 * lax.rsqrt(qss + EPS) * qw_ref[...]
    qT_ref[0] = qn.reshape(D_ATTN, tm).astype(BF16)
    vT_ref[0] = lax.dot_general(wvT_ref[...], h, _NT, preferred_element_type=F32).astype(BF16)

    k = jnp.dot(h, wk_ref[...], preferred_element_type=F32)
    k2 = k * k
    hi = k2.astype(BF16)
    lo = (k2 - hi.astype(F32)).astype(BF16)
    kss = (jnp.dot(hi, g_ref[...], preferred_element_type=F32)
           + jnp.dot(lo, g_ref[...], preferred_element_type=F32))
    kn = k * lax.rsqrt(kss * (1.0 / HEAD_DIM) + EPS) * kw_ref[...]
    k_ref[0] = kn.astype(BF16)
    for bi in range(tm // BLOCK):
        km_ref[0, bi] = jnp.mean(kn[bi * BLOCK:(bi + 1) * BLOCK], axis=0, keepdims=True)

    d_rest = wr_ref.shape[1]
    cw = 1024
    for c in range(d_rest // cw):
        rest_ref[0, :, c * cw:(c + 1) * cw] = jnp.dot(
            h, wr_ref[:, c * cw:(c + 1) * cw], preferred_element_type=F32).astype(BF16)


def _inproj(x, n1, wqT, wk, wvT, wr, qw, kw, gmat, tm):
    B, S, D = x.shape
    d_rest = wr.shape[1]
    nb = S // BLOCK
    grid = (B, S // tm)
    return pl.pallas_call(
        _inproj_body,
        grid=grid,
        in_specs=[
            pl.BlockSpec((1, tm, D), lambda b, s: (b, s, 0)),
            _const_spec(n1.shape), _const_spec(wqT.shape), _const_spec(wk.shape),
            _const_spec(wvT.shape), _const_spec(wr.shape), _const_spec(qw.shape),
            _const_spec(kw.shape), _const_spec(gmat.shape),
        ],
        out_specs=[
            pl.BlockSpec((1, D_ATTN, tm), lambda b, s: (b, 0, s)),
            pl.BlockSpec((1, tm, D_ATTN), lambda b, s: (b, s, 0)),
            pl.BlockSpec((1, D_ATTN, tm), lambda b, s: (b, 0, s)),
            pl.BlockSpec((1, tm // BLOCK, 1, D_ATTN), lambda b, s: (b, s, 0, 0)),
            pl.BlockSpec((1, tm, d_rest), lambda b, s: (b, s, 0)),
        ],
        out_shape=[
            jax.ShapeDtypeStruct((B, D_ATTN, S), BF16),
            jax.ShapeDtypeStruct((B, S, D_ATTN), BF16),
            jax.ShapeDtypeStruct((B, D_ATTN, S), BF16),
            jax.ShapeDtypeStruct((B, nb, 1, D_ATTN), F32),
            jax.ShapeDtypeStruct((B, S, d_rest), BF16),
        ],
        compiler_params=pltpu.CompilerParams(
            dimension_semantics=("parallel", "parallel"), vmem_limit_bytes=VMEM_LIMIT),
        name="inproj",
    )(x, n1, wqT, wk, wvT, wr, qw, kw, gmat)


def _attn_body(far_ref, qT_ref, k_ref, vT_ref, km_ref, bd_ref, bp_ref, o_ref, sel_ref):
    p = pl.program_id(1)
    i = pl.program_id(2)
    nb = km_ref.shape[1]
    lanes = HEADS_PER_STEP * HEAD_DIM
    qpair = qT_ref[0]
    row = lax.broadcasted_iota(jnp.int32, (lanes, BLOCK), 0)
    blk = lax.broadcasted_iota(jnp.int32, (nb, BLOCK), 0)
    past = blk < i
    outs = []
    for hh in range(HEADS_PER_STEP):
        qz = jnp.where((row >= hh * HEAD_DIM) & (row < (hh + 1) * HEAD_DIM), qpair, jnp.zeros_like(qpair))

        gate = jnp.dot(km_ref[0], qz.astype(F32), preferred_element_type=F32,
                       precision=lax.Precision.HIGHEST)
        gate = jnp.where(past, gate, -jnp.inf)
        rank = jnp.zeros((nb, BLOCK), jnp.int32)
        for m in range(nb):
            gm = gate[m:m + 1, :]
            beats = jnp.where(gm > gate, 1, jnp.where((gm == gate) & (blk > m), 1, 0))
            rank = rank + beats
        sel_ref[...] = jnp.where(past & (rank < TOPK), 0.0, NEG)

        def scores(j, bias):
            kb = k_ref[0, pl.ds(pl.multiple_of(j * BLOCK, BLOCK), BLOCK), :]
            return jnp.dot(kb, qz, preferred_element_type=F32) + bias

        def values(j):
            return vT_ref[0, hh * HEAD_DIM:(hh + 1) * HEAD_DIM,
                          pl.ds(pl.multiple_of(j * BLOCK, BLOCK), BLOCK)]

        s = scores(i, bd_ref[hh])
        m0 = jnp.max(s, axis=0, keepdims=True)
        pm = jnp.exp(s - m0)
        l0 = jnp.sum(pm, axis=0, keepdims=True)
        acc0 = jnp.dot(values(i), pm.astype(BF16), preferred_element_type=F32)

        def step(j, carry, bias):
            m_prev, l_prev, acc = carry
            s = scores(j, bias) + sel_ref[pl.ds(j, 1), :]
            m_new = jnp.maximum(m_prev, jnp.max(s, axis=0, keepdims=True))
            alpha = jnp.exp(m_prev - m_new)
            pm = jnp.exp(s - m_new)
            l_new = alpha * l_prev + jnp.sum(pm, axis=0, keepdims=True)
            acc = alpha * acc + jnp.dot(values(j), pm.astype(BF16), preferred_element_type=F32)
            return m_new, l_new, acc

        carry = step(jnp.maximum(i - 1, 0), (m0, l0, acc0), bp_ref[hh])
        far = far_ref[p * HEADS_PER_STEP + hh]
        _, l_fin, acc = lax.fori_loop(0, jnp.maximum(i - 1, 0),
                                      lambda j, c: step(j, c, far), carry)
        outs.append(acc * (1.0 / l_fin))
    o_ref[0] = jnp.concatenate(outs, axis=0).T.astype(BF16)


def _attention(qT, k, vT, kmean, bias_d, bias_p, bias_far):
    B, _, S = qT.shape
    nb = S // BLOCK
    lanes = HEADS_PER_STEP * HEAD_DIM
    grid = (B, N_HEADS // HEADS_PER_STEP, nb)
    return pl.pallas_call(
        _attn_body,
        grid=grid,
        in_specs=[
            pl.BlockSpec(memory_space=pltpu.SMEM),
            pl.BlockSpec((1, lanes, BLOCK), lambda b, p, i: (b, p, i)),
            pl.BlockSpec((1, S, lanes), lambda b, p, i: (b, 0, p)),
            pl.BlockSpec((1, lanes, S), lambda b, p, i: (b, p, 0)),
            pl.BlockSpec((1, nb, lanes), lambda b, p, i: (b, 0, p)),
            pl.BlockSpec((HEADS_PER_STEP, BLOCK, BLOCK), lambda b, p, i: (p, 0, 0)),
            pl.BlockSpec((HEADS_PER_STEP, BLOCK, BLOCK), lambda b, p, i: (p, 0, 0)),
        ],
        out_specs=pl.BlockSpec((1, BLOCK, lanes), lambda b, p, i: (b, i, p)),
        out_shape=jax.ShapeDtypeStruct((B, S, D_ATTN), BF16),
        scratch_shapes=[pltpu.VMEM((nb, BLOCK), F32)],
        compiler_params=pltpu.CompilerParams(
            dimension_semantics=("parallel", "parallel", "arbitrary"), vmem_limit_bytes=VMEM_LIMIT),
        name="moba_attn",
    )(bias_far, qT, k, vT, kmean, bias_d, bias_p)


def _rnn_body(xr_ref, yr_ref, cw_ref, cb_ref, wa_ref, wi_ref, ba_ref, bi_ref, lam_ref,
              o_ref, xp_ref, a_ref, u_ref, h_ref, hc_ref):
    tr = xr_ref.shape[1]
    d = xr_ref.shape[2]

    @pl.when(pl.program_id(1) == 0)
    def _():
        xp_ref[0:SUBLANES, :] = jnp.zeros((SUBLANES, d), F32)
        hc_ref[...] = jnp.zeros_like(hc_ref)

    xp_ref[SUBLANES:, :] = xr_ref[0].astype(F32)
    xc = cb_ref[...] + xp_ref[pl.ds(SUBLANES - (CONV_WIDTH - 1), tr), :] * cw_ref[0:1, :]
    for j in range(1, CONV_WIDTH):
        xc = xc + xp_ref[pl.ds(SUBLANES - (CONV_WIDTH - 1) + j, tr), :] * cw_ref[j:j + 1, :]
    xp_ref[0:SUBLANES, :] = xp_ref[tr:tr + SUBLANES, :]

    xb = xc.astype(BF16)
    ra, ia = [], []
    for g in range(d // GROUP):
        xg = xb[:, g * GROUP:(g + 1) * GROUP]
        ra.append(jnp.dot(xg, wa_ref[g], preferred_element_type=F32))
        ia.append(jnp.dot(xg, wi_ref[g], preferred_element_type=F32))
    r = jax.nn.sigmoid(jnp.concatenate(ra, axis=1) + ba_ref[...])
    ig = jax.nn.sigmoid(jnp.concatenate(ia, axis=1) + bi_ref[...])
    nlam = -lam_ref[...]
    softplus = jnp.maximum(nlam, 0.0) + jnp.log1p(jnp.exp(-jnp.abs(nlam)))
    log_a = (-LRU_C) * r * softplus
    a = jnp.exp(log_a)
    u = jnp.sqrt(1.0 - jnp.exp(2.0 * log_a)) * (ig * xc)

    rowm = lax.broadcasted_iota(jnp.int32, (tr, d), 0) & (SUBLANES - 1)
    sh = 1
    while sh < SUBLANES:
        keep = rowm >= sh
        a_s = pltpu.roll(a, sh, 0)
        u_s = pltpu.roll(u, sh, 0)
        u = jnp.where(keep, a * u_s + u, u)
        a = jnp.where(keep, a * a_s, a)
        sh *= 2
    a_ref[...] = a
    u_ref[...] = u

    def group(g, hc):
        rows = pl.ds(pl.multiple_of(g * SUBLANES, SUBLANES), SUBLANES)
        h = a_ref[rows, :] * hc + u_ref[rows, :]
        h_ref[rows, :] = h
        return h[SUBLANES - 1:SUBLANES, :]

    hc_ref[...] = lax.fori_loop(0, tr // SUBLANES, group, hc_ref[...])

    y = yr_ref[0].astype(F32)
    gelu = 0.5 * y * (1.0 + jnp.tanh(math.sqrt(2.0 / math.pi) * (y + 0.044715 * (y * y * y))))
    o_ref[0] = (h_ref[...] * gelu).astype(BF16)


def _rnn(rest, conv_w, conv_b, wa_bd, wi_bd, b_a, b_i, lam, tr):
    B, S, _ = rest.shape
    d = conv_w.shape[1]
    grid = (B, S // tr)
    return pl.pallas_call(
        _rnn_body,
        grid=grid,
        in_specs=[
            pl.BlockSpec((1, tr, d), lambda b, s: (b, s, 0)),
            pl.BlockSpec((1, tr, d), lambda b, s: (b, s, 1)),
            _const_spec(conv_w.shape), _const_spec(conv_b.shape), _const_spec(wa_bd.shape),
            _const_spec(wi_bd.shape), _const_spec(b_a.shape), _const_spec(b_i.shape),
            _const_spec(lam.shape),
        ],
        out_specs=pl.BlockSpec((1, tr, d), lambda b, s: (b, s, 0)),
        out_shape=jax.ShapeDtypeStruct((B, S, d), BF16),
        scratch_shapes=[
            pltpu.VMEM((tr + SUBLANES, d), F32),
            pltpu.VMEM((tr, d), F32), pltpu.VMEM((tr, d), F32), pltpu.VMEM((tr, d), F32),
            pltpu.VMEM((1, d), F32),
        ],
        compiler_params=pltpu.CompilerParams(
            dimension_semantics=("parallel", "arbitrary"), vmem_limit_bytes=VMEM_LIMIT),
        name="rglru",
    )(rest, rest, conv_w, conv_b, wa_bd, wi_bd, b_a, b_i, lam)


def _out_body(x_ref, oa_ref, or_ref, ga_ref, gr_ref, bg_ref, pa_ref, pr_ref, wo_ref, n2_ref,
              w1_ref, w2_ref, y_ref):
    a = jnp.dot(oa_ref[...], pa_ref[...], preferred_element_type=F32)
    r = jnp.dot(or_ref[...], pr_ref[...], preferred_element_type=F32)
    g_a = jax.nn.sigmoid(ga_ref[...].astype(F32) + bg_ref[0:1, :])
    g_r = jax.nn.sigmoid(gr_ref[...].astype(F32) + bg_ref[1:2, :])
    merged = (g_a * a + g_r * r).astype(BF16)
    x1 = x_ref[...] + jnp.dot(merged, wo_ref[...], preferred_element_type=F32)
    ms = jnp.mean(x1 * x1, axis=-1, keepdims=True)
    h2 = (x1 * lax.rsqrt(ms + EPS) * n2_ref[...]).astype(BF16)
    d_ff = w1_ref.shape[1]
    cw = 1024
    acc = x1
    for c in range(d_ff // cw):
        t = jnp.dot(h2, w1_ref[:, c * cw:(c + 1) * cw], preferred_element_type=F32)
        t = jnp.maximum(t, 0.0)
        acc = acc + jnp.dot((t * t).astype(BF16), w2_ref[c * cw:(c + 1) * cw, :],
                            preferred_element_type=F32)
    y_ref[...] = acc


def _out(x2, o_attn, o_rnn, rest, bg, pa, pr, wo, n2, w1, w2, tm):
    T, D = x2.shape
    d_rnn = o_rnn.shape[1]
    grid = (T // tm,)
    return pl.pallas_call(
        _out_body,
        grid=grid,
        in_specs=[
            pl.BlockSpec((tm, D), lambda t: (t, 0)),
            pl.BlockSpec((tm, D_ATTN), lambda t: (t, 0)),
            pl.BlockSpec((tm, d_rnn), lambda t: (t, 0)),
            pl.BlockSpec((tm, D), lambda t: (t, 2)),
            pl.BlockSpec((tm, D), lambda t: (t, 3)),
            _const_spec(bg.shape), _const_spec(pa.shape), _const_spec(pr.shape),
            _const_spec(wo.shape), _const_spec(n2.shape), _const_spec(w1.shape),
            _const_spec(w2.shape),
        ],
        out_specs=pl.BlockSpec((tm, D), lambda t: (t, 0)),
        out_shape=jax.ShapeDtypeStruct((T, D), F32),
        compiler_params=pltpu.CompilerParams(
            dimension_semantics=("parallel",), vmem_limit_bytes=VMEM_LIMIT),
        name="merge_mlp",
    )(x2, o_attn, o_rnn, rest, rest, bg, pa, pr, wo, n2, w1, w2)


def _t5_bucket(rel):
    max_exact = NUM_BUCKETS // 2
    n = jnp.maximum(rel, 0)
    nf = jnp.maximum(n, 1).astype(F32)
    large = max_exact + (jnp.log(nf / max_exact) / math.log(MAX_DISTANCE / max_exact)
                         * (NUM_BUCKETS - max_exact)).astype(jnp.int32)
    large = jnp.minimum(large, NUM_BUCKETS - 1)
    return jnp.where(n < max_exact, n, large)


def _bias_tables(rel_bias):
    offs = jnp.arange(BLOCK)
    rel_own = offs[None, :] - offs[:, None]
    tab = rel_bias.T
    bias_d = jnp.where(rel_own >= 0, tab[:, _t5_bucket(rel_own)], NEG)
    bias_p = tab[:, _t5_bucket(rel_own + BLOCK)]
    bias_far = tab[:, NUM_BUCKETS - 1]
    return bias_d.astype(F32), bias_p.astype(F32), bias_far.astype(F32)


def _block_diag(w):
    per = GROUP // RNN_BLOCK_DIM
    w4 = w.reshape(RNN_BLOCKS // per, per, RNN_BLOCK_DIM, RNN_BLOCK_DIM)
    eye = jnp.eye(per, dtype=w.dtype)
    return jnp.einsum('gade,ab->gadbe', w4, eye).reshape(RNN_BLOCKS // per, GROUP, GROUP)


def _layer(x, norm1_w, w_in, b_gate, q_norm_w, k_norm_w, bias_tabs, conv_w, conv_b,
           w_rg_a, b_rg_a, w_rg_i, b_rg_i, lru_lambda, w_proj_attn, w_proj_rnn,
           w_out, norm2_w, w_ff1, w_ff2, tm_in, tr, tm_out):
    B, S, D = x.shape
    d_rnn = conv_w.shape[1]
    c0, c1, c2 = D_ATTN, 2 * D_ATTN, 3 * D_ATTN
    wqT = w_in[:, :c0].T.astype(BF16)
    wk = w_in[:, c0:c1].astype(BF16)
    wvT = w_in[:, c1:c2].T.astype(BF16)
    wr = w_in[:, c2:].astype(BF16)
    qw = (q_norm_w * (HEAD_DIM ** -0.5)).reshape(1, HEAD_DIM, 1)
    kw = jnp.tile(k_norm_w, N_HEADS).reshape(1, D_ATTN)
    head_of = np.arange(D_ATTN) // HEAD_DIM
    gmat = jnp.asarray(head_of[:, None] == head_of[None, :], dtype=BF16)

    qT, k, vT, kmean, rest = _inproj(x, norm1_w.reshape(1, D), wqT, wk, wvT, wr, qw, kw, gmat, tm_in)
    o_attn = _attention(qT, k, vT, kmean.reshape(B, S // BLOCK, D_ATTN), *bias_tabs)
    o_rnn = _rnn(rest, conv_w, conv_b.reshape(1, d_rnn),
                 _block_diag(w_rg_a).astype(BF16), _block_diag(w_rg_i).astype(BF16),
                 b_rg_a.reshape(1, d_rnn), b_rg_i.reshape(1, d_rnn), lru_lambda.reshape(1, d_rnn), tr)
    y = _out(x.reshape(B * S, D), o_attn.reshape(B * S, D_ATTN), o_rnn.reshape(B * S, d_rnn),
             rest.reshape(B * S, -1), b_gate, w_proj_attn.astype(BF16), w_proj_rnn.astype(BF16),
             w_out.astype(BF16), norm2_w.reshape(1, D), w_ff1.astype(BF16), w_ff2.astype(BF16), tm_out)
    return y.reshape(B, S, D)


def kernel(x, norm1_w, w_in, b_gate, q_norm_w, k_norm_w, rel_bias, conv_w, conv_b, w_rg_a, b_rg_a,
           w_rg_i, b_rg_i, lru_lambda, w_proj_attn, w_proj_rnn, w_out, norm2_w, w_ff1, w_ff2):
    S = x.shape[1]
    assert S % BLOCK == 0
    bias_tabs = _bias_tables(rel_bias)
    tm_in = min(512, S)
    tr = min(256, S)
    tm_out = min(256, S)
    for l in range(norm1_w.shape[0]):
        x = _layer(x, norm1_w[l], w_in[l], b_gate[l], q_norm_w[l], k_norm_w[l], bias_tabs,
                   conv_w[l], conv_b[l], w_rg_a[l], b_rg_a[l], w_rg_i[l], b_rg_i[l], lru_lambda[l],
                   w_proj_attn[l], w_proj_rnn[l], w_out[l], norm2_w[l], w_ff1[l], w_ff2[l],
                   tm_in, tr, tm_out)
    return x
```

```python
import functools
import math

import numpy as np
import jax
import jax.numpy as jnp
from jax import lax
from jax.experimental import pallas as pl
from jax.experimental.pallas import tpu as pltpu

N_HEADS = 8
HEAD_DIM = 64
D_ATTN = N_HEADS * HEAD_DIM
BLOCK = 256
TOPK = 3
NUM_BUCKETS = 32
MAX_DISTANCE = 128
RNN_BLOCKS = 16
RNN_BLOCK_DIM = 64
CONV_WIDTH = 4
LRU_C = 8.0
EPS = 1e-6
NEG = -1e30
LOG2E = math.log2(math.e)
HEADS_PER_STEP = 4
GROUP = 256
SUBLANES = 8
VMEM_LIMIT = 56 * 1024 * 1024

F32 = jnp.float32
BF16 = jnp.bfloat16
_NT = (((1,), (1,)), ((), ()))


def _const_spec(shape):
    nd = len(shape)
    return pl.BlockSpec(shape, lambda *_: (0,) * nd, pipeline_mode=pl.Buffered(1))


def _inproj_body(x_ref, n1_ref, wqT_ref, wk_ref, wvT_ref, wr_ref, qw_ref, kw_ref, g_ref,
                 qT_ref, k_ref, vT_ref, km_ref, rest_ref):
    tm = x_ref.shape[1]
    x = x_ref[0]
    ms = jnp.mean(x * x, axis=-1, keepdims=True)
    h = (x * lax.rsqrt(ms + EPS) * n1_ref[...]).astype(BF16)

    qT = lax.dot_general(wqT_ref[...], h, _NT, preferred_element_type=F32)
    q3 = qT.reshape(N_HEADS, HEAD_DIM, tm)
    qss = jnp.mean(q3 * q3, axis=1, keepdims=True)
    qn = q3 * lax.rsqrt(qss + EPS) * qw_ref[...]
    qT_ref[0] = qn.reshape(D_ATTN, tm).astype(BF16)
    vT_ref[0] = lax.dot_general(wvT_ref[...], h, _NT, preferred_element_type=F32).astype(BF16)

    k = jnp.dot(h, wk_ref[...], preferred_element_type=F32)
    k2 = k * k
    hi = k2.astype(BF16)
    lo = (k2 - hi.astype(F32)).astype(BF16)
    kss = (jnp.dot(hi, g_ref[...], preferred_element_type=F32)
           + jnp.dot(lo, g_ref[...], preferred_element_type=F32))
    kn = k * lax.rsqrt(kss * (1.0 / HEAD_DIM) + EPS) * kw_ref[...]
    k_ref[0] = kn.astype(BF16)
    for bi in range(tm // BLOCK):
        km_ref[0, bi] = jnp.mean(kn[bi * BLOCK:(bi + 1) * BLOCK], axis=0, keepdims=True)

    d_rest = wr_ref.shape[1]
    cw = 1024
    for c in range(d_rest // cw):
        rest_ref[0, :, c * cw:(c + 1) * cw] = jnp.dot(
            h, wr_ref[:, c * cw:(c + 1) * cw], preferred_element_type=F32).astype(BF16)


def _inproj(x, n1, wqT, wk, wvT, wr, qw, kw, gmat, tm):
    B, S, D = x.shape
    d_rest = wr.shape[1]
    nb = S // BLOCK
    grid = (B, S // tm)
    return pl.pallas_call(
        _inproj_body,
        grid=grid,
        in_specs=[
            pl.BlockSpec((1, tm, D), lambda b, s: (b, s, 0)),
            _const_spec(n1.shape), _const_spec(wqT.shape), _const_spec(wk.shape),
            _const_spec(wvT.shape), _const_spec(wr.shape), _const_spec(qw.shape),
            _const_spec(kw.shape), _const_spec(gmat.shape),
        ],
        out_specs=[
            pl.BlockSpec((1, D_ATTN, tm), lambda b, s: (b, 0, s)),
            pl.BlockSpec((1, tm, D_ATTN), lambda b, s: (b, s, 0)),
            pl.BlockSpec((1, D_ATTN, tm), lambda b, s: (b, 0, s)),
            pl.BlockSpec((1, tm // BLOCK, 1, D_ATTN), lambda b, s: (b, s, 0, 0)),
            pl.BlockSpec((1, tm, d_rest), lambda b, s: (b, s, 0)),
        ],
        out_shape=[
            jax.ShapeDtypeStruct((B, D_ATTN, S), BF16),
            jax.ShapeDtypeStruct((B, S, D_ATTN), BF16),
            jax.ShapeDtypeStruct((B, D_ATTN, S), BF16),
            jax.ShapeDtypeStruct((B, nb, 1, D_ATTN), F32),
            jax.ShapeDtypeStruct((B, S, d_rest), BF16),
        ],
        compiler_params=pltpu.CompilerParams(
            dimension_semantics=("parallel", "parallel"), vmem_limit_bytes=VMEM_LIMIT),
        name="inproj",
    )(x, n1, wqT, wk, wvT, wr, qw, kw, gmat)


def _attn_body(far_ref, qT_ref, k_ref, vT_ref, km_ref, bd_ref, bp_ref, o_ref, sel_ref):
    p = pl.program_id(1)
    i = pl.program_id(2)
    nb = km_ref.shape[1]
    hps = bd_ref.shape[0]
    lanes = hps * HEAD_DIM
    heads = range(hps)
    qall = qT_ref[0]
    row = lax.broadcasted_iota(jnp.int32, (lanes, BLOCK), 0)
    blk = lax.broadcasted_iota(jnp.int32, (nb, BLOCK), 0)
    past = blk < i
    qz = [jnp.where((row >= hh * HEAD_DIM) & (row < (hh + 1) * HEAD_DIM), qall, jnp.zeros_like(qall))
          for hh in heads]

    for hh in heads:
        gate = jnp.dot(km_ref[0], qz[hh].astype(F32), preferred_element_type=F32,
                       precision=lax.Precision.HIGHEST)
        gate = jnp.where(past, gate, -jnp.inf)
        rank = jnp.zeros((nb, BLOCK), jnp.int32)
        for m in range(nb):
            gm = gate[m:m + 1, :]
            rank = rank + jnp.where(gm > gate, 1, jnp.where((gm == gate) & (blk > m), 1, 0))
        sel_ref[hh] = jnp.where(past & (rank < TOPK), far_ref[p * hps + hh], NEG)

    def keys(j):
        return k_ref[0, pl.ds(pl.multiple_of(j * BLOCK, BLOCK), BLOCK), :]

    def values(j, hh):
        return vT_ref[0, hh * HEAD_DIM:(hh + 1) * HEAD_DIM,
                      pl.ds(pl.multiple_of(j * BLOCK, BLOCK), BLOCK)]

    def scores(j):
        kb = keys(j)
        return tuple(jnp.dot(kb, qz[hh], preferred_element_type=F32) for hh in heads)

    def update(j, qk, bias, state):
        part = []
        for hh in heads:
            m_prev, l_prev, _ = state[hh]
            s = qk[hh] + bias(hh)
            m_new = jnp.maximum(m_prev, jnp.max(s, axis=0, keepdims=True))
            alpha = jnp.exp2(m_prev - m_new)
            pm = jnp.exp2(s - m_new)
            l_new = alpha * l_prev + jnp.sum(pm, axis=0, keepdims=True)
            part.append((m_new, l_new, alpha, pm.astype(BF16)))
        new = []
        for hh in heads:
            m_new, l_new, alpha, pb = part[hh]
            acc = alpha * state[hh][2] + jnp.dot(values(j, hh), pb, preferred_element_type=F32)
            new.append((m_new, l_new, acc))
        return tuple(new)

    jp = jnp.maximum(i - 1, 0)
    qk_own = scores(i)
    qk_prev = scores(jp)
    state = tuple((jnp.full((1, BLOCK), NEG, F32), jnp.zeros((1, BLOCK), F32),
                   jnp.zeros((HEAD_DIM, BLOCK), F32)) for _ in heads)
    state = update(i, qk_own, lambda hh: bd_ref[hh], state)
    state = update(jp, qk_prev,
                   lambda hh: bp_ref[hh] + jnp.where(sel_ref[hh, pl.ds(jp, 1), :] > 0.5 * NEG, 0.0, NEG),
                   state)
    state = lax.fori_loop(
        0, jp, lambda j, st: update(j, scores(j), lambda hh: sel_ref[hh, pl.ds(j, 1), :], st), state)
    outs = [acc * (1.0 / l_fin) for (_, l_fin, acc) in state]
    o_ref[0] = jnp.concatenate(outs, axis=0).T.astype(BF16)


def _attention(qT, k, vT, kmean, bias_d, bias_p, bias_far, hps):
    B, _, S = qT.shape
    nb = S // BLOCK
    lanes = hps * HEAD_DIM
    grid = (B, N_HEADS // hps, nb)
    return pl.pallas_call(
        _attn_body,
        grid=grid,
        in_specs=[
            pl.BlockSpec(memory_space=pltpu.SMEM),
            pl.BlockSpec((1, lanes, BLOCK), lambda b, p, i: (b, p, i)),
            pl.BlockSpec((1, S, lanes), lambda b, p, i: (b, 0, p)),
            pl.BlockSpec((1, lanes, S), lambda b, p, i: (b, p, 0)),
            pl.BlockSpec((1, nb, lanes), lambda b, p, i: (b, 0, p)),
            pl.BlockSpec((hps, BLOCK, BLOCK), lambda b, p, i: (p, 0, 0)),
            pl.BlockSpec((hps, BLOCK, BLOCK), lambda b, p, i: (p, 0, 0)),
        ],
        out_specs=pl.BlockSpec((1, BLOCK, lanes), lambda b, p, i: (b, i, p)),
        out_shape=jax.ShapeDtypeStruct((B, S, D_ATTN), BF16),
        scratch_shapes=[pltpu.VMEM((hps, nb, BLOCK), F32)],
        compiler_params=pltpu.CompilerParams(
            dimension_semantics=("parallel", "parallel", "arbitrary"), vmem_limit_bytes=VMEM_LIMIT),
        name="moba_attn",
    )(bias_far, qT, k, vT, kmean, bias_d, bias_p)


def _rnn_body(xr_ref, yr_ref, cw_ref, cb_ref, wa_ref, wi_ref, ba_ref, bi_ref, lam_ref,
              o_ref, xp_ref, a_ref, u_ref, h_ref, hc_ref):
    tr = xr_ref.shape[1]
    d = xr_ref.shape[2]

    @pl.when(pl.program_id(1) == 0)
    def _():
        xp_ref[0:SUBLANES, :] = jnp.zeros((SUBLANES, d), F32)
        hc_ref[...] = jnp.zeros_like(hc_ref)

    xp_ref[SUBLANES:, :] = xr_ref[0].astype(F32)
    xc = cb_ref[...] + xp_ref[pl.ds(SUBLANES - (CONV_WIDTH - 1), tr), :] * cw_ref[0:1, :]
    for j in range(1, CONV_WIDTH):
        xc = xc + xp_ref[pl.ds(SUBLANES - (CONV_WIDTH - 1) + j, tr), :] * cw_ref[j:j + 1, :]
    xp_ref[0:SUBLANES, :] = xp_ref[tr:tr + SUBLANES, :]

    xb = xc.astype(BF16)
    ra, ia = [], []
    for g in range(d // GROUP):
        xg = xb[:, g * GROUP:(g + 1) * GROUP]
        ra.append(jnp.dot(xg, wa_ref[g], preferred_element_type=F32))
        ia.append(jnp.dot(xg, wi_ref[g], preferred_element_type=F32))
    r = jax.nn.sigmoid(jnp.concatenate(ra, axis=1) + ba_ref[...])
    ig = jax.nn.sigmoid(jnp.concatenate(ia, axis=1) + bi_ref[...])
    nlam = -lam_ref[...]
    softplus = jnp.maximum(nlam, 0.0) + jnp.log1p(jnp.exp(-jnp.abs(nlam)))
    log_a = (-LRU_C) * r * softplus
    a = jnp.exp(log_a)
    u = jnp.sqrt(1.0 - jnp.exp(2.0 * log_a)) * (ig * xc)

    rowm = lax.broadcasted_iota(jnp.int32, (tr, d), 0) & (SUBLANES - 1)
    sh = 1
    while sh < SUBLANES:
        keep = rowm >= sh
        a_s = pltpu.roll(a, sh, 0)
        u_s = pltpu.roll(u, sh, 0)
        u = jnp.where(keep, a * u_s + u, u)
        a = jnp.where(keep, a * a_s, a)
        sh *= 2
    a_ref[...] = a
    u_ref[...] = u

    def group(g, hc):
        rows = pl.ds(pl.multiple_of(g * SUBLANES, SUBLANES), SUBLANES)
        h = a_ref[rows, :] * hc + u_ref[rows, :]
        h_ref[rows, :] = h
        return h[SUBLANES - 1:SUBLANES, :]

    hc_ref[...] = lax.fori_loop(0, tr // SUBLANES, group, hc_ref[...])

    y = yr_ref[0].astype(F32)
    gelu = 0.5 * y * (1.0 + jnp.tanh(math.sqrt(2.0 / math.pi) * (y + 0.044715 * (y * y * y))))
    o_ref[0] = (h_ref[...] * gelu).astype(BF16)


def _rnn(rest, conv_w, conv_b, wa_bd, wi_bd, b_a, b_i, lam, tr):
    B, S, _ = rest.shape
    d = conv_w.shape[1]
    grid = (B, S // tr)
    return pl.pallas_call(
        _rnn_body,
        grid=grid,
        in_specs=[
            pl.BlockSpec((1, tr, d), lambda b, s: (b, s, 0)),
            pl.BlockSpec((1, tr, d), lambda b, s: (b, s, 1)),
            _const_spec(conv_w.shape), _const_spec(conv_b.shape), _const_spec(wa_bd.shape),
            _const_spec(wi_bd.shape), _const_spec(b_a.shape), _const_spec(b_i.shape),
            _const_spec(lam.shape),
        ],
        out_specs=pl.BlockSpec((1, tr, d), lambda b, s: (b, s, 0)),
        out_shape=jax.ShapeDtypeStruct((B, S, d), BF16),
        scratch_shapes=[
            pltpu.VMEM((tr + SUBLANES, d), F32),
            pltpu.VMEM((tr, d), F32), pltpu.VMEM((tr, d), F32), pltpu.VMEM((tr, d), F32),
            pltpu.VMEM((1, d), F32),
        ],
        compiler_params=pltpu.CompilerParams(
            dimension_semantics=("parallel", "arbitrary"), vmem_limit_bytes=VMEM_LIMIT),
        name="rglru",
    )(rest, rest, conv_w, conv_b, wa_bd, wi_bd, b_a, b_i, lam)


def _out_body(x_ref, oa_ref, or_ref, ga_ref, gr_ref, bg_ref, pa_ref, pr_ref, wo_ref, n2_ref,
              w1_ref, w2_ref, y_ref):
    a = jnp.dot(oa_ref[...], pa_ref[...], preferred_element_type=F32)
    r = jnp.dot(or_ref[...], pr_ref[...], preferred_element_type=F32)
    g_a = jax.nn.sigmoid(ga_ref[...].astype(F32) + bg_ref[0:1, :])
    g_r = jax.nn.sigmoid(gr_ref[...].astype(F32) + bg_ref[1:2, :])
    merged = (g_a * a + g_r * r).astype(BF16)
    x1 = x_ref[...] + jnp.dot(merged, wo_ref[...], preferred_element_type=F32)
    ms = jnp.mean(x1 * x1, axis=-1, keepdims=True)
    h2 = (x1 * lax.rsqrt(ms + EPS) * n2_ref[...]).astype(BF16)
    d_ff = w1_ref.shape[1]
    cw = 1024
    acc = x1
    for c in range(d_ff // cw):
        t = jnp.dot(h2, w1_ref[:, c * cw:(c + 1) * cw], preferred_element_type=F32)
        t = jnp.maximum(t, 0.0)
        acc = acc + jnp.dot((t * t).astype(BF16), w2_ref[c * cw:(c + 1) * cw, :],
                            preferred_element_type=F32)
    y_ref[...] = acc


def _out(x2, o_attn, o_rnn, rest, bg, pa, pr, wo, n2, w1, w2, tm):
    T, D = x2.shape
    d_rnn = o_rnn.shape[1]
    grid = (T // tm,)
    return pl.pallas_call(
        _out_body,
        grid=grid,
        in_specs=[
            pl.BlockSpec((tm, D), lambda t: (t, 0)),
            pl.BlockSpec((tm, D_ATTN), lambda t: (t, 0)),
            pl.BlockSpec((tm, d_rnn), lambda t: (t, 0)),
            pl.BlockSpec((tm, D), lambda t: (t, 2)),
            pl.BlockSpec((tm, D), lambda t: (t, 3)),
            _const_spec(bg.shape), _const_spec(pa.shape), _const_spec(pr.shape),
            _const_spec(wo.shape), _const_spec(n2.shape), _const_spec(w1.shape),
            _const_spec(w2.shape),
        ],
        out_specs=pl.BlockSpec((tm, D), lambda t: (t, 0)),
        out_shape=jax.ShapeDtypeStruct((T, D), F32),
        compiler_params=pltpu.CompilerParams(
            dimension_semantics=("parallel",), vmem_limit_bytes=VMEM_LIMIT),
        name="merge_mlp",
    )(x2, o_attn, o_rnn, rest, rest, bg, pa, pr, wo, n2, w1, w2)


def _t5_bucket(rel):
    max_exact = NUM_BUCKETS // 2
    n = jnp.maximum(rel, 0)
    nf = jnp.maximum(n, 1).astype(F32)
    large = max_exact + (jnp.log(nf / max_exact) / math.log(MAX_DISTANCE / max_exact)
                         * (NUM_BUCKETS - max_exact)).astype(jnp.int32)
    large = jnp.minimum(large, NUM_BUCKETS - 1)
    return jnp.where(n < max_exact, n, large)


def _bias_tables(rel_bias):
    offs = jnp.arange(BLOCK)
    rel_own = offs[None, :] - offs[:, None]
    buckets = jnp.arange(NUM_BUCKETS)

    def lookup(rel):
        onehot = (_t5_bucket(rel)[..., None] == buckets).astype(F32)
        return jnp.einsum('kqb,bh->hkq', onehot, rel_bias, precision=lax.Precision.HIGHEST)

    bias_d = jnp.where(rel_own >= 0, lookup(rel_own) * LOG2E, NEG)
    bias_p = lookup(rel_own + BLOCK) * LOG2E
    bias_far = rel_bias[NUM_BUCKETS - 1] * LOG2E
    return bias_d.astype(F32), bias_p.astype(F32), bias_far.astype(F32)


def _block_diag(w):
    per = GROUP // RNN_BLOCK_DIM
    w4 = w.reshape(RNN_BLOCKS // per, per, RNN_BLOCK_DIM, RNN_BLOCK_DIM)
    eye = jnp.eye(per, dtype=w.dtype)
    return jnp.einsum('gade,ab->gadbe', w4, eye).reshape(RNN_BLOCKS // per, GROUP, GROUP)


def _layer(x, norm1_w, w_in, b_gate, q_norm_w, k_norm_w, bias_tabs, conv_w, conv_b,
           w_rg_a, b_rg_a, w_rg_i, b_rg_i, lru_lambda, w_proj_attn, w_proj_rnn,
           w_out, norm2_w, w_ff1, w_ff2, tm_in, tr, tm_out):
    B, S, D = x.shape
    d_rnn = conv_w.shape[1]
    c0, c1, c2 = D_ATTN, 2 * D_ATTN, 3 * D_ATTN
    wqT = w_in[:, :c0].T.astype(BF16)
    wk = w_in[:, c0:c1].astype(BF16)
    wvT = w_in[:, c1:c2].T.astype(BF16)
    wr = w_in[:, c2:].astype(BF16)
    qw = (q_norm_w * (HEAD_DIM ** -0.5 * LOG2E)).reshape(1, HEAD_DIM, 1)
    kw = jnp.tile(k_norm_w, N_HEADS).reshape(1, D_ATTN)
    head_of = np.arange(D_ATTN) // HEAD_DIM
    gmat = jnp.asarray(head_of[:, None] == head_of[None, :], dtype=BF16)

    qT, k, vT, kmean, rest = _inproj(x, norm1_w.reshape(1, D), wqT, wk, wvT, wr, qw, kw, gmat, tm_in)
    o_attn = _attention(qT, k, vT, kmean.reshape(B, S // BLOCK, D_ATTN), *bias_tabs, HEADS_PER_STEP)
    o_rnn = _rnn(rest, conv_w, conv_b.reshape(1, d_rnn),
                 _block_diag(w_rg_a).astype(BF16), _block_diag(w_rg_i).astype(BF16),
                 b_rg_a.reshape(1, d_rnn), b_rg_i.reshape(1, d_rnn), lru_lambda.reshape(1, d_rnn), tr)
    y = _out(x.reshape(B * S, D), o_attn.reshape(B * S, D_ATTN), o_rnn.reshape(B * S, d_rnn),
             rest.reshape(B * S, -1), b_gate, w_proj_attn.astype(BF16), w_proj_rnn.astype(BF16),
             w_out.astype(BF16), norm2_w.reshape(1, D), w_ff1.astype(BF16), w_ff2.astype(BF16), tm_out)
    return y.reshape(B, S, D)


def kernel(x, norm1_w, w_in, b_gate, q_norm_w, k_norm_w, rel_bias, conv_w, conv_b, w_rg_a, b_rg_a,
           w_rg_i, b_rg_i, lru_lambda, w_proj_attn, w_proj_rnn, w_out, norm2_w, w_ff1, w_ff2):
    S = x.shape[1]
    assert S % BLOCK == 0
    bias_tabs = _bias_tables(rel_bias)
    tm_in = min(512, S)
    tr = min(256, S)
    tm_out = min(256, S)
    for l in range(norm1_w.shape[0]):
        x = _layer(x, norm1_w[l], w_in[l], b_gate[l], q_norm_w[l], k_norm_w[l], bias_tabs,
                   conv_w[l], conv_b[l], w_rg_a[l], b_rg_a[l], w_rg_i[l], b_rg_i[l], lru_lambda[l],
                   w_proj_attn[l], w_proj_rnn[l], w_out[l], norm2_w[l], w_ff1[l], w_ff2[l],
                   tm_in, tr, tm_out)
    return x
```

```python
import functools
import math

import numpy as np
import jax
import jax.numpy as jnp
from jax import lax
from jax.experimental import pallas as pl
from jax.experimental.pallas import tpu as pltpu

N_HEADS = 8
HEAD_DIM = 64
D_ATTN = N_HEADS * HEAD_DIM
BLOCK = 256
TOPK = 3
NUM_BUCKETS = 32
MAX_DISTANCE = 128
RNN_BLOCKS = 16
RNN_BLOCK_DIM = 64
CONV_WIDTH = 4
LRU_C = 8.0
EPS = 1e-6
NEG = -1e30
LOG2E = math.log2(math.e)
HEADS_PER_STEP = 4
GROUP = 256
SUBLANES = 8
VMEM_LIMIT = 56 * 1024 * 1024

F32 = jnp.float32
BF16 = jnp.bfloat16
_NT = (((1,), (1,)), ((), ()))


def _const_spec(shape):
    nd = len(shape)
    return pl.BlockSpec(shape, lambda *_: (0,) * nd, pipeline_mode=pl.Buffered(1))


def _inproj_body(x_ref, n1_ref, wqT_ref, wk_ref, wvT_ref, wr_ref, qw_ref, kw_ref, g_ref,
                 qT_ref, k_ref, vT_ref, km_ref, rest_ref):
    tm = x_ref.shape[1]
    x = x_ref[0]
    ms = jnp.mean(x * x, axis=-1, keepdims=True)
    h = (x * lax.rsqrt(ms + EPS) * n1_ref[...]).astype(BF16)

    qT = lax.dot_general(wqT_ref[...], h, _NT, preferred_element_type=F32)
    q3 = qT.reshape(N_HEADS, HEAD_DIM, tm)
    qss = jnp.mean(q3 * q3, axis=1, keepdims=True)
    qn = q3 * lax.rsqrt(qss + EPS) * qw_ref[...]
    qT_ref[0] = qn.reshape(D_ATTN, tm).astype(BF16)
    vT_ref[0] = lax.dot_general(wvT_ref[...], h, _NT, preferred_element_type=F32).astype(BF16)

    k = jnp.dot(h, wk_ref[...], preferred_element_type=F32)
    k2 = k * k
    hi = k2.astype(BF16)
    lo = (k2 - hi.astype(F32)).astype(BF16)
    kss = (jnp.dot(hi, g_ref[...], preferred_element_type=F32)
           + jnp.dot(lo, g_ref[...], preferred_element_type=F32))
    kn = k * lax.rsqrt(kss * (1.0 / HEAD_DIM) + EPS) * kw_ref[...]
    k_ref[0] = kn.astype(BF16)
    for bi in range(tm // BLOCK):
        km_ref[0, bi] = jnp.mean(kn[bi * BLOCK:(bi + 1) * BLOCK], axis=0, keepdims=True)

    d_rest = wr_ref.shape[1]
    cw = 1024
    for c in range(d_rest // cw):
        rest_ref[0, :, c * cw:(c + 1) * cw] = jnp.dot(
            h, wr_ref[:, c * cw:(c + 1) * cw], preferred_element_type=F32).astype(BF16)


def _inproj(x, n1, wqT, wk, wvT, wr, qw, kw, gmat, tm):
    B, S, D = x.shape
    d_rest = wr.shape[1]
    nb = S // BLOCK
    grid = (B, S // tm)
    return pl.pallas_call(
        _inproj_body,
        grid=grid,
        in_specs=[
            pl.BlockSpec((1, tm, D), lambda b, s: (b, s, 0)),
            _const_spec(n1.shape), _const_spec(wqT.shape), _const_spec(wk.shape),
            _const_spec(wvT.shape), _const_spec(wr.shape), _const_spec(qw.shape),
            _const_spec(kw.shape), _const_spec(gmat.shape),
        ],
        out_specs=[
            pl.BlockSpec((1, D_ATTN, tm), lambda b, s: (b, 0, s)),
            pl.BlockSpec((1, tm, D_ATTN), lambda b, s: (b, s, 0)),
            pl.BlockSpec((1, D_ATTN, tm), lambda b, s: (b, 0, s)),
            pl.BlockSpec((1, tm // BLOCK, 1, D_ATTN), lambda b, s: (b, s, 0, 0)),
            pl.BlockSpec((1, tm, d_rest), lambda b, s: (b, s, 0)),
        ],
        out_shape=[
            jax.ShapeDtypeStruct((B, D_ATTN, S), BF16),
            jax.ShapeDtypeStruct((B, S, D_ATTN), BF16),
            jax.ShapeDtypeStruct((B, D_ATTN, S), BF16),
            jax.ShapeDtypeStruct((B, nb, 1, D_ATTN), F32),
            jax.ShapeDtypeStruct((B, S, d_rest), BF16),
        ],
        compiler_params=pltpu.CompilerParams(
            dimension_semantics=("parallel", "parallel"), vmem_limit_bytes=VMEM_LIMIT),
        name="inproj",
    )(x, n1, wqT, wk, wvT, wr, qw, kw, gmat)


def _attn_body(far_ref, qT_ref, k_ref, vT_ref, km_ref, bd_ref, bp_ref, o_ref, sel_ref, qk_ref):
    p = pl.program_id(1)
    i = pl.program_id(2)
    nb = km_ref.shape[1]
    hps = bd_ref.shape[0]
    lanes = hps * HEAD_DIM
    heads = range(hps)
    qall = qT_ref[0]
    row = lax.broadcasted_iota(jnp.int32, (lanes, BLOCK), 0)
    blk = lax.broadcasted_iota(jnp.int32, (nb, BLOCK), 0)
    past = blk < i
    qz = [jnp.where((row >= hh * HEAD_DIM) & (row < (hh + 1) * HEAD_DIM), qall, jnp.zeros_like(qall))
          for hh in heads]

    km = km_ref[0]
    lane_head = lax.broadcasted_iota(jnp.int32, (nb, lanes), 1) // HEAD_DIM
    km_hi = km.astype(BF16)
    km_r = km - km_hi.astype(F32)
    km_mid = km_r.astype(BF16)
    km_lo = (km_r - km_mid.astype(F32)).astype(BF16)
    stacked = jnp.concatenate(
        [jnp.where(lane_head == hh, piece, jnp.zeros_like(piece))
         for piece in (km_hi, km_mid, km_lo) for hh in heads], axis=0)
    gates = jnp.dot(stacked, qall, preferred_element_type=F32)
    for hh in heads:
        gate = (gates[hh * nb:(hh + 1) * nb] + gates[(hps + hh) * nb:(hps + hh + 1) * nb]
                + gates[(2 * hps + hh) * nb:(2 * hps + hh + 1) * nb])
        gate = jnp.where(past, gate, -jnp.inf)
        rank = jnp.zeros((nb, BLOCK), jnp.int32)
        for m in range(nb):
            gm = gate[m:m + 1, :]
            rank = rank + jnp.where(gm > gate, 1, jnp.where((gm == gate) & (blk > m), 1, 0))
        sel_ref[hh] = jnp.where(past & (rank < TOPK), far_ref[p * hps + hh], NEG)

    def keys(j):
        return k_ref[0, pl.ds(pl.multiple_of(j * BLOCK, BLOCK), BLOCK), :]

    def values(j, hh):
        return vT_ref[0, hh * HEAD_DIM:(hh + 1) * HEAD_DIM,
                      pl.ds(pl.multiple_of(j * BLOCK, BLOCK), BLOCK)]

    def issue(slot, j):
        kb = keys(j)
        for hh in heads:
            qk_ref[slot, hh] = jnp.dot(kb, qz[hh], preferred_element_type=F32)

    def update(slot, j, bias, state):
        part = []
        for hh in heads:
            m_prev, l_prev, _ = state[hh]
            s = qk_ref[slot, hh] + bias(hh)
            m_new = jnp.maximum(m_prev, jnp.max(s, axis=0, keepdims=True))
            alpha = jnp.exp2(m_prev - m_new)
            pm = jnp.exp2(s - m_new)
            l_new = alpha * l_prev + jnp.sum(pm, axis=0, keepdims=True)
            part.append((m_new, l_new, alpha, pm.astype(BF16)))
        new = []
        for hh in heads:
            m_new, l_new, alpha, pb = part[hh]
            acc = alpha * state[hh][2] + jnp.dot(values(j, hh), pb, preferred_element_type=F32)
            new.append((m_new, l_new, acc))
        return tuple(new)

    jp = jnp.maximum(i - 1, 0)
    n_old = jp
    state = tuple((jnp.full((1, BLOCK), NEG, F32), jnp.zeros((1, BLOCK), F32),
                   jnp.zeros((HEAD_DIM, BLOCK), F32)) for _ in heads)
    issue(0, i)
    issue(1, jp)
    state = update(0, i, lambda hh: bd_ref[hh], state)
    issue(0, 0)
    state = update(1, jp,
                   lambda hh: bp_ref[hh] + jnp.where(sel_ref[hh, pl.ds(jp, 1), :] > 0.5 * NEG, 0.0, NEG),
                   state)

    def older_pair(t, st):
        ja = 2 * t
        jb = ja + 1
        issue(1, jb)
        st = update(0, ja, lambda hh: sel_ref[hh, pl.ds(ja, 1), :], st)
        issue(0, ja + 2)
        st = update(1, jb, lambda hh: jnp.where(jb < n_old, sel_ref[hh, pl.ds(jb, 1), :], NEG), st)
        return st

    state = lax.fori_loop(0, (n_old + 1) // 2, older_pair, state)
    outs = [acc * (1.0 / l_fin) for (_, l_fin, acc) in state]
    o_ref[0] = jnp.concatenate(outs, axis=0).T.astype(BF16)


def _attention(qT, k, vT, kmean, bias_d, bias_p, bias_far, hps):
    B, _, S = qT.shape
    nb = S // BLOCK
    lanes = hps * HEAD_DIM
    grid = (B, N_HEADS // hps, nb)
    return pl.pallas_call(
        _attn_body,
        grid=grid,
        in_specs=[
            pl.BlockSpec(memory_space=pltpu.SMEM),
            pl.BlockSpec((1, lanes, BLOCK), lambda b, p, i: (b, p, i)),
            pl.BlockSpec((1, S, lanes), lambda b, p, i: (b, 0, p)),
            pl.BlockSpec((1, lanes, S), lambda b, p, i: (b, p, 0)),
            pl.BlockSpec((1, nb, lanes), lambda b, p, i: (b, 0, p)),
            pl.BlockSpec((hps, BLOCK, BLOCK), lambda b, p, i: (p, 0, 0)),
            pl.BlockSpec((hps, BLOCK, BLOCK), lambda b, p, i: (p, 0, 0)),
        ],
        out_specs=pl.BlockSpec((1, BLOCK, lanes), lambda b, p, i: (b, i, p)),
        out_shape=jax.ShapeDtypeStruct((B, S, D_ATTN), BF16),
        scratch_shapes=[pltpu.VMEM((hps, nb, BLOCK), F32), pltpu.VMEM((2, hps, BLOCK, BLOCK), F32)],
        compiler_params=pltpu.CompilerParams(
            dimension_semantics=("parallel", "parallel", "arbitrary"), vmem_limit_bytes=VMEM_LIMIT),
        name="moba_attn",
    )(bias_far, qT, k, vT, kmean, bias_d, bias_p)


def _rnn_body(xr_ref, yr_ref, cw_ref, cb_ref, wa_ref, wi_ref, ba_ref, bi_ref, lam_ref,
              o_ref, xp_ref, a_ref, u_ref, h_ref, hc_ref):
    tr = xr_ref.shape[1]
    d = xr_ref.shape[2]

    @pl.when(pl.program_id(1) == 0)
    def _():
        xp_ref[0:SUBLANES, :] = jnp.zeros((SUBLANES, d), F32)
        hc_ref[...] = jnp.zeros_like(hc_ref)

    xp_ref[SUBLANES:, :] = xr_ref[0].astype(F32)
    xc = cb_ref[...] + xp_ref[pl.ds(SUBLANES - (CONV_WIDTH - 1), tr), :] * cw_ref[0:1, :]
    for j in range(1, CONV_WIDTH):
        xc = xc + xp_ref[pl.ds(SUBLANES - (CONV_WIDTH - 1) + j, tr), :] * cw_ref[j:j + 1, :]
    xp_ref[0:SUBLANES, :] = xp_ref[tr:tr + SUBLANES, :]

    xb = xc.astype(BF16)
    ra, ia = [], []
    for g in range(d // GROUP):
        xg = xb[:, g * GROUP:(g + 1) * GROUP]
        ra.append(jnp.dot(xg, wa_ref[g], preferred_element_type=F32))
        ia.append(jnp.dot(xg, wi_ref[g], preferred_element_type=F32))
    r = jax.nn.sigmoid(jnp.concatenate(ra, axis=1) + ba_ref[...])
    ig = jax.nn.sigmoid(jnp.concatenate(ia, axis=1) + bi_ref[...])
    nlam = -lam_ref[...]
    softplus = jnp.maximum(nlam, 0.0) + jnp.log1p(jnp.exp(-jnp.abs(nlam)))
    log_a = (-LRU_C) * r * softplus
    a = jnp.exp(log_a)
    u = jnp.sqrt(1.0 - jnp.exp(2.0 * log_a)) * (ig * xc)

    rowm = lax.broadcasted_iota(jnp.int32, (tr, d), 0) & (SUBLANES - 1)
    sh = 1
    while sh < SUBLANES:
        keep = rowm >= sh
        a_s = pltpu.roll(a, sh, 0)
        u_s = pltpu.roll(u, sh, 0)
        u = jnp.where(keep, a * u_s + u, u)
        a = jnp.where(keep, a * a_s, a)
        sh *= 2
    a_ref[...] = a
    u_ref[...] = u

    def group(g, hc):
        rows = pl.ds(pl.multiple_of(g * SUBLANES, SUBLANES), SUBLANES)
        h = a_ref[rows, :] * hc + u_ref[rows, :]
        h_ref[rows, :] = h
        return h[SUBLANES - 1:SUBLANES, :]

    hc_ref[...] = lax.fori_loop(0, tr // SUBLANES, group, hc_ref[...])

    y = yr_ref[0].astype(F32)
    gelu = 0.5 * y * (1.0 + jnp.tanh(math.sqrt(2.0 / math.pi) * (y + 0.044715 * (y * y * y))))
    o_ref[0] = (h_ref[...] * gelu).astype(BF16)


def _rnn(rest, conv_w, conv_b, wa_bd, wi_bd, b_a, b_i, lam, tr):
    B, S, _ = rest.shape
    d = conv_w.shape[1]
    grid = (B, S // tr)
    return pl.pallas_call(
        _rnn_body,
        grid=grid,
        in_specs=[
            pl.BlockSpec((1, tr, d), lambda b, s: (b, s, 0)),
            pl.BlockSpec((1, tr, d), lambda b, s: (b, s, 1)),
            _const_spec(conv_w.shape), _const_spec(conv_b.shape), _const_spec(wa_bd.shape),
            _const_spec(wi_bd.shape), _const_spec(b_a.shape), _const_spec(b_i.shape),
            _const_spec(lam.shape),
        ],
        out_specs=pl.BlockSpec((1, tr, d), lambda b, s: (b, s, 0)),
        out_shape=jax.ShapeDtypeStruct((B, S, d), BF16),
        scratch_shapes=[
            pltpu.VMEM((tr + SUBLANES, d), F32),
            pltpu.VMEM((tr, d), F32), pltpu.VMEM((tr, d), F32), pltpu.VMEM((tr, d), F32),
            pltpu.VMEM((1, d), F32),
        ],
        compiler_params=pltpu.CompilerParams(
            dimension_semantics=("parallel", "arbitrary"), vmem_limit_bytes=VMEM_LIMIT),
        name="rglru",
    )(rest, rest, conv_w, conv_b, wa_bd, wi_bd, b_a, b_i, lam)


def _out_body(x_ref, oa_ref, or_ref, ga_ref, gr_ref, bg_ref, pa_ref, pr_ref, wo_ref, n2_ref,
              w1_ref, w2_ref, y_ref):
    a = jnp.dot(oa_ref[...], pa_ref[...], preferred_element_type=F32)
    r = jnp.dot(or_ref[...], pr_ref[...], preferred_element_type=F32)
    g_a = jax.nn.sigmoid(ga_ref[...].astype(F32) + bg_ref[0:1, :])
    g_r = jax.nn.sigmoid(gr_ref[...].astype(F32) + bg_ref[1:2, :])
    merged = (g_a * a + g_r * r).astype(BF16)
    x1 = x_ref[...] + jnp.dot(merged, wo_ref[...], preferred_element_type=F32)
    ms = jnp.mean(x1 * x1, axis=-1, keepdims=True)
    h2 = (x1 * lax.rsqrt(ms + EPS) * n2_ref[...]).astype(BF16)
    d_ff = w1_ref.shape[1]
    cw = 1024
    acc = x1
    for c in range(d_ff // cw):
        t = jnp.dot(h2, w1_ref[:, c * cw:(c + 1) * cw], preferred_element_type=F32)
        t = jnp.maximum(t, 0.0)
        acc = acc + jnp.dot((t * t).astype(BF16), w2_ref[c * cw:(c + 1) * cw, :],
                            preferred_element_type=F32)
    y_ref[...] = acc


def _out(x2, o_attn, o_rnn, rest, bg, pa, pr, wo, n2, w1, w2, tm):
    T, D = x2.shape
    d_rnn = o_rnn.shape[1]
    grid = (T // tm,)
    return pl.pallas_call(
        _out_body,
        grid=grid,
        in_specs=[
            pl.BlockSpec((tm, D), lambda t: (t, 0)),
            pl.BlockSpec((tm, D_ATTN), lambda t: (t, 0)),
            pl.BlockSpec((tm, d_rnn), lambda t: (t, 0)),
            pl.BlockSpec((tm, D), lambda t: (t, 2)),
            pl.BlockSpec((tm, D), lambda t: (t, 3)),
            _const_spec(bg.shape), _const_spec(pa.shape), _const_spec(pr.shape),
            _const_spec(wo.shape), _const_spec(n2.shape), _const_spec(w1.shape),
            _const_spec(w2.shape),
        ],
        out_specs=pl.BlockSpec((tm, D), lambda t: (t, 0)),
        out_shape=jax.ShapeDtypeStruct((T, D), F32),
        compiler_params=pltpu.CompilerParams(
            dimension_semantics=("parallel",), vmem_limit_bytes=VMEM_LIMIT),
        name="merge_mlp",
    )(x2, o_attn, o_rnn, rest, rest, bg, pa, pr, wo, n2, w1, w2)


def _t5_bucket(rel):
    max_exact = NUM_BUCKETS // 2
    n = jnp.maximum(rel, 0)
    nf = jnp.maximum(n, 1).astype(F32)
    large = max_exact + (jnp.log(nf / max_exact) / math.log(MAX_DISTANCE / max_exact)
                         * (NUM_BUCKETS - max_exact)).astype(jnp.int32)
    large = jnp.minimum(large, NUM_BUCKETS - 1)
    return jnp.where(n < max_exact, n, large)


def _bias_tables(rel_bias):
    offs = jnp.arange(BLOCK)
    rel_own = offs[None, :] - offs[:, None]
    buckets = jnp.arange(NUM_BUCKETS)

    def lookup(rel):
        onehot = (_t5_bucket(rel)[..., None] == buckets).astype(F32)
        return jnp.einsum('kqb,bh->hkq', onehot, rel_bias, precision=lax.Precision.HIGHEST)

    bias_d = jnp.where(rel_own >= 0, lookup(rel_own) * LOG2E, NEG)
    bias_p = lookup(rel_own + BLOCK) * LOG2E
    bias_far = rel_bias[NUM_BUCKETS - 1] * LOG2E
    return bias_d.astype(F32), bias_p.astype(F32), bias_far.astype(F32)


def _block_diag(w):
    per = GROUP // RNN_BLOCK_DIM
    w4 = w.reshape(RNN_BLOCKS // per, per, RNN_BLOCK_DIM, RNN_BLOCK_DIM)
    eye = jnp.eye(per, dtype=w.dtype)
    return jnp.einsum('gade,ab->gadbe', w4, eye).reshape(RNN_BLOCKS // per, GROUP, GROUP)


def _layer(x, norm1_w, w_in, b_gate, q_norm_w, k_norm_w, bias_tabs, conv_w, conv_b,
           w_rg_a, b_rg_a, w_rg_i, b_rg_i, lru_lambda, w_proj_attn, w_proj_rnn,
           w_out, norm2_w, w_ff1, w_ff2, tm_in, tr, tm_out):
    B, S, D = x.shape
    d_rnn = conv_w.shape[1]
    c0, c1, c2 = D_ATTN, 2 * D_ATTN, 3 * D_ATTN
    wqT = w_in[:, :c0].T.astype(BF16)
    wk = w_in[:, c0:c1].astype(BF16)
    wvT = w_in[:, c1:c2].T.astype(BF16)
    wr = w_in[:, c2:].astype(BF16)
    qw = (q_norm_w * (HEAD_DIM ** -0.5 * LOG2E)).reshape(1, HEAD_DIM, 1)
    kw = jnp.tile(k_norm_w, N_HEADS).reshape(1, D_ATTN)
    head_of = np.arange(D_ATTN) // HEAD_DIM
    gmat = jnp.asarray(head_of[:, None] == head_of[None, :], dtype=BF16)

    qT, k, vT, kmean, rest = _inproj(x, norm1_w.reshape(1, D), wqT, wk, wvT, wr, qw, kw, gmat, tm_in)
    o_attn = _attention(qT, k, vT, kmean.reshape(B, S // BLOCK, D_ATTN), *bias_tabs, HEADS_PER_STEP)
    o_rnn = _rnn(rest, conv_w, conv_b.reshape(1, d_rnn),
                 _block_diag(w_rg_a).astype(BF16), _block_diag(w_rg_i).astype(BF16),
                 b_rg_a.reshape(1, d_rnn), b_rg_i.reshape(1, d_rnn), lru_lambda.reshape(1, d_rnn), tr)
    y = _out(x.reshape(B * S, D), o_attn.reshape(B * S, D_ATTN), o_rnn.reshape(B * S, d_rnn),
             rest.reshape(B * S, -1), b_gate, w_proj_attn.astype(BF16), w_proj_rnn.astype(BF16),
             w_out.astype(BF16), norm2_w.reshape(1, D), w_ff1.astype(BF16), w_ff2.astype(BF16), tm_out)
    return y.reshape(B, S, D)


def kernel(x, norm1_w, w_in, b_gate, q_norm_w, k_norm_w, rel_bias, conv_w, conv_b, w_rg_a, b_rg_a,
           w_rg_i, b_rg_i, lru_lambda, w_proj_attn, w_proj_rnn, w_out, norm2_w, w_ff1, w_ff2):
    S = x.shape[1]
    assert S % BLOCK == 0
    bias_tabs = _bias_tables(rel_bias)
    tm_in = min(512, S)
    tr = min(256, S)
    tm_out = min(256, S)
    for l in range(norm1_w.shape[0]):
        x = _layer(x, norm1_w[l], w_in[l], b_gate[l], q_norm_w[l], k_norm_w[l], bias_tabs,
                   conv_w[l], conv_b[l], w_rg_a[l], b_rg_a[l], w_rg_i[l], b_rg_i[l], lru_lambda[l],
                   w_proj_attn[l], w_proj_rnn[l], w_out[l], norm2_w[l], w_ff1[l], w_ff2[l],
                   tm_in, tr, tm_out)
    return x
```

```python
import functools
import math

import numpy as np
import jax
import jax.numpy as jnp
from jax import lax
from jax.experimental import pallas as pl
from jax.experimental.pallas import tpu as pltpu

N_HEADS = 8
HEAD_DIM = 64
D_ATTN = N_HEADS * HEAD_DIM
BLOCK = 256
TOPK = 3
NUM_BUCKETS = 32
MAX_DISTANCE = 128
RNN_BLOCKS = 16
RNN_BLOCK_DIM = 64
CONV_WIDTH = 4
LRU_C = 8.0
EPS = 1e-6
NEG = -1e30
LOG2E = math.log2(math.e)
HEADS_PER_STEP = 4
GROUP = 256
SUBLANES = 8
VMEM_LIMIT = 56 * 1024 * 1024

F32 = jnp.float32
BF16 = jnp.bfloat16
_NT = (((1,), (1,)), ((), ()))


def _const_spec(shape):
    nd = len(shape)
    return pl.BlockSpec(shape, lambda *_: (0,) * nd, pipeline_mode=pl.Buffered(1))


def _inproj_body(x_ref, n1_ref, wqT_ref, wk_ref, wvT_ref, wr_ref, qw_ref, kw_ref, g_ref,
                 qT_ref, k_ref, vT_ref, km_ref, rest_ref):
    tm = x_ref.shape[1]
    x = x_ref[0]
    ms = jnp.mean(x * x, axis=-1, keepdims=True)
    h = (x * lax.rsqrt(ms + EPS) * n1_ref[...]).astype(BF16)

    qT = lax.dot_general(wqT_ref[...], h, _NT, preferred_element_type=F32)
    q3 = qT.reshape(N_HEADS, HEAD_DIM, tm)
    qss = jnp.mean(q3 * q3, axis=1, keepdims=True)
    qn = q3 * lax.rsqrt(qss + EPS) * qw_ref[...]
    qT_ref[0] = qn.reshape(D_ATTN, tm).astype(BF16)
    vT_ref[0] = lax.dot_general(wvT_ref[...], h, _NT, preferred_element_type=F32).astype(BF16)

    k = jnp.dot(h, wk_ref[...], preferred_element_type=F32)
    k2 = k * k
    hi = k2.astype(BF16)
    lo = (k2 - hi.astype(F32)).astype(BF16)
    kss = (jnp.dot(hi, g_ref[...], preferred_element_type=F32)
           + jnp.dot(lo, g_ref[...], preferred_element_type=F32))
    kn = k * lax.rsqrt(kss * (1.0 / HEAD_DIM) + EPS) * kw_ref[...]
    k_ref[0] = kn.astype(BF16)
    for bi in range(tm // BLOCK):
        km_ref[0, bi] = jnp.mean(kn[bi * BLOCK:(bi + 1) * BLOCK], axis=0, keepdims=True)

    d_rest = wr_ref.shape[1]
    cw = 1024
    for c in range(d_rest // cw):
        rest_ref[0, :, c * cw:(c + 1) * cw] = jnp.dot(
            h, wr_ref[:, c * cw:(c + 1) * cw], preferred_element_type=F32).astype(BF16)


def _inproj(x, n1, wqT, wk, wvT, wr, qw, kw, gmat, tm):
    B, S, D = x.shape
    d_rest = wr.shape[1]
    nb = S // BLOCK
    grid = (B, S // tm)
    return pl.pallas_call(
        _inproj_body,
        grid=grid,
        in_specs=[
            pl.BlockSpec((1, tm, D), lambda b, s: (b, s, 0)),
            _const_spec(n1.shape), _const_spec(wqT.shape), _const_spec(wk.shape),
            _const_spec(wvT.shape), _const_spec(wr.shape), _const_spec(qw.shape),
            _const_spec(kw.shape), _const_spec(gmat.shape),
        ],
        out_specs=[
            pl.BlockSpec((1, D_ATTN, tm), lambda b, s: (b, 0, s)),
            pl.BlockSpec((1, tm, D_ATTN), lambda b, s: (b, s, 0)),
            pl.BlockSpec((1, D_ATTN, tm), lambda b, s: (b, 0, s)),
            pl.BlockSpec((1, tm // BLOCK, 1, D_ATTN), lambda b, s: (b, s, 0, 0)),
            pl.BlockSpec((1, tm, d_rest), lambda b, s: (b, s, 0)),
        ],
        out_shape=[
            jax.ShapeDtypeStruct((B, D_ATTN, S), BF16),
            jax.ShapeDtypeStruct((B, S, D_ATTN), BF16),
            jax.ShapeDtypeStruct((B, D_ATTN, S), BF16),
            jax.ShapeDtypeStruct((B, nb, 1, D_ATTN), F32),
            jax.ShapeDtypeStruct((B, S, d_rest), BF16),
        ],
        compiler_params=pltpu.CompilerParams(
            dimension_semantics=("parallel", "parallel"), vmem_limit_bytes=VMEM_LIMIT),
        name="inproj",
    )(x, n1, wqT, wk, wvT, wr, qw, kw, gmat)


def _attn_body(far_ref, qT_ref, k_ref, vT_ref, km_ref, bd_ref, bp_ref, o_ref, sel_ref, qk_ref):
    p = pl.program_id(1)
    i = pl.program_id(2)
    nb = km_ref.shape[1]
    hps = bd_ref.shape[0]
    lanes = hps * HEAD_DIM
    heads = range(hps)
    qall = qT_ref[0]
    row = lax.broadcasted_iota(jnp.int32, (lanes, BLOCK), 0)
    blk = lax.broadcasted_iota(jnp.int32, (nb, BLOCK), 0)
    past = blk < i
    qz = [jnp.where((row >= hh * HEAD_DIM) & (row < (hh + 1) * HEAD_DIM), qall, jnp.zeros_like(qall))
          for hh in heads]

    km = km_ref[0]
    lane_head = lax.broadcasted_iota(jnp.int32, (nb, lanes), 1) // HEAD_DIM
    km_hi = km.astype(BF16)
    km_r = km - km_hi.astype(F32)
    km_mid = km_r.astype(BF16)
    km_lo = (km_r - km_mid.astype(F32)).astype(BF16)
    stacked = jnp.concatenate(
        [jnp.where(lane_head == hh, piece, jnp.zeros_like(piece))
         for piece in (km_hi, km_mid, km_lo) for hh in heads], axis=0)
    gates = jnp.dot(stacked, qall, preferred_element_type=F32)
    for hh in heads:
        gate = (gates[hh * nb:(hh + 1) * nb] + gates[(hps + hh) * nb:(hps + hh + 1) * nb]
                + gates[(2 * hps + hh) * nb:(2 * hps + hh + 1) * nb])
        gate = jnp.where(past, gate, -jnp.inf)
        rank = jnp.zeros((nb, BLOCK), jnp.int32)
        for m in range(nb):
            gm = gate[m:m + 1, :]
            rank = rank + jnp.where(gm > gate, 1, jnp.where((gm == gate) & (blk > m), 1, 0))
        sel_ref[hh] = jnp.where(past & (rank < TOPK), far_ref[p * hps + hh], NEG)

    def keys(j):
        return k_ref[0, pl.ds(pl.multiple_of(j * BLOCK, BLOCK), BLOCK), :]

    def values(j, hh):
        return vT_ref[0, hh * HEAD_DIM:(hh + 1) * HEAD_DIM,
                      pl.ds(pl.multiple_of(j * BLOCK, BLOCK), BLOCK)]

    def issue(slot, j):
        kb = keys(j)
        for hh in heads:
            qk_ref[slot, hh] = jnp.dot(kb, qz[hh], preferred_element_type=F32)

    def update(slot, j, bias, state):
        part = []
        for hh in heads:
            m_prev, l_prev, _ = state[hh]
            s = qk_ref[slot, hh] + bias(hh)
            m_new = jnp.maximum(m_prev, jnp.max(s, axis=0, keepdims=True))
            alpha = jnp.exp2(m_prev - m_new)
            pm = jnp.exp2(s - m_new)
            l_new = alpha * l_prev + jnp.sum(pm, axis=0, keepdims=True)
            part.append((m_new, l_new, alpha, pm.astype(BF16)))
        new = []
        for hh in heads:
            m_new, l_new, alpha, pb = part[hh]
            acc = alpha * state[hh][2] + jnp.dot(values(j, hh), pb, preferred_element_type=F32)
            new.append((m_new, l_new, acc))
        return tuple(new)

    jp = jnp.maximum(i - 1, 0)
    n_old = jp
    state = tuple((jnp.full((1, BLOCK), NEG, F32), jnp.zeros((1, BLOCK), F32),
                   jnp.zeros((HEAD_DIM, BLOCK), F32)) for _ in heads)
    issue(0, i)
    issue(1, jp)
    state = update(0, i, lambda hh: bd_ref[hh], state)
    issue(0, 0)
    state = update(1, jp,
                   lambda hh: bp_ref[hh] + jnp.where(sel_ref[hh, pl.ds(jp, 1), :] > 0.5 * NEG, 0.0, NEG),
                   state)

    def older_pair(t, st):
        ja = 2 * t
        jb = ja + 1
        issue(1, jb)
        st = update(0, ja, lambda hh: sel_ref[hh, pl.ds(ja, 1), :], st)
        issue(0, ja + 2)
        st = update(1, jb, lambda hh: jnp.where(jb < n_old, sel_ref[hh, pl.ds(jb, 1), :], NEG), st)
        return st

    state = lax.fori_loop(0, (n_old + 1) // 2, older_pair, state)
    outs = [acc * (1.0 / l_fin) for (_, l_fin, acc) in state]
    o_ref[0] = jnp.concatenate(outs, axis=0).T.astype(BF16)


def _attention(qT, k, vT, kmean, bias_d, bias_p, bias_far, hps):
    B, _, S = qT.shape
    nb = S // BLOCK
    lanes = hps * HEAD_DIM
    grid = (B, N_HEADS // hps, nb)
    return pl.pallas_call(
        _attn_body,
        grid=grid,
        in_specs=[
            pl.BlockSpec(memory_space=pltpu.SMEM),
            pl.BlockSpec((1, lanes, BLOCK), lambda b, p, i: (b, p, i)),
            pl.BlockSpec((1, S, lanes), lambda b, p, i: (b, 0, p)),
            pl.BlockSpec((1, lanes, S), lambda b, p, i: (b, p, 0)),
            pl.BlockSpec((1, nb, lanes), lambda b, p, i: (b, 0, p)),
            pl.BlockSpec((hps, BLOCK, BLOCK), lambda b, p, i: (p, 0, 0)),
            pl.BlockSpec((hps, BLOCK, BLOCK), lambda b, p, i: (p, 0, 0)),
        ],
        out_specs=pl.BlockSpec((1, BLOCK, lanes), lambda b, p, i: (b, i, p)),
        out_shape=jax.ShapeDtypeStruct((B, S, D_ATTN), BF16),
        scratch_shapes=[pltpu.VMEM((hps, nb, BLOCK), F32), pltpu.VMEM((2, hps, BLOCK, BLOCK), F32)],
        compiler_params=pltpu.CompilerParams(
            dimension_semantics=("parallel", "parallel", "arbitrary"), vmem_limit_bytes=VMEM_LIMIT),
        name="moba_attn",
    )(bias_far, qT, k, vT, kmean, bias_d, bias_p)


def _rnn_body(xr_ref, yr_ref, perm_ref, permT_ref, cw_ref, cb_ref, wa_ref, wi_ref, ba_ref, bi_ref,
              lam_ref, o_ref, xs_ref, xc_ref, a_ref, u_ref, tail_ref, hc_ref):
    tr = xr_ref.shape[1]
    d = xr_ref.shape[2]
    seg = tr // SUBLANES
    taps = CONV_WIDTH - 1

    @pl.when(pl.program_id(1) == 0)
    def _():
        tail_ref[...] = jnp.zeros_like(tail_ref)
        hc_ref[...] = jnp.zeros_like(hc_ref)

    xs_ref[...] = jnp.dot(perm_ref[...], xr_ref[0], preferred_element_type=F32)

    def slab(k):
        return xs_ref[k * SUBLANES:(k + 1) * SUBLANES, :]

    first_seg = lax.broadcasted_iota(jnp.int32, (SUBLANES, d), 0) == 0
    last = [slab(seg - taps + j) for j in range(taps)]
    window = [jnp.where(first_seg, pltpu.roll(tail_ref[j], 1, 0), pltpu.roll(last[j], 1, 0))
              for j in range(taps)]
    for j in range(taps):
        tail_ref[j] = last[j]
    cw = [cw_ref[j:j + 1, :] for j in range(CONV_WIDTH)]
    cb = cb_ref[...]
    for k in range(seg):
        cur = slab(k)
        acc = cb + window[0] * cw[0]
        for j in range(1, taps):
            acc = acc + window[j] * cw[j]
        xc_ref[k * SUBLANES:(k + 1) * SUBLANES, :] = acc + cur * cw[taps]
        window = window[1:] + [cur]

    xc = xc_ref[...]
    xb = xc.astype(BF16)
    ra, ia = [], []
    for g in range(d // GROUP):
        xg = xb[:, g * GROUP:(g + 1) * GROUP]
        ra.append(jnp.dot(xg, wa_ref[g], preferred_element_type=F32))
        ia.append(jnp.dot(xg, wi_ref[g], preferred_element_type=F32))
    r = 1.0 / (1.0 + jnp.exp2((jnp.concatenate(ra, axis=1) + ba_ref[...]) * (-LOG2E)))
    ig = 1.0 / (1.0 + jnp.exp2((jnp.concatenate(ia, axis=1) + bi_ref[...]) * (-LOG2E)))
    nlam = -lam_ref[...]
    softplus = jnp.maximum(nlam, 0.0) + jnp.log1p(jnp.exp(-jnp.abs(nlam)))
    a = jnp.exp2(r * (softplus * (-LRU_C * LOG2E)))
    a_ref[...] = a
    gap = 1.0 - a * a
    u_ref[...] = (gap * lax.rsqrt(jnp.maximum(gap, 1e-30))) * (ig * xc)

    h = u_ref[0:SUBLANES, :]
    prod = a_ref[0:SUBLANES, :]
    for k in range(1, seg):
        rows = slice(k * SUBLANES, (k + 1) * SUBLANES)
        a_k = a_ref[rows, :]
        h = a_k * h + u_ref[rows, :]
        prod = a_k * prod
        u_ref[rows, :] = h
        a_ref[rows, :] = prod
    entry = [hc_ref[...]]
    for s in range(SUBLANES):
        entry.append(prod[s:s + 1, :] * entry[s] + h[s:s + 1, :])
    hc_ref[...] = entry[SUBLANES]
    h_in = jnp.concatenate(entry[:SUBLANES], axis=0)
    for k in range(seg):
        rows = slice(k * SUBLANES, (k + 1) * SUBLANES)
        u_ref[rows, :] = a_ref[rows, :] * h_in + u_ref[rows, :]

    y = jnp.dot(perm_ref[...], yr_ref[0], preferred_element_type=F32)
    k1 = -2.0 * math.sqrt(2.0 / math.pi) * LOG2E
    gelu = y / (1.0 + jnp.exp2(y * (k1 + (k1 * 0.044715) * (y * y))))
    o_slab = (u_ref[...] * gelu).astype(BF16)
    o_ref[0] = jnp.dot(permT_ref[...], o_slab, preferred_element_type=F32).astype(BF16)


def _rnn(rest, conv_w, conv_b, wa_bd, wi_bd, b_a, b_i, lam, tr):
    B, S, _ = rest.shape
    d = conv_w.shape[1]
    grid = (B, S // tr)
    seg = tr // SUBLANES
    t = np.arange(tr)
    perm_np = np.zeros((tr, tr), np.float32)
    perm_np[(t % seg) * SUBLANES + t // seg, t] = 1.0
    perm = jnp.asarray(perm_np, BF16)
    permT = jnp.asarray(perm_np.T, BF16)
    return pl.pallas_call(
        _rnn_body,
        grid=grid,
        in_specs=[
            pl.BlockSpec((1, tr, d), lambda b, s: (b, s, 0)),
            pl.BlockSpec((1, tr, d), lambda b, s: (b, s, 1)),
            _const_spec(perm.shape), _const_spec(permT.shape),
            _const_spec(conv_w.shape), _const_spec(conv_b.shape), _const_spec(wa_bd.shape),
            _const_spec(wi_bd.shape), _const_spec(b_a.shape), _const_spec(b_i.shape),
            _const_spec(lam.shape),
        ],
        out_specs=pl.BlockSpec((1, tr, d), lambda b, s: (b, s, 0)),
        out_shape=jax.ShapeDtypeStruct((B, S, d), BF16),
        scratch_shapes=[
            pltpu.VMEM((tr, d), F32), pltpu.VMEM((tr, d), F32), pltpu.VMEM((tr, d), F32), pltpu.VMEM((tr, d), F32),
            pltpu.VMEM((CONV_WIDTH - 1, SUBLANES, d), F32), pltpu.VMEM((1, d), F32),
        ],
        compiler_params=pltpu.CompilerParams(
            dimension_semantics=("parallel", "arbitrary"), vmem_limit_bytes=VMEM_LIMIT),
        name="rglru",
    )(rest, rest, perm, permT, conv_w, conv_b, wa_bd, wi_bd, b_a, b_i, lam)


def _out_body(x_ref, oa_ref, or_ref, ga_ref, gr_ref, bg_ref, pa_ref, pr_ref, wo_ref, n2_ref,
              w1_ref, w2_ref, y_ref):
    a = jnp.dot(oa_ref[...], pa_ref[...], preferred_element_type=F32)
    r = jnp.dot(or_ref[...], pr_ref[...], preferred_element_type=F32)
    g_a = jax.nn.sigmoid(ga_ref[...].astype(F32) + bg_ref[0:1, :])
    g_r = jax.nn.sigmoid(gr_ref[...].astype(F32) + bg_ref[1:2, :])
    merged = (g_a * a + g_r * r).astype(BF16)
    x1 = x_ref[...] + jnp.dot(merged, wo_ref[...], preferred_element_type=F32)
    ms = jnp.mean(x1 * x1, axis=-1, keepdims=True)
    h2 = (x1 * lax.rsqrt(ms + EPS) * n2_ref[...]).astype(BF16)
    d_ff = w1_ref.shape[1]
    cw = 1024
    acc = x1
    for c in range(d_ff // cw):
        t = jnp.dot(h2, w1_ref[:, c * cw:(c + 1) * cw], preferred_element_type=F32)
        t = jnp.maximum(t, 0.0)
        acc = acc + jnp.dot((t * t).astype(BF16), w2_ref[c * cw:(c + 1) * cw, :],
                            preferred_element_type=F32)
    y_ref[...] = acc


def _out(x2, o_attn, o_rnn, rest, bg, pa, pr, wo, n2, w1, w2, tm):
    T, D = x2.shape
    d_rnn = o_rnn.shape[1]
    grid = (T // tm,)
    return pl.pallas_call(
        _out_body,
        grid=grid,
        in_specs=[
            pl.BlockSpec((tm, D), lambda t: (t, 0)),
            pl.BlockSpec((tm, D_ATTN), lambda t: (t, 0)),
            pl.BlockSpec((tm, d_rnn), lambda t: (t, 0)),
            pl.BlockSpec((tm, D), lambda t: (t, 2)),
            pl.BlockSpec((tm, D), lambda t: (t, 3)),
            _const_spec(bg.shape), _const_spec(pa.shape), _const_spec(pr.shape),
            _const_spec(wo.shape), _const_spec(n2.shape), _const_spec(w1.shape),
            _const_spec(w2.shape),
        ],
        out_specs=pl.BlockSpec((tm, D), lambda t: (t, 0)),
        out_shape=jax.ShapeDtypeStruct((T, D), F32),
        compiler_params=pltpu.CompilerParams(
            dimension_semantics=("parallel",), vmem_limit_bytes=VMEM_LIMIT),
        name="merge_mlp",
    )(x2, o_attn, o_rnn, rest, rest, bg, pa, pr, wo, n2, w1, w2)


def _t5_bucket(rel):
    max_exact = NUM_BUCKETS // 2
    n = jnp.maximum(rel, 0)
    nf = jnp.maximum(n, 1).astype(F32)
    large = max_exact + (jnp.log(nf / max_exact) / math.log(MAX_DISTANCE / max_exact)
                         * (NUM_BUCKETS - max_exact)).astype(jnp.int32)
    large = jnp.minimum(large, NUM_BUCKETS - 1)
    return jnp.where(n < max_exact, n, large)


def _bias_tables(rel_bias):
    offs = jnp.arange(BLOCK)
    rel_own = offs[None, :] - offs[:, None]
    buckets = jnp.arange(NUM_BUCKETS)

    def lookup(rel):
        onehot = (_t5_bucket(rel)[..., None] == buckets).astype(F32)
        return jnp.einsum('kqb,bh->hkq', onehot, rel_bias, precision=lax.Precision.HIGHEST)

    bias_d = jnp.where(rel_own >= 0, lookup(rel_own) * LOG2E, NEG)
    bias_p = lookup(rel_own + BLOCK) * LOG2E
    bias_far = rel_bias[NUM_BUCKETS - 1] * LOG2E
    return bias_d.astype(F32), bias_p.astype(F32), bias_far.astype(F32)


def _block_diag(w):
    per = GROUP // RNN_BLOCK_DIM
    w4 = w.reshape(RNN_BLOCKS // per, per, RNN_BLOCK_DIM, RNN_BLOCK_DIM)
    eye = jnp.eye(per, dtype=w.dtype)
    return jnp.einsum('gade,ab->gadbe', w4, eye).reshape(RNN_BLOCKS // per, GROUP, GROUP)


def _layer(x, norm1_w, w_in, b_gate, q_norm_w, k_norm_w, bias_tabs, conv_w, conv_b,
           w_rg_a, b_rg_a, w_rg_i, b_rg_i, lru_lambda, w_proj_attn, w_proj_rnn,
           w_out, norm2_w, w_ff1, w_ff2, tm_in, tr, tm_out):
    B, S, D = x.shape
    d_rnn = conv_w.shape[1]
    c0, c1, c2 = D_ATTN, 2 * D_ATTN, 3 * D_ATTN
    wqT = w_in[:, :c0].T.astype(BF16)
    wk = w_in[:, c0:c1].astype(BF16)
    wvT = w_in[:, c1:c2].T.astype(BF16)
    wr = w_in[:, c2:].astype(BF16)
    qw = (q_norm_w * (HEAD_DIM ** -0.5 * LOG2E)).reshape(1, HEAD_DIM, 1)
    kw = jnp.tile(k_norm_w, N_HEADS).reshape(1, D_ATTN)
    head_of = np.arange(D_ATTN) // HEAD_DIM
    gmat = jnp.asarray(head_of[:, None] == head_of[None, :], dtype=BF16)

    qT, k, vT, kmean, rest = _inproj(x, norm1_w.reshape(1, D), wqT, wk, wvT, wr, qw, kw, gmat, tm_in)
    o_attn = _attention(qT, k, vT, kmean.reshape(B, S // BLOCK, D_ATTN), *bias_tabs, HEADS_PER_STEP)
    o_rnn = _rnn(rest, conv_w, conv_b.reshape(1, d_rnn),
                 _block_diag(w_rg_a).astype(BF16), _block_diag(w_rg_i).astype(BF16),
                 b_rg_a.reshape(1, d_rnn), b_rg_i.reshape(1, d_rnn), lru_lambda.reshape(1, d_rnn), tr)
    y = _out(x.reshape(B * S, D), o_attn.reshape(B * S, D_ATTN), o_rnn.reshape(B * S, d_rnn),
             rest.reshape(B * S, -1), b_gate, w_proj_attn.astype(BF16), w_proj_rnn.astype(BF16),
             w_out.astype(BF16), norm2_w.reshape(1, D), w_ff1.astype(BF16), w_ff2.astype(BF16), tm_out)
    return y.reshape(B, S, D)


def kernel(x, norm1_w, w_in, b_gate, q_norm_w, k_norm_w, rel_bias, conv_w, conv_b, w_rg_a, b_rg_a,
           w_rg_i, b_rg_i, lru_lambda, w_proj_attn, w_proj_rnn, w_out, norm2_w, w_ff1, w_ff2):
    S = x.shape[1]
    assert S % BLOCK == 0
    bias_tabs = _bias_tables(rel_bias)
    tm_in = min(512, S)
    tr = min(256, S)
    tm_out = min(256, S)
    for l in range(norm1_w.shape[0]):
        x = _layer(x, norm1_w[l], w_in[l], b_gate[l], q_norm_w[l], k_norm_w[l], bias_tabs,
                   conv_w[l], conv_b[l], w_rg_a[l], b_rg_a[l], w_rg_i[l], b_rg_i[l], lru_lambda[l],
                   w_proj_attn[l], w_proj_rnn[l], w_out[l], norm2_w[l], w_ff1[l], w_ff2[l],
                   tm_in, tr, tm_out)
    return x
```

```python
import functools
import math

import numpy as np
import jax
import jax.numpy as jnp
from jax import lax
from jax.experimental import pallas as pl
from jax.experimental.pallas import tpu as pltpu

N_HEADS = 8
HEAD_DIM = 64
D_ATTN = N_HEADS * HEAD_DIM
BLOCK = 256
TOPK = 3
NUM_BUCKETS = 32
MAX_DISTANCE = 128
RNN_BLOCKS = 16
RNN_BLOCK_DIM = 64
CONV_WIDTH = 4
LRU_C = 8.0
EPS = 1e-6
NEG = -1e30
LOG2E = math.log2(math.e)
HEADS_PER_STEP = 4
GROUP = 256
SUBLANES = 8
BF16_ROWS = 16
V_ROWS = HEAD_DIM + BF16_ROWS
VMEM_LIMIT = 56 * 1024 * 1024

F32 = jnp.float32
BF16 = jnp.bfloat16
_NT = (((1,), (1,)), ((), ()))


def _const_spec(shape):
    nd = len(shape)
    return pl.BlockSpec(shape, lambda *_: (0,) * nd, pipeline_mode=pl.Buffered(1))


def _inproj_body(x_ref, n1_ref, wqT_ref, wk_ref, wvT_ref, wr_ref, qw_ref, kw_ref, g_ref,
                 qT_ref, k_ref, vT_ref, km_ref, rest_ref):
    tm = x_ref.shape[1]
    x = x_ref[0]
    ms = jnp.mean(x * x, axis=-1, keepdims=True)
    h = (x * lax.rsqrt(ms + EPS) * n1_ref[...]).astype(BF16)

    qT = lax.dot_general(wqT_ref[...], h, _NT, preferred_element_type=F32)
    q3 = qT.reshape(N_HEADS, HEAD_DIM, tm)
    qss = jnp.mean(q3 * q3, axis=1, keepdims=True)
    qn = q3 * lax.rsqrt(qss + EPS) * qw_ref[...]
    qT_ref[0] = qn.reshape(D_ATTN, tm).astype(BF16)
    vT = lax.dot_general(wvT_ref[...], h, _NT, preferred_element_type=F32).astype(BF16)
    ones_rows = (lax.broadcasted_iota(jnp.int32, (BF16_ROWS, tm), 0) == 0).astype(BF16)
    for hd in range(N_HEADS):
        vT_ref[0, hd * V_ROWS:hd * V_ROWS + HEAD_DIM, :] = vT[hd * HEAD_DIM:(hd + 1) * HEAD_DIM]
        vT_ref[0, hd * V_ROWS + HEAD_DIM:(hd + 1) * V_ROWS, :] = ones_rows

    k = jnp.dot(h, wk_ref[...], preferred_element_type=F32)
    k2 = k * k
    hi = k2.astype(BF16)
    lo = (k2 - hi.astype(F32)).astype(BF16)
    kss = (jnp.dot(hi, g_ref[...], preferred_element_type=F32)
           + jnp.dot(lo, g_ref[...], preferred_element_type=F32))
    kn = k * lax.rsqrt(kss * (1.0 / HEAD_DIM) + EPS) * kw_ref[...]
    k_ref[0] = kn.astype(BF16)
    for bi in range(tm // BLOCK):
        km_ref[0, bi] = jnp.mean(kn[bi * BLOCK:(bi + 1) * BLOCK], axis=0, keepdims=True)

    d_rest = wr_ref.shape[1]
    cw = 1024
    for c in range(d_rest // cw):
        rest_ref[0, :, c * cw:(c + 1) * cw] = jnp.dot(
            h, wr_ref[:, c * cw:(c + 1) * cw], preferred_element_type=F32).astype(BF16)


def _inproj(x, n1, wqT, wk, wvT, wr, qw, kw, gmat, tm):
    B, S, D = x.shape
    d_rest = wr.shape[1]
    nb = S // BLOCK
    grid = (B, S // tm)
    return pl.pallas_call(
        _inproj_body,
        grid=grid,
        in_specs=[
            pl.BlockSpec((1, tm, D), lambda b, s: (b, s, 0)),
            _const_spec(n1.shape), _const_spec(wqT.shape), _const_spec(wk.shape),
            _const_spec(wvT.shape), _const_spec(wr.shape), _const_spec(qw.shape),
            _const_spec(kw.shape), _const_spec(gmat.shape),
        ],
        out_specs=[
            pl.BlockSpec((1, D_ATTN, tm), lambda b, s: (b, 0, s)),
            pl.BlockSpec((1, tm, D_ATTN), lambda b, s: (b, s, 0)),
            pl.BlockSpec((1, N_HEADS * V_ROWS, tm), lambda b, s: (b, 0, s)),
            pl.BlockSpec((1, tm // BLOCK, 1, D_ATTN), lambda b, s: (b, s, 0, 0)),
            pl.BlockSpec((1, tm, d_rest), lambda b, s: (b, s, 0)),
        ],
        out_shape=[
            jax.ShapeDtypeStruct((B, D_ATTN, S), BF16),
            jax.ShapeDtypeStruct((B, S, D_ATTN), BF16),
            jax.ShapeDtypeStruct((B, N_HEADS * V_ROWS, S), BF16),
            jax.ShapeDtypeStruct((B, nb, 1, D_ATTN), F32),
            jax.ShapeDtypeStruct((B, S, d_rest), BF16),
        ],
        compiler_params=pltpu.CompilerParams(
            dimension_semantics=("parallel", "parallel"), vmem_limit_bytes=VMEM_LIMIT),
        name="inproj",
    )(x, n1, wqT, wk, wvT, wr, qw, kw, gmat)


def _attn_body(far_ref, qT_ref, k_ref, vT_ref, km_ref, bd_ref, bp_ref, o_ref, sel_ref, qk_ref):
    p = pl.program_id(1)
    i = pl.program_id(2)
    nb = km_ref.shape[1]
    hps = bd_ref.shape[0]
    lanes = hps * HEAD_DIM
    heads = range(hps)
    qall = qT_ref[0]
    row = lax.broadcasted_iota(jnp.int32, (lanes, BLOCK), 0)
    blk = lax.broadcasted_iota(jnp.int32, (nb, BLOCK), 0)
    past = blk < i
    qz = [jnp.where((row >= hh * HEAD_DIM) & (row < (hh + 1) * HEAD_DIM), qall, jnp.zeros_like(qall))
          for hh in heads]

    km = km_ref[0]
    lane_head = lax.broadcasted_iota(jnp.int32, (nb, lanes), 1) // HEAD_DIM
    km_hi = km.astype(BF16)
    km_r = km - km_hi.astype(F32)
    km_mid = km_r.astype(BF16)
    km_lo = (km_r - km_mid.astype(F32)).astype(BF16)
    stacked = jnp.concatenate(
        [jnp.where(lane_head == hh, piece, jnp.zeros_like(piece))
         for piece in (km_hi, km_mid, km_lo) for hh in heads], axis=0)
    gates = jnp.dot(stacked, qall, preferred_element_type=F32)
    for hh in heads:
        gate = (gates[hh * nb:(hh + 1) * nb] + gates[(hps + hh) * nb:(hps + hh + 1) * nb]
                + gates[(2 * hps + hh) * nb:(2 * hps + hh + 1) * nb])
        gate = jnp.where(past, gate, -jnp.inf)
        rank = jnp.zeros((nb, BLOCK), jnp.int32)
        for m in range(nb):
            gm = gate[m:m + 1, :]
            rank = rank + jnp.where(gm > gate, 1, jnp.where((gm == gate) & (blk > m), 1, 0))
        sel_ref[hh] = jnp.where(past & (rank < TOPK), far_ref[p * hps + hh], NEG)

    def keys(j):
        return k_ref[0, pl.ds(pl.multiple_of(j * BLOCK, BLOCK), BLOCK), :]

    def values(j, hh):
        return vT_ref[0, hh * V_ROWS:(hh + 1) * V_ROWS,
                      pl.ds(pl.multiple_of(j * BLOCK, BLOCK), BLOCK)]

    def issue(slot, j):
        kb = keys(j)
        for hh in heads:
            qk_ref[slot, hh] = jnp.dot(kb, qz[hh], preferred_element_type=F32)

    def update(slot, j, bias, row, state):
        part = []
        for hh in heads:
            m_prev, _ = state[hh]
            s = qk_ref[slot, hh] if bias is None else qk_ref[slot, hh] + bias(hh)
            r = row(hh)
            m_new = jnp.maximum(m_prev, jnp.max(s, axis=0, keepdims=True) + r)
            alpha = jnp.exp2(m_prev - m_new)
            pb = jnp.exp2((s - (m_new - r)).astype(BF16))
            part.append((m_new, alpha, pb))
        new = []
        for hh in heads:
            m_new, alpha, pb = part[hh]
            acc = alpha * state[hh][1] + jnp.dot(values(j, hh), pb, preferred_element_type=F32)
            new.append((m_new, acc))
        return tuple(new)

    jp = jnp.maximum(i - 1, 0)
    n_old = jp
    state = tuple((jnp.full((1, BLOCK), NEG, F32), jnp.zeros((V_ROWS, BLOCK), F32)) for _ in heads)
    zero_row = jnp.zeros((1, BLOCK), F32)
    issue(0, i)
    issue(1, jp)
    state = update(0, i, lambda hh: bd_ref[hh], lambda hh: zero_row, state)
    issue(0, 0)
    state = update(1, jp, lambda hh: bp_ref[hh],
                   lambda hh: jnp.where(sel_ref[hh, pl.ds(jp, 1), :] > 0.5 * NEG, 0.0, NEG), state)

    def older_pair(t, st):
        ja = 2 * t
        jb = ja + 1
        issue(1, jb)
        st = update(0, ja, None, lambda hh: sel_ref[hh, pl.ds(ja, 1), :], st)
        issue(0, ja + 2)
        st = update(1, jb, None,
                    lambda hh: jnp.where(jb < n_old, sel_ref[hh, pl.ds(jb, 1), :], NEG), st)
        return st

    state = lax.fori_loop(0, (n_old + 1) // 2, older_pair, state)
    outs = [acc[:HEAD_DIM] * (1.0 / acc[HEAD_DIM:HEAD_DIM + 1]) for (_, acc) in state]
    o_ref[0] = jnp.concatenate(outs, axis=0).T.astype(BF16)


def _attention(qT, k, vT, kmean, bias_d, bias_p, bias_far, hps):
    B, _, S = qT.shape
    nb = S // BLOCK
    lanes = hps * HEAD_DIM
    grid = (B, N_HEADS // hps, nb)
    return pl.pallas_call(
        _attn_body,
        grid=grid,
        in_specs=[
            pl.BlockSpec(memory_space=pltpu.SMEM),
            pl.BlockSpec((1, lanes, BLOCK), lambda b, p, i: (b, p, i)),
            pl.BlockSpec((1, S, lanes), lambda b, p, i: (b, 0, p)),
            pl.BlockSpec((1, hps * V_ROWS, S), lambda b, p, i: (b, p, 0)),
            pl.BlockSpec((1, nb, lanes), lambda b, p, i: (b, 0, p)),
            pl.BlockSpec((hps, BLOCK, BLOCK), lambda b, p, i: (p, 0, 0)),
            pl.BlockSpec((hps, BLOCK, BLOCK), lambda b, p, i: (p, 0, 0)),
        ],
        out_specs=pl.BlockSpec((1, BLOCK, lanes), lambda b, p, i: (b, i, p)),
        out_shape=jax.ShapeDtypeStruct((B, S, D_ATTN), BF16),
        scratch_shapes=[pltpu.VMEM((hps, nb, BLOCK), F32), pltpu.VMEM((2, hps, BLOCK, BLOCK), F32)],
        compiler_params=pltpu.CompilerParams(
            dimension_semantics=("parallel", "parallel", "arbitrary"), vmem_limit_bytes=VMEM_LIMIT),
        name="moba_attn",
    )(bias_far, qT, k, vT, kmean, bias_d, bias_p)


def _rnn_body(xr_ref, yr_ref, perm_ref, permT_ref, cw_ref, cb_ref, wa_ref, wi_ref, ba_ref, bi_ref,
              lam_ref, o_ref, xs_ref, xc_ref, a_ref, u_ref, tail_ref, hc_ref):
    tr = xr_ref.shape[1]
    d = xr_ref.shape[2]
    seg = tr // SUBLANES
    taps = CONV_WIDTH - 1

    @pl.when(pl.program_id(1) == 0)
    def _():
        tail_ref[...] = jnp.zeros_like(tail_ref)
        hc_ref[...] = jnp.zeros_like(hc_ref)

    xs_ref[...] = jnp.dot(perm_ref[...], xr_ref[0], preferred_element_type=F32)

    def slab(k):
        return xs_ref[k * SUBLANES:(k + 1) * SUBLANES, :]

    first_seg = lax.broadcasted_iota(jnp.int32, (SUBLANES, d), 0) == 0
    last = [slab(seg - taps + j) for j in range(taps)]
    window = [jnp.where(first_seg, pltpu.roll(tail_ref[j], 1, 0), pltpu.roll(last[j], 1, 0))
              for j in range(taps)]
    for j in range(taps):
        tail_ref[j] = last[j]
    cw = [cw_ref[j:j + 1, :] for j in range(CONV_WIDTH)]
    cb = cb_ref[...]
    for k in range(seg):
        cur = slab(k)
        acc = cb + window[0] * cw[0]
        for j in range(1, taps):
            acc = acc + window[j] * cw[j]
        xc_ref[k * SUBLANES:(k + 1) * SUBLANES, :] = acc + cur * cw[taps]
        window = window[1:] + [cur]

    xc = xc_ref[...]
    xb = xc.astype(BF16)
    ra, ia = [], []
    for g in range(d // GROUP):
        xg = xb[:, g * GROUP:(g + 1) * GROUP]
        ra.append(jnp.dot(xg, wa_ref[g], preferred_element_type=F32))
        ia.append(jnp.dot(xg, wi_ref[g], preferred_element_type=F32))
    r = 1.0 / (1.0 + jnp.exp2((jnp.concatenate(ra, axis=1) + ba_ref[...]) * (-LOG2E)))
    ig = 1.0 / (1.0 + jnp.exp2((jnp.concatenate(ia, axis=1) + bi_ref[...]) * (-LOG2E)))
    nlam = -lam_ref[...]
    softplus = jnp.maximum(nlam, 0.0) + jnp.log1p(jnp.exp(-jnp.abs(nlam)))
    a = jnp.exp2(r * (softplus * (-LRU_C * LOG2E)))
    a_ref[...] = a
    gap = 1.0 - a * a
    u_ref[...] = (gap * lax.rsqrt(jnp.maximum(gap, 1e-30))) * (ig * xc)

    h = u_ref[0:SUBLANES, :]
    prod = a_ref[0:SUBLANES, :]
    for k in range(1, seg):
        rows = slice(k * SUBLANES, (k + 1) * SUBLANES)
        a_k = a_ref[rows, :]
        h = a_k * h + u_ref[rows, :]
        prod = a_k * prod
        u_ref[rows, :] = h
        a_ref[rows, :] = prod
    entry = [hc_ref[...]]
    for s in range(SUBLANES):
        entry.append(prod[s:s + 1, :] * entry[s] + h[s:s + 1, :])
    hc_ref[...] = entry[SUBLANES]
    h_in = jnp.concatenate(entry[:SUBLANES], axis=0)
    for k in range(seg):
        rows = slice(k * SUBLANES, (k + 1) * SUBLANES)
        u_ref[rows, :] = a_ref[rows, :] * h_in + u_ref[rows, :]

    y = jnp.dot(perm_ref[...], yr_ref[0], preferred_element_type=F32)
    k1 = -2.0 * math.sqrt(2.0 / math.pi) * LOG2E
    gelu = y / (1.0 + jnp.exp2(y * (k1 + (k1 * 0.044715) * (y * y))))
    o_slab = (u_ref[...] * gelu).astype(BF16)
    o_ref[0] = jnp.dot(permT_ref[...], o_slab, preferred_element_type=F32).astype(BF16)


def _rnn(rest, conv_w, conv_b, wa_bd, wi_bd, b_a, b_i, lam, tr):
    B, S, _ = rest.shape
    d = conv_w.shape[1]
    grid = (B, S // tr)
    seg = tr // SUBLANES
    t = np.arange(tr)
    perm_np = np.zeros((tr, tr), np.float32)
    perm_np[(t % seg) * SUBLANES + t // seg, t] = 1.0
    perm = jnp.asarray(perm_np, BF16)
    permT = jnp.asarray(perm_np.T, BF16)
    return pl.pallas_call(
        _rnn_body,
        grid=grid,
        in_specs=[
            pl.BlockSpec((1, tr, d), lambda b, s: (b, s, 0)),
            pl.BlockSpec((1, tr, d), lambda b, s: (b, s, 1)),
            _const_spec(perm.shape), _const_spec(permT.shape),
            _const_spec(conv_w.shape), _const_spec(conv_b.shape), _const_spec(wa_bd.shape),
            _const_spec(wi_bd.shape), _const_spec(b_a.shape), _const_spec(b_i.shape),
            _const_spec(lam.shape),
        ],
        out_specs=pl.BlockSpec((1, tr, d), lambda b, s: (b, s, 0)),
        out_shape=jax.ShapeDtypeStruct((B, S, d), BF16),
        scratch_shapes=[
            pltpu.VMEM((tr, d), F32), pltpu.VMEM((tr, d), F32), pltpu.VMEM((tr, d), F32), pltpu.VMEM((tr, d), F32),
            pltpu.VMEM((CONV_WIDTH - 1, SUBLANES, d), F32), pltpu.VMEM((1, d), F32),
        ],
        compiler_params=pltpu.CompilerParams(
            dimension_semantics=("parallel", "arbitrary"), vmem_limit_bytes=VMEM_LIMIT),
        name="rglru",
    )(rest, rest, perm, permT, conv_w, conv_b, wa_bd, wi_bd, b_a, b_i, lam)


def _out_body(x_ref, oa_ref, or_ref, ga_ref, gr_ref, bg_ref, pa_ref, pr_ref, wo_ref, n2_ref,
              w1_ref, w2_ref, y_ref):
    a = jnp.dot(oa_ref[...], pa_ref[...], preferred_element_type=F32)
    r = jnp.dot(or_ref[...], pr_ref[...], preferred_element_type=F32)
    g_a = jax.nn.sigmoid(ga_ref[...].astype(F32) + bg_ref[0:1, :])
    g_r = jax.nn.sigmoid(gr_ref[...].astype(F32) + bg_ref[1:2, :])
    merged = (g_a * a + g_r * r).astype(BF16)
    x1 = x_ref[...] + jnp.dot(merged, wo_ref[...], preferred_element_type=F32)
    ms = jnp.mean(x1 * x1, axis=-1, keepdims=True)
    h2 = (x1 * lax.rsqrt(ms + EPS) * n2_ref[...]).astype(BF16)
    d_ff = w1_ref.shape[1]
    cw = 1024
    acc = x1
    for c in range(d_ff // cw):
        t = jnp.dot(h2, w1_ref[:, c * cw:(c + 1) * cw], preferred_element_type=F32)
        t = jnp.maximum(t, 0.0)
        acc = acc + jnp.dot((t * t).astype(BF16), w2_ref[c * cw:(c + 1) * cw, :],
                            preferred_element_type=F32)
    y_ref[...] = acc


def _out(x2, o_attn, o_rnn, rest, bg, pa, pr, wo, n2, w1, w2, tm):
    T, D = x2.shape
    d_rnn = o_rnn.shape[1]
    grid = (T // tm,)
    return pl.pallas_call(
        _out_body,
        grid=grid,
        in_specs=[
            pl.BlockSpec((tm, D), lambda t: (t, 0)),
            pl.BlockSpec((tm, D_ATTN), lambda t: (t, 0)),
            pl.BlockSpec((tm, d_rnn), lambda t: (t, 0)),
            pl.BlockSpec((tm, D), lambda t: (t, 2)),
            pl.BlockSpec((tm, D), lambda t: (t, 3)),
            _const_spec(bg.shape), _const_spec(pa.shape), _const_spec(pr.shape),
            _const_spec(wo.shape), _const_spec(n2.shape), _const_spec(w1.shape),
            _const_spec(w2.shape),
        ],
        out_specs=pl.BlockSpec((tm, D), lambda t: (t, 0)),
        out_shape=jax.ShapeDtypeStruct((T, D), F32),
        compiler_params=pltpu.CompilerParams(
            dimension_semantics=("parallel",), vmem_limit_bytes=VMEM_LIMIT),
        name="merge_mlp",
    )(x2, o_attn, o_rnn, rest, rest, bg, pa, pr, wo, n2, w1, w2)


def _t5_bucket(rel):
    max_exact = NUM_BUCKETS // 2
    n = jnp.maximum(rel, 0)
    nf = jnp.maximum(n, 1).astype(F32)
    large = max_exact + (jnp.log(nf / max_exact) / math.log(MAX_DISTANCE / max_exact)
                         * (NUM_BUCKETS - max_exact)).astype(jnp.int32)
    large = jnp.minimum(large, NUM_BUCKETS - 1)
    return jnp.where(n < max_exact, n, large)


def _bias_tables(rel_bias):
    offs = jnp.arange(BLOCK)
    rel_own = offs[None, :] - offs[:, None]
    buckets = jnp.arange(NUM_BUCKETS)

    def lookup(rel):
        onehot = (_t5_bucket(rel)[..., None] == buckets).astype(F32)
        return jnp.einsum('kqb,bh->hkq', onehot, rel_bias, precision=lax.Precision.HIGHEST)

    bias_d = jnp.where(rel_own >= 0, lookup(rel_own) * LOG2E, NEG)
    bias_p = lookup(rel_own + BLOCK) * LOG2E
    bias_far = rel_bias[NUM_BUCKETS - 1] * LOG2E
    return bias_d.astype(F32), bias_p.astype(F32), bias_far.astype(F32)


def _block_diag(w):
    per = GROUP // RNN_BLOCK_DIM
    w4 = w.reshape(RNN_BLOCKS // per, per, RNN_BLOCK_DIM, RNN_BLOCK_DIM)
    eye = jnp.eye(per, dtype=w.dtype)
    return jnp.einsum('gade,ab->gadbe', w4, eye).reshape(RNN_BLOCKS // per, GROUP, GROUP)


def _layer(x, norm1_w, w_in, b_gate, q_norm_w, k_norm_w, bias_tabs, conv_w, conv_b,
           w_rg_a, b_rg_a, w_rg_i, b_rg_i, lru_lambda, w_proj_attn, w_proj_rnn,
           w_out, norm2_w, w_ff1, w_ff2, tm_in, tr, tm_out):
    B, S, D = x.shape
    d_rnn = conv_w.shape[1]
    c0, c1, c2 = D_ATTN, 2 * D_ATTN, 3 * D_ATTN
    wqT = w_in[:, :c0].T.astype(BF16)
    wk = w_in[:, c0:c1].astype(BF16)
    wvT = w_in[:, c1:c2].T.astype(BF16)
    wr = w_in[:, c2:].astype(BF16)
    qw = (q_norm_w * (HEAD_DIM ** -0.5 * LOG2E)).reshape(1, HEAD_DIM, 1)
    kw = jnp.tile(k_norm_w, N_HEADS).reshape(1, D_ATTN)
    head_of = np.arange(D_ATTN) // HEAD_DIM
    gmat = jnp.asarray(head_of[:, None] == head_of[None, :], dtype=BF16)

    qT, k, vT, kmean, rest = _inproj(x, norm1_w.reshape(1, D), wqT, wk, wvT, wr, qw, kw, gmat, tm_in)
    o_attn = _attention(qT, k, vT, kmean.reshape(B, S // BLOCK, D_ATTN), *bias_tabs, HEADS_PER_STEP)
    o_rnn = _rnn(rest, conv_w, conv_b.reshape(1, d_rnn),
                 _block_diag(w_rg_a).astype(BF16), _block_diag(w_rg_i).astype(BF16),
                 b_rg_a.reshape(1, d_rnn), b_rg_i.reshape(1, d_rnn), lru_lambda.reshape(1, d_rnn), tr)
    y = _out(x.reshape(B * S, D), o_attn.reshape(B * S, D_ATTN), o_rnn.reshape(B * S, d_rnn),
             rest.reshape(B * S, -1), b_gate, w_proj_attn.astype(BF16), w_proj_rnn.astype(BF16),
             w_out.astype(BF16), norm2_w.reshape(1, D), w_ff1.astype(BF16), w_ff2.astype(BF16), tm_out)
    return y.reshape(B, S, D)


def kernel(x, norm1_w, w_in, b_gate, q_norm_w, k_norm_w, rel_bias, conv_w, conv_b, w_rg_a, b_rg_a,
           w_rg_i, b_rg_i, lru_lambda, w_proj_attn, w_proj_rnn, w_out, norm2_w, w_ff1, w_ff2):
    S = x.shape[1]
    assert S % BLOCK == 0
    bias_tabs = _bias_tables(rel_bias)
    tm_in = min(512, S)
    tr = min(256, S)
    tm_out = min(256, S)
    for l in range(norm1_w.shape[0]):
        x = _layer(x, norm1_w[l], w_in[l], b_gate[l], q_norm_w[l], k_norm_w[l], bias_tabs,
                   conv_w[l], conv_b[l], w_rg_a[l], b_rg_a[l], w_rg_i[l], b_rg_i[l], lru_lambda[l],
                   w_proj_attn[l], w_proj_rnn[l], w_out[l], norm2_w[l], w_ff1[l], w_ff2[l],
                   tm_in, tr, tm_out)
    return x
```

```python
import functools
import math

import numpy as np
import jax
import jax.numpy as jnp
from jax import lax
from jax.experimental import pallas as pl
from jax.experimental.pallas import tpu as pltpu

N_HEADS = 8
HEAD_DIM = 64
D_ATTN = N_HEADS * HEAD_DIM
BLOCK = 256
TOPK = 3
NUM_BUCKETS = 32
MAX_DISTANCE = 128
RNN_BLOCKS = 16
RNN_BLOCK_DIM = 64
CONV_WIDTH = 4
LRU_C = 8.0
EPS = 1e-6
NEG = -1e30
LOG2E = math.log2(math.e)
HEADS_PER_STEP = 4
GROUP = 256
SUBLANES = 8
BF16_ROWS = 16
V_ROWS = HEAD_DIM + BF16_ROWS
VMEM_LIMIT = 56 * 1024 * 1024

F32 = jnp.float32
BF16 = jnp.bfloat16
_NT = (((1,), (1,)), ((), ()))


def _const_spec(shape):
    nd = len(shape)
    return pl.BlockSpec(shape, lambda *_: (0,) * nd, pipeline_mode=pl.Buffered(1))


def _inproj_rnn_body(x_ref, n1_ref, wqT_ref, wk_ref, wvT_ref, wxy_ref, wg_ref, qw_ref, kw_ref, g_ref,
                     perm_ref, permT_ref, cw_ref, cb_ref, wa_ref, wi_ref, ba_ref, bi_ref, lam_ref,
                     qT_ref, k_ref, vT_ref, km_ref, gates_ref, orn_ref,
                     xs_ref, y_ref, xc_ref, a_ref, u_ref, tail_ref, hc_ref):
    tm = x_ref.shape[1]
    d = cw_ref.shape[1]
    tr = perm_ref.shape[0]
    seg = tr // SUBLANES
    taps = CONV_WIDTH - 1

    @pl.when(pl.program_id(1) == 0)
    def _():
        tail_ref[...] = jnp.zeros_like(tail_ref)
        hc_ref[...] = jnp.zeros_like(hc_ref)

    x = x_ref[0]
    ms = jnp.mean(x * x, axis=-1, keepdims=True)
    h = (x * lax.rsqrt(ms + EPS) * n1_ref[...]).astype(BF16)

    def rnn_inputs(t):
        hp = jnp.dot(perm_ref[...], h[t * tr:(t + 1) * tr], preferred_element_type=F32).astype(BF16)
        xs_ref[t] = jnp.dot(hp, wxy_ref[:, :d], preferred_element_type=F32)
        y_ref[t] = jnp.dot(hp, wxy_ref[:, d:], preferred_element_type=F32)

    def rnn_conv(t):
        def slab(k):
            return xs_ref[t, k * SUBLANES:(k + 1) * SUBLANES, :]

        first_seg = lax.broadcasted_iota(jnp.int32, (SUBLANES, d), 0) == 0
        last = [slab(seg - taps + j) for j in range(taps)]
        window = [jnp.where(first_seg, pltpu.roll(tail_ref[j], 1, 0), pltpu.roll(last[j], 1, 0))
                  for j in range(taps)]
        for j in range(taps):
            tail_ref[j] = last[j]
        cw = [cw_ref[j:j + 1, :] for j in range(CONV_WIDTH)]
        cb = cb_ref[...]
        for k in range(seg):
            cur = slab(k)
            acc = cb + window[0] * cw[0]
            for j in range(1, taps):
                acc = acc + window[j] * cw[j]
            xc_ref[t, k * SUBLANES:(k + 1) * SUBLANES, :] = acc + cur * cw[taps]
            window = window[1:] + [cur]

    def rnn_gates(t):
        xc = xc_ref[t]
        xb = xc.astype(BF16)
        ra, ia = [], []
        for g in range(d // GROUP):
            xg = xb[:, g * GROUP:(g + 1) * GROUP]
            ra.append(jnp.dot(xg, wa_ref[g], preferred_element_type=F32))
            ia.append(jnp.dot(xg, wi_ref[g], preferred_element_type=F32))
        r = 1.0 / (1.0 + jnp.exp2((jnp.concatenate(ra, axis=1) + ba_ref[...]) * (-LOG2E)))
        ig = 1.0 / (1.0 + jnp.exp2((jnp.concatenate(ia, axis=1) + bi_ref[...]) * (-LOG2E)))
        nlam = -lam_ref[...]
        softplus = jnp.maximum(nlam, 0.0) + jnp.log1p(jnp.exp(-jnp.abs(nlam)))
        a = jnp.exp2(r * (softplus * (-LRU_C * LOG2E)))
        a_ref[t] = a
        gap = 1.0 - a * a
        u_ref[t] = (gap * lax.rsqrt(jnp.maximum(gap, 1e-30))) * (ig * xc)

    def rnn_scan(t):
        hcur = u_ref[t, 0:SUBLANES, :]
        prod = a_ref[t, 0:SUBLANES, :]
        for k in range(1, seg):
            rows = slice(k * SUBLANES, (k + 1) * SUBLANES)
            a_k = a_ref[t, rows, :]
            hcur = a_k * hcur + u_ref[t, rows, :]
            prod = a_k * prod
            u_ref[t, rows, :] = hcur
            a_ref[t, rows, :] = prod
        entry = [hc_ref[...]]
        for s in range(SUBLANES):
            entry.append(prod[s:s + 1, :] * entry[s] + hcur[s:s + 1, :])
        hc_ref[...] = entry[SUBLANES]
        h_in = jnp.concatenate(entry[:SUBLANES], axis=0)
        for k in range(seg):
            rows = slice(k * SUBLANES, (k + 1) * SUBLANES)
            u_ref[t, rows, :] = a_ref[t, rows, :] * h_in + u_ref[t, rows, :]

    def rnn_output(t):
        y = y_ref[t]
        k1 = -2.0 * math.sqrt(2.0 / math.pi) * LOG2E
        gelu = y / (1.0 + jnp.exp2(y * (k1 + (k1 * 0.044715) * (y * y))))
        o_slab = (u_ref[t] * gelu).astype(BF16)
        orn_ref[0, t * tr:(t + 1) * tr, :] = jnp.dot(
            permT_ref[...], o_slab, preferred_element_type=F32).astype(BF16)

    def proj_q():
        qT = lax.dot_general(wqT_ref[...], h, _NT, preferred_element_type=F32)
        q3 = qT.reshape(N_HEADS, HEAD_DIM, tm)
        qss = jnp.mean(q3 * q3, axis=1, keepdims=True)
        qn = q3 * lax.rsqrt(qss + EPS) * qw_ref[...]
        qT_ref[0] = qn.reshape(D_ATTN, tm).astype(BF16)

    def proj_v():
        vT = lax.dot_general(wvT_ref[...], h, _NT, preferred_element_type=F32).astype(BF16)
        ones_rows = (lax.broadcasted_iota(jnp.int32, (BF16_ROWS, tm), 0) == 0).astype(BF16)
        for hd in range(N_HEADS):
            vT_ref[0, hd * V_ROWS:hd * V_ROWS + HEAD_DIM, :] = vT[hd * HEAD_DIM:(hd + 1) * HEAD_DIM]
            vT_ref[0, hd * V_ROWS + HEAD_DIM:(hd + 1) * V_ROWS, :] = ones_rows

    def proj_k():
        k = jnp.dot(h, wk_ref[...], preferred_element_type=F32)
        k2 = k * k
        hi = k2.astype(BF16)
        lo = (k2 - hi.astype(F32)).astype(BF16)
        kss = (jnp.dot(hi, g_ref[...], preferred_element_type=F32)
               + jnp.dot(lo, g_ref[...], preferred_element_type=F32))
        kn = k * lax.rsqrt(kss * (1.0 / HEAD_DIM) + EPS) * kw_ref[...]
        k_ref[0] = kn.astype(BF16)
        for bi in range(tm // BLOCK):
            km_ref[0, bi] = jnp.mean(kn[bi * BLOCK:(bi + 1) * BLOCK], axis=0, keepdims=True)

    gate_chunk = 512

    def proj_gate(c):
        cols = slice(c * gate_chunk, (c + 1) * gate_chunk)
        gates_ref[0, :, cols] = jnp.dot(h, wg_ref[:, cols], preferred_element_type=F32).astype(BF16)

    assert tm == 2 * tr and wg_ref.shape[1] == 4 * gate_chunk
    rnn_inputs(0)
    proj_q()
    rnn_conv(0)
    rnn_gates(0)
    proj_v()
    proj_gate(0)
    rnn_inputs(1)
    rnn_scan(0)
    rnn_output(0)
    proj_k()
    rnn_conv(1)
    rnn_gates(1)
    proj_gate(1)
    proj_gate(2)
    proj_gate(3)
    rnn_scan(1)
    rnn_output(1)


def _inproj_rnn(x, n1, wqT, wk, wvT, wxy, wg, qw, kw, gmat, conv_w, conv_b, wa_bd, wi_bd, b_a, b_i, lam,
                tm, tr):
    B, S, D = x.shape
    d = conv_w.shape[1]
    d_g = wg.shape[1]
    nb = S // BLOCK
    n_sub = tm // tr
    grid = (B, S // tm)
    seg = tr // SUBLANES
    t = np.arange(tr)
    perm_np = np.zeros((tr, tr), np.float32)
    perm_np[(t % seg) * SUBLANES + t // seg, t] = 1.0
    perm = jnp.asarray(perm_np, BF16)
    permT = jnp.asarray(perm_np.T, BF16)
    consts = (n1, wqT, wk, wvT, wxy, wg, qw, kw, gmat, perm, permT, conv_w, conv_b, wa_bd, wi_bd, b_a, b_i, lam)
    return pl.pallas_call(
        _inproj_rnn_body,
        grid=grid,
        in_specs=[pl.BlockSpec((1, tm, D), lambda b, s: (b, s, 0))] + [_const_spec(c.shape) for c in consts],
        out_specs=[
            pl.BlockSpec((1, D_ATTN, tm), lambda b, s: (b, 0, s)),
            pl.BlockSpec((1, tm, D_ATTN), lambda b, s: (b, s, 0)),
            pl.BlockSpec((1, N_HEADS * V_ROWS, tm), lambda b, s: (b, 0, s)),
            pl.BlockSpec((1, tm // BLOCK, 1, D_ATTN), lambda b, s: (b, s, 0, 0)),
            pl.BlockSpec((1, tm, d_g), lambda b, s: (b, s, 0)),
            pl.BlockSpec((1, tm, d), lambda b, s: (b, s, 0)),
        ],
        out_shape=[
            jax.ShapeDtypeStruct((B, D_ATTN, S), BF16),
            jax.ShapeDtypeStruct((B, S, D_ATTN), BF16),
            jax.ShapeDtypeStruct((B, N_HEADS * V_ROWS, S), BF16),
            jax.ShapeDtypeStruct((B, nb, 1, D_ATTN), F32),
            jax.ShapeDtypeStruct((B, S, d_g), BF16),
            jax.ShapeDtypeStruct((B, S, d), BF16),
        ],
        scratch_shapes=[pltpu.VMEM((n_sub, tr, d), F32) for _ in range(5)] + [
            pltpu.VMEM((CONV_WIDTH - 1, SUBLANES, d), F32), pltpu.VMEM((1, d), F32)],
        compiler_params=pltpu.CompilerParams(
            dimension_semantics=("parallel", "arbitrary"), vmem_limit_bytes=VMEM_LIMIT),
        name="inproj_rglru",
    )(x, *consts)


def _attn_body(far_ref, qT_ref, k_ref, vT_ref, km_ref, bd_ref, bp_ref, o_ref, sel_ref, qk_ref):
    p = pl.program_id(1)
    i = pl.program_id(2)
    nb = km_ref.shape[1]
    hps = bd_ref.shape[0]
    lanes = hps * HEAD_DIM
    heads = range(hps)
    qall = qT_ref[0]
    row = lax.broadcasted_iota(jnp.int32, (lanes, BLOCK), 0)
    blk = lax.broadcasted_iota(jnp.int32, (nb, BLOCK), 0)
    past = blk < i
    qz = [jnp.where((row >= hh * HEAD_DIM) & (row < (hh + 1) * HEAD_DIM), qall, jnp.zeros_like(qall))
          for hh in heads]

    km = km_ref[0]
    lane_head = lax.broadcasted_iota(jnp.int32, (nb, lanes), 1) // HEAD_DIM
    km_hi = km.astype(BF16)
    km_r = km - km_hi.astype(F32)
    km_mid = km_r.astype(BF16)
    km_lo = (km_r - km_mid.astype(F32)).astype(BF16)
    stacked = jnp.concatenate(
        [jnp.where(lane_head == hh, piece, jnp.zeros_like(piece))
         for piece in (km_hi, km_mid, km_lo) for hh in heads], axis=0)
    gates = jnp.dot(stacked, qall, preferred_element_type=F32)
    for hh in heads:
        gate = (gates[hh * nb:(hh + 1) * nb] + gates[(hps + hh) * nb:(hps + hh + 1) * nb]
                + gates[(2 * hps + hh) * nb:(2 * hps + hh + 1) * nb])
        gate = jnp.where(past, gate, -jnp.inf)
        rank = jnp.zeros((nb, BLOCK), jnp.int32)
        for m in range(nb):
            gm = gate[m:m + 1, :]
            rank = rank + jnp.where(gm > gate, 1, jnp.where((gm == gate) & (blk > m), 1, 0))
        sel_ref[hh] = jnp.where(past & (rank < TOPK), far_ref[p * hps + hh], NEG)

    def keys(j):
        return k_ref[0, pl.ds(pl.multiple_of(j * BLOCK, BLOCK), BLOCK), :]

    def values(j, hh):
        return vT_ref[0, hh * V_ROWS:(hh + 1) * V_ROWS,
                      pl.ds(pl.multiple_of(j * BLOCK, BLOCK), BLOCK)]

    def issue(slot, j):
        kb = keys(j)
        for hh in heads:
            qk_ref[slot, hh] = jnp.dot(kb, qz[hh], preferred_element_type=F32)

    def update(slot, j, bias, row, state):
        part = []
        for hh in heads:
            m_prev, _ = state[hh]
            s = qk_ref[slot, hh] if bias is None else qk_ref[slot, hh] + bias(hh)
            r = row(hh)
            m_new = jnp.maximum(m_prev, jnp.max(s, axis=0, keepdims=True) + r)
            alpha = jnp.exp2(m_prev - m_new)
            pb = jnp.exp2((s - (m_new - r)).astype(BF16))
            part.append((m_new, alpha, pb))
        new = []
        for hh in heads:
            m_new, alpha, pb = part[hh]
            acc = alpha * state[hh][1] + jnp.dot(values(j, hh), pb, preferred_element_type=F32)
            new.append((m_new, acc))
        return tuple(new)

    jp = jnp.maximum(i - 1, 0)
    n_old = jp
    state = tuple((jnp.full((1, BLOCK), NEG, F32), jnp.zeros((V_ROWS, BLOCK), F32)) for _ in heads)
    zero_row = jnp.zeros((1, BLOCK), F32)
    issue(0, i)
    issue(1, jp)
    state = update(0, i, lambda hh: bd_ref[hh], lambda hh: zero_row, state)
    issue(0, 0)
    state = update(1, jp, lambda hh: bp_ref[hh],
                   lambda hh: jnp.where(sel_ref[hh, pl.ds(jp, 1), :] > 0.5 * NEG, 0.0, NEG), state)

    def older_pair(t, st):
        ja = 2 * t
        jb = ja + 1
        issue(1, jb)
        st = update(0, ja, None, lambda hh: sel_ref[hh, pl.ds(ja, 1), :], st)
        issue(0, ja + 2)
        st = update(1, jb, None,
                    lambda hh: jnp.where(jb < n_old, sel_ref[hh, pl.ds(jb, 1), :], NEG), st)
        return st

    state = lax.fori_loop(0, (n_old + 1) // 2, older_pair, state)
    outs = [acc[:HEAD_DIM] * (1.0 / acc[HEAD_DIM:HEAD_DIM + 1]) for (_, acc) in state]
    o_ref[0] = jnp.concatenate(outs, axis=0).T.astype(BF16)


def _attention(qT, k, vT, kmean, bias_d, bias_p, bias_far, hps):
    B, _, S = qT.shape
    nb = S // BLOCK
    lanes = hps * HEAD_DIM
    grid = (B, N_HEADS // hps, nb)
    return pl.pallas_call(
        _attn_body,
        grid=grid,
        in_specs=[
            pl.BlockSpec(memory_space=pltpu.SMEM),
            pl.BlockSpec((1, lanes, BLOCK), lambda b, p, i: (b, p, i)),
            pl.BlockSpec((1, S, lanes), lambda b, p, i: (b, 0, p)),
            pl.BlockSpec((1, hps * V_ROWS, S), lambda b, p, i: (b, p, 0)),
            pl.BlockSpec((1, nb, lanes), lambda b, p, i: (b, 0, p)),
            pl.BlockSpec((hps, BLOCK, BLOCK), lambda b, p, i: (p, 0, 0)),
            pl.BlockSpec((hps, BLOCK, BLOCK), lambda b, p, i: (p, 0, 0)),
        ],
        out_specs=pl.BlockSpec((1, BLOCK, lanes), lambda b, p, i: (b, i, p)),
        out_shape=jax.ShapeDtypeStruct((B, S, D_ATTN), BF16),
        scratch_shapes=[pltpu.VMEM((hps, nb, BLOCK), F32), pltpu.VMEM((2, hps, BLOCK, BLOCK), F32)],
        compiler_params=pltpu.CompilerParams(
            dimension_semantics=("parallel", "parallel", "arbitrary"), vmem_limit_bytes=VMEM_LIMIT),
        name="moba_attn",
    )(bias_far, qT, k, vT, kmean, bias_d, bias_p)


def _out_body(x_ref, oa_ref, or_ref, ga_ref, gr_ref, bg_ref, pa_ref, pr_ref, wo_ref, n2_ref,
              w1_ref, w2_ref, y_ref):
    a = jnp.dot(oa_ref[...], pa_ref[...], preferred_element_type=F32)
    r = jnp.dot(or_ref[...], pr_ref[...], preferred_element_type=F32)
    g_a = jax.nn.sigmoid(ga_ref[...].astype(F32) + bg_ref[0:1, :])
    g_r = jax.nn.sigmoid(gr_ref[...].astype(F32) + bg_ref[1:2, :])
    merged = (g_a * a + g_r * r).astype(BF16)
    x1 = x_ref[...] + jnp.dot(merged, wo_ref[...], preferred_element_type=F32)
    ms = jnp.mean(x1 * x1, axis=-1, keepdims=True)
    h2 = (x1 * lax.rsqrt(ms + EPS) * n2_ref[...]).astype(BF16)
    d_ff = w1_ref.shape[1]
    cw = 1024
    acc = x1
    for c in range(d_ff // cw):
        t = jnp.dot(h2, w1_ref[:, c * cw:(c + 1) * cw], preferred_element_type=F32)
        t = jnp.maximum(t, 0.0)
        acc = acc + jnp.dot((t * t).astype(BF16), w2_ref[c * cw:(c + 1) * cw, :],
                            preferred_element_type=F32)
    y_ref[...] = acc


def _out(x2, o_attn, o_rnn, gates, bg, pa, pr, wo, n2, w1, w2, tm):
    T, D = x2.shape
    d_rnn = o_rnn.shape[1]
    grid = (T // tm,)
    return pl.pallas_call(
        _out_body,
        grid=grid,
        in_specs=[
            pl.BlockSpec((tm, D), lambda t: (t, 0)),
            pl.BlockSpec((tm, D_ATTN), lambda t: (t, 0)),
            pl.BlockSpec((tm, d_rnn), lambda t: (t, 0)),
            pl.BlockSpec((tm, D), lambda t: (t, 0)),
            pl.BlockSpec((tm, D), lambda t: (t, 1)),
            _const_spec(bg.shape), _const_spec(pa.shape), _const_spec(pr.shape),
            _const_spec(wo.shape), _const_spec(n2.shape), _const_spec(w1.shape),
            _const_spec(w2.shape),
        ],
        out_specs=pl.BlockSpec((tm, D), lambda t: (t, 0)),
        out_shape=jax.ShapeDtypeStruct((T, D), F32),
        compiler_params=pltpu.CompilerParams(
            dimension_semantics=("parallel",), vmem_limit_bytes=VMEM_LIMIT),
        name="merge_mlp",
    )(x2, o_attn, o_rnn, gates, gates, bg, pa, pr, wo, n2, w1, w2)


def _t5_bucket(rel):
    max_exact = NUM_BUCKETS // 2
    n = jnp.maximum(rel, 0)
    nf = jnp.maximum(n, 1).astype(F32)
    large = max_exact + (jnp.log(nf / max_exact) / math.log(MAX_DISTANCE / max_exact)
                         * (NUM_BUCKETS - max_exact)).astype(jnp.int32)
    large = jnp.minimum(large, NUM_BUCKETS - 1)
    return jnp.where(n < max_exact, n, large)


def _bias_tables(rel_bias):
    offs = jnp.arange(BLOCK)
    rel_own = offs[None, :] - offs[:, None]
    buckets = jnp.arange(NUM_BUCKETS)

    def lookup(rel):
        onehot = (_t5_bucket(rel)[..., None] == buckets).astype(F32)
        return jnp.einsum('kqb,bh->hkq', onehot, rel_bias, precision=lax.Precision.HIGHEST)

    bias_d = jnp.where(rel_own >= 0, lookup(rel_own) * LOG2E, NEG)
    bias_p = lookup(rel_own + BLOCK) * LOG2E
    bias_far = rel_bias[NUM_BUCKETS - 1] * LOG2E
    return bias_d.astype(F32), bias_p.astype(F32), bias_far.astype(F32)


def _block_diag(w):
    per = GROUP // RNN_BLOCK_DIM
    w4 = w.reshape(RNN_BLOCKS // per, per, RNN_BLOCK_DIM, RNN_BLOCK_DIM)
    eye = jnp.eye(per, dtype=w.dtype)
    return jnp.einsum('gade,ab->gadbe', w4, eye).reshape(RNN_BLOCKS // per, GROUP, GROUP)


def _layer(x, norm1_w, w_in, b_gate, q_norm_w, k_norm_w, bias_tabs, conv_w, conv_b,
           w_rg_a, b_rg_a, w_rg_i, b_rg_i, lru_lambda, w_proj_attn, w_proj_rnn,
           w_out, norm2_w, w_ff1, w_ff2, tm_in, tr, tm_out):
    B, S, D = x.shape
    d_rnn = conv_w.shape[1]
    c0, c1, c2 = D_ATTN, 2 * D_ATTN, 3 * D_ATTN
    wqT = w_in[:, :c0].T.astype(BF16)
    wk = w_in[:, c0:c1].astype(BF16)
    wvT = w_in[:, c1:c2].T.astype(BF16)
    c3 = c2 + 2 * d_rnn
    wxy = w_in[:, c2:c3].astype(BF16)
    wg = w_in[:, c3:].astype(BF16)
    qw = (q_norm_w * (HEAD_DIM ** -0.5 * LOG2E)).reshape(1, HEAD_DIM, 1)
    kw = jnp.tile(k_norm_w, N_HEADS).reshape(1, D_ATTN)
    head_of = np.arange(D_ATTN) // HEAD_DIM
    gmat = jnp.asarray(head_of[:, None] == head_of[None, :], dtype=BF16)

    qT, k, vT, kmean, gates, o_rnn = _inproj_rnn(
        x, norm1_w.reshape(1, D), wqT, wk, wvT, wxy, wg, qw, kw, gmat, conv_w, conv_b.reshape(1, d_rnn),
        _block_diag(w_rg_a).astype(BF16), _block_diag(w_rg_i).astype(BF16),
        b_rg_a.reshape(1, d_rnn), b_rg_i.reshape(1, d_rnn), lru_lambda.reshape(1, d_rnn), tm_in, tr)
    o_attn = _attention(qT, k, vT, kmean.reshape(B, S // BLOCK, D_ATTN), *bias_tabs, HEADS_PER_STEP)
    y = _out(x.reshape(B * S, D), o_attn.reshape(B * S, D_ATTN), o_rnn.reshape(B * S, d_rnn),
             gates.reshape(B * S, -1), b_gate, w_proj_attn.astype(BF16), w_proj_rnn.astype(BF16),
             w_out.astype(BF16), norm2_w.reshape(1, D), w_ff1.astype(BF16), w_ff2.astype(BF16), tm_out)
    return y.reshape(B, S, D)


def kernel(x, norm1_w, w_in, b_gate, q_norm_w, k_norm_w, rel_bias, conv_w, conv_b, w_rg_a, b_rg_a,
           w_rg_i, b_rg_i, lru_lambda, w_proj_attn, w_proj_rnn, w_out, norm2_w, w_ff1, w_ff2):
    S = x.shape[1]
    assert S % BLOCK == 0
    bias_tabs = _bias_tables(rel_bias)
    tm_in = min(512, S)
    tr = min(256, S)
    tm_out = min(256, S)
    for l in range(norm1_w.shape[0]):
        x = _layer(x, norm1_w[l], w_in[l], b_gate[l], q_norm_w[l], k_norm_w[l], bias_tabs,
                   conv_w[l], conv_b[l], w_rg_a[l], b_rg_a[l], w_rg_i[l], b_rg_i[l], lru_lambda[l],
                   w_proj_attn[l], w_proj_rnn[l], w_out[l], norm2_w[l], w_ff1[l], w_ff2[l],
                   tm_in, tr, tm_out)
    return x
```

```python
import functools
import math

import numpy as np
import jax
import jax.numpy as jnp
from jax import lax
from jax.experimental import pallas as pl
from jax.experimental.pallas import tpu as pltpu

N_HEADS = 8
HEAD_DIM = 64
D_ATTN = N_HEADS * HEAD_DIM
BLOCK = 256
TOPK = 3
NUM_BUCKETS = 32
MAX_DISTANCE = 128
RNN_BLOCKS = 16
RNN_BLOCK_DIM = 64
CONV_WIDTH = 4
LRU_C = 8.0
EPS = 1e-6
NEG = -1e30
LOG2E = math.log2(math.e)
HEADS_PER_STEP = 8
TILE_HEADS = 4
GROUP = 256
SUBLANES = 8
BF16_ROWS = 16
V_ROWS = HEAD_DIM + BF16_ROWS
VMEM_LIMIT = 56 * 1024 * 1024

F32 = jnp.float32
BF16 = jnp.bfloat16
_NT = (((1,), (1,)), ((), ()))


def _const_spec(shape):
    nd = len(shape)
    return pl.BlockSpec(shape, lambda *_: (0,) * nd, pipeline_mode=pl.Buffered(1))


def _inproj_rnn_body(x_ref, n1_ref, wqT_ref, wk_ref, wvT_ref, wxy_ref, wg_ref, qw_ref, kw_ref, g_ref,
                     perm_ref, permT_ref, cw_ref, cb_ref, wa_ref, wi_ref, ba_ref, bi_ref, lam_ref,
                     qT_ref, k_ref, vT_ref, km_ref, gates_ref, orn_ref,
                     xs_ref, y_ref, xc_ref, a_ref, u_ref, tail_ref, hc_ref):
    tm = x_ref.shape[1]
    d = cw_ref.shape[1]
    tr = perm_ref.shape[0]
    seg = tr // SUBLANES
    taps = CONV_WIDTH - 1

    @pl.when(pl.program_id(1) == 0)
    def _():
        tail_ref[...] = jnp.zeros_like(tail_ref)
        hc_ref[...] = jnp.zeros_like(hc_ref)

    x = x_ref[0]
    ms = jnp.mean(x * x, axis=-1, keepdims=True)
    h = (x * lax.rsqrt(ms + EPS) * n1_ref[...]).astype(BF16)

    def rnn_inputs(t):
        hp = jnp.dot(perm_ref[...], h[t * tr:(t + 1) * tr], preferred_element_type=F32).astype(BF16)
        xs_ref[t] = jnp.dot(hp, wxy_ref[:, :d], preferred_element_type=F32)
        y_ref[t] = jnp.dot(hp, wxy_ref[:, d:], preferred_element_type=F32)

    def rnn_conv(t):
        def slab(k):
            return xs_ref[t, k * SUBLANES:(k + 1) * SUBLANES, :]

        first_seg = lax.broadcasted_iota(jnp.int32, (SUBLANES, d), 0) == 0
        last = [slab(seg - taps + j) for j in range(taps)]
        window = [jnp.where(first_seg, pltpu.roll(tail_ref[j], 1, 0), pltpu.roll(last[j], 1, 0))
                  for j in range(taps)]
        for j in range(taps):
            tail_ref[j] = last[j]
        cw = [cw_ref[j:j + 1, :] for j in range(CONV_WIDTH)]
        cb = cb_ref[...]
        for k in range(seg):
            cur = slab(k)
            acc = cb + window[0] * cw[0]
            for j in range(1, taps):
                acc = acc + window[j] * cw[j]
            xc_ref[t, k * SUBLANES:(k + 1) * SUBLANES, :] = acc + cur * cw[taps]
            window = window[1:] + [cur]

    def rnn_gates(t):
        xc = xc_ref[t]
        xb = xc.astype(BF16)
        ra, ia = [], []
        for g in range(d // GROUP):
            xg = xb[:, g * GROUP:(g + 1) * GROUP]
            ra.append(jnp.dot(xg, wa_ref[g], preferred_element_type=F32))
            ia.append(jnp.dot(xg, wi_ref[g], preferred_element_type=F32))
        r = 1.0 / (1.0 + jnp.exp2((jnp.concatenate(ra, axis=1) + ba_ref[...]) * (-LOG2E)))
        ig = 1.0 / (1.0 + jnp.exp2((jnp.concatenate(ia, axis=1) + bi_ref[...]) * (-LOG2E)))
        nlam = -lam_ref[...]
        softplus = jnp.maximum(nlam, 0.0) + jnp.log1p(jnp.exp(-jnp.abs(nlam)))
        a = jnp.exp2(r * (softplus * (-LRU_C * LOG2E)))
        a_ref[t] = a
        gap = 1.0 - a * a
        u_ref[t] = (gap * lax.rsqrt(jnp.maximum(gap, 1e-30))) * (ig * xc)

    def rnn_scan(t):
        hcur = u_ref[t, 0:SUBLANES, :]
        prod = a_ref[t, 0:SUBLANES, :]
        for k in range(1, seg):
            rows = slice(k * SUBLANES, (k + 1) * SUBLANES)
            a_k = a_ref[t, rows, :]
            hcur = a_k * hcur + u_ref[t, rows, :]
            prod = a_k * prod
            u_ref[t, rows, :] = hcur
            a_ref[t, rows, :] = prod
        entry = [hc_ref[...]]
        for s in range(SUBLANES):
            entry.append(prod[s:s + 1, :] * entry[s] + hcur[s:s + 1, :])
        hc_ref[...] = entry[SUBLANES]
        h_in = jnp.concatenate(entry[:SUBLANES], axis=0)
        for k in range(seg):
            rows = slice(k * SUBLANES, (k + 1) * SUBLANES)
            u_ref[t, rows, :] = a_ref[t, rows, :] * h_in + u_ref[t, rows, :]

    def rnn_output(t):
        y = y_ref[t]
        k1 = -2.0 * math.sqrt(2.0 / math.pi) * LOG2E
        gelu = y / (1.0 + jnp.exp2(y * (k1 + (k1 * 0.044715) * (y * y))))
        o_slab = (u_ref[t] * gelu).astype(BF16)
        orn_ref[0, t * tr:(t + 1) * tr, :] = jnp.dot(
            permT_ref[...], o_slab, preferred_element_type=F32).astype(BF16)

    def proj_q():
        qT = lax.dot_general(wqT_ref[...], h, _NT, preferred_element_type=F32)
        q3 = qT.reshape(N_HEADS, HEAD_DIM, tm)
        qss = jnp.mean(q3 * q3, axis=1, keepdims=True)
        qn = q3 * lax.rsqrt(qss + EPS) * qw_ref[...]
        qT_ref[0] = qn.reshape(D_ATTN, tm).astype(BF16)

    def proj_v():
        vT = lax.dot_general(wvT_ref[...], h, _NT, preferred_element_type=F32).astype(BF16)
        ones_rows = (lax.broadcasted_iota(jnp.int32, (BF16_ROWS, tm), 0) == 0).astype(BF16)
        for hd in range(N_HEADS):
            vT_ref[0, hd * V_ROWS:hd * V_ROWS + HEAD_DIM, :] = vT[hd * HEAD_DIM:(hd + 1) * HEAD_DIM]
            vT_ref[0, hd * V_ROWS + HEAD_DIM:(hd + 1) * V_ROWS, :] = ones_rows

    def proj_k():
        k = jnp.dot(h, wk_ref[...], preferred_element_type=F32)
        k2 = k * k
        hi = k2.astype(BF16)
        lo = (k2 - hi.astype(F32)).astype(BF16)
        kss = (jnp.dot(hi, g_ref[...], preferred_element_type=F32)
               + jnp.dot(lo, g_ref[...], preferred_element_type=F32))
        kn = k * lax.rsqrt(kss * (1.0 / HEAD_DIM) + EPS) * kw_ref[...]
        k_ref[0] = kn.astype(BF16)
        for bi in range(tm // BLOCK):
            km_ref[0, bi] = jnp.mean(kn[bi * BLOCK:(bi + 1) * BLOCK], axis=0, keepdims=True)

    gate_chunk = 512

    def proj_gate(c):
        cols = slice(c * gate_chunk, (c + 1) * gate_chunk)
        gates_ref[0, :, cols] = jnp.dot(h, wg_ref[:, cols], preferred_element_type=F32).astype(BF16)

    assert tm == 2 * tr and wg_ref.shape[1] == 4 * gate_chunk
    rnn_inputs(0)
    proj_q()
    rnn_conv(0)
    rnn_gates(0)
    proj_v()
    proj_gate(0)
    rnn_inputs(1)
    rnn_scan(0)
    rnn_output(0)
    proj_k()
    rnn_conv(1)
    rnn_gates(1)
    proj_gate(1)
    proj_gate(2)
    proj_gate(3)
    rnn_scan(1)
    rnn_output(1)


def _inproj_rnn(x, n1, wqT, wk, wvT, wxy, wg, qw, kw, gmat, conv_w, conv_b, wa_bd, wi_bd, b_a, b_i, lam,
                tm, tr):
    B, S, D = x.shape
    d = conv_w.shape[1]
    d_g = wg.shape[1]
    nb = S // BLOCK
    n_sub = tm // tr
    grid = (B, S // tm)
    seg = tr // SUBLANES
    t = np.arange(tr)
    perm_np = np.zeros((tr, tr), np.float32)
    perm_np[(t % seg) * SUBLANES + t // seg, t] = 1.0
    perm = jnp.asarray(perm_np, BF16)
    permT = jnp.asarray(perm_np.T, BF16)
    consts = (n1, wqT, wk, wvT, wxy, wg, qw, kw, gmat, perm, permT, conv_w, conv_b, wa_bd, wi_bd, b_a, b_i, lam)
    return pl.pallas_call(
        _inproj_rnn_body,
        grid=grid,
        in_specs=[pl.BlockSpec((1, tm, D), lambda b, s: (b, s, 0))] + [_const_spec(c.shape) for c in consts],
        out_specs=[
            pl.BlockSpec((1, D_ATTN, tm), lambda b, s: (b, 0, s)),
            pl.BlockSpec((1, tm, D_ATTN), lambda b, s: (b, s, 0)),
            pl.BlockSpec((1, N_HEADS * V_ROWS, tm), lambda b, s: (b, 0, s)),
            pl.BlockSpec((1, tm // BLOCK, 1, D_ATTN), lambda b, s: (b, s, 0, 0)),
            pl.BlockSpec((1, tm, d_g), lambda b, s: (b, s, 0)),
            pl.BlockSpec((1, tm, d), lambda b, s: (b, s, 0)),
        ],
        out_shape=[
            jax.ShapeDtypeStruct((B, D_ATTN, S), BF16),
            jax.ShapeDtypeStruct((B, S, D_ATTN), BF16),
            jax.ShapeDtypeStruct((B, N_HEADS * V_ROWS, S), BF16),
            jax.ShapeDtypeStruct((B, nb, 1, D_ATTN), F32),
            jax.ShapeDtypeStruct((B, S, d_g), BF16),
            jax.ShapeDtypeStruct((B, S, d), BF16),
        ],
        scratch_shapes=[pltpu.VMEM((n_sub, tr, d), F32) for _ in range(5)] + [
            pltpu.VMEM((CONV_WIDTH - 1, SUBLANES, d), F32), pltpu.VMEM((1, d), F32)],
        compiler_params=pltpu.CompilerParams(
            dimension_semantics=("parallel", "arbitrary"), vmem_limit_bytes=VMEM_LIMIT),
        name="inproj_rglru",
    )(x, *consts)


def _attn_body(far_ref, qT_ref, k_ref, vT_ref, km_ref, bd_ref, bp_ref, o_ref, sel_ref, qk_ref, m_ref, acc_ref):
    p = pl.program_id(1)
    i = pl.program_id(2)
    nb = km_ref.shape[1]
    hps = bd_ref.shape[0]
    heads = range(hps)
    groups = range(hps // TILE_HEADS)
    lanes = TILE_HEADS * HEAD_DIM
    row = lax.broadcasted_iota(jnp.int32, (lanes, BLOCK), 0)
    blk = lax.broadcasted_iota(jnp.int32, (nb, BLOCK), 0)
    lane_head = lax.broadcasted_iota(jnp.int32, (nb, lanes), 1) // HEAD_DIM
    past = blk < i
    qz = []
    for g in groups:
        qall = qT_ref[0, g * lanes:(g + 1) * lanes, :]
        qz += [jnp.where((row >= t * HEAD_DIM) & (row < (t + 1) * HEAD_DIM), qall, jnp.zeros_like(qall))
               for t in range(TILE_HEADS)]

        km = km_ref[0, :, g * lanes:(g + 1) * lanes]
        km_hi = km.astype(BF16)
        km_r = km - km_hi.astype(F32)
        km_mid = km_r.astype(BF16)
        km_lo = (km_r - km_mid.astype(F32)).astype(BF16)
        stacked = jnp.concatenate(
            [jnp.where(lane_head == t, piece, jnp.zeros_like(piece))
             for piece in (km_hi, km_mid, km_lo) for t in range(TILE_HEADS)], axis=0)
        gates = jnp.dot(stacked, qall, preferred_element_type=F32)
        for t in range(TILE_HEADS):
            hh = g * TILE_HEADS + t
            gate = (gates[t * nb:(t + 1) * nb] + gates[(TILE_HEADS + t) * nb:(TILE_HEADS + t + 1) * nb]
                    + gates[(2 * TILE_HEADS + t) * nb:(2 * TILE_HEADS + t + 1) * nb])
            gate = jnp.where(past, gate, -jnp.inf)
            rank = jnp.zeros((nb, BLOCK), jnp.int32)
            for m in range(nb):
                gm = gate[m:m + 1, :]
                rank = rank + jnp.where(gm > gate, 1, jnp.where((gm == gate) & (blk > m), 1, 0))
            sel_ref[hh] = jnp.where(past & (rank < TOPK), far_ref[p * hps + hh], NEG)

    def keys(j, g):
        return k_ref[0, pl.ds(pl.multiple_of(j * BLOCK, BLOCK), BLOCK), g * lanes:(g + 1) * lanes]

    def values(j, hh):
        return vT_ref[0, hh * V_ROWS:(hh + 1) * V_ROWS,
                      pl.ds(pl.multiple_of(j * BLOCK, BLOCK), BLOCK)]

    def issue(slot, j):
        for g in groups:
            kb = keys(j, g)
            for hh in range(g * TILE_HEADS, (g + 1) * TILE_HEADS):
                qk_ref[slot, hh] = jnp.dot(kb, qz[hh], preferred_element_type=F32)

    def step(slot, j, bias, row, nxt_slot, nxt_j):
        for g in groups:
            kb = keys(nxt_j, g)
            for hh in range(g * TILE_HEADS, (g + 1) * TILE_HEADS):
                qk_ref[nxt_slot, hh] = jnp.dot(kb, qz[hh], preferred_element_type=F32)
                m_prev = m_ref[hh]
                s = qk_ref[slot, hh] if bias is None else qk_ref[slot, hh] + bias(hh)
                r = row(hh)
                m_new = jnp.maximum(m_prev, jnp.max(s, axis=0, keepdims=True) + r)
                m_ref[hh] = m_new
                alpha = jnp.exp2(m_prev - m_new)
                pb = jnp.exp2((s - (m_new - r)).astype(BF16))
                acc_ref[hh] = alpha * acc_ref[hh] + jnp.dot(values(j, hh), pb, preferred_element_type=F32)

    jp = jnp.maximum(i - 1, 0)
    n_old = jp
    m_ref[...] = jnp.full(m_ref.shape, NEG, F32)
    acc_ref[...] = jnp.zeros(acc_ref.shape, F32)
    zero_row = jnp.zeros((1, BLOCK), F32)
    issue(0, i)
    step(0, i, lambda hh: bd_ref[hh], lambda hh: zero_row, 1, jp)
    step(1, jp, lambda hh: bp_ref[hh],
         lambda hh: jnp.where(sel_ref[hh, pl.ds(jp, 1), :] > 0.5 * NEG, 0.0, NEG), 0, 0)

    def older_pair(t, carry):
        ja = 2 * t
        jb = ja + 1
        step(0, ja, None, lambda hh: sel_ref[hh, pl.ds(ja, 1), :], 1, jb)
        step(1, jb, None, lambda hh: jnp.where(jb < n_old, sel_ref[hh, pl.ds(jb, 1), :], NEG), 0, ja + 2)
        return carry

    lax.fori_loop(0, (n_old + 1) // 2, older_pair, 0)
    outs = [acc_ref[hh, :HEAD_DIM, :] * (1.0 / acc_ref[hh, HEAD_DIM:HEAD_DIM + 1, :]) for hh in heads]
    o_ref[0] = jnp.concatenate(outs, axis=0).T.astype(BF16)


def _attention(qT, k, vT, kmean, bias_d, bias_p, bias_far, hps):
    B, _, S = qT.shape
    nb = S // BLOCK
    lanes = hps * HEAD_DIM
    grid = (B, N_HEADS // hps, nb)
    assert hps % TILE_HEADS == 0
    return pl.pallas_call(
        _attn_body,
        grid=grid,
        in_specs=[
            pl.BlockSpec(memory_space=pltpu.SMEM),
            pl.BlockSpec((1, lanes, BLOCK), lambda b, p, i: (b, p, i)),
            pl.BlockSpec((1, S, lanes), lambda b, p, i: (b, 0, p)),
            pl.BlockSpec((1, hps * V_ROWS, S), lambda b, p, i: (b, p, 0)),
            pl.BlockSpec((1, nb, lanes), lambda b, p, i: (b, 0, p)),
            pl.BlockSpec((hps, BLOCK, BLOCK), lambda b, p, i: (p, 0, 0)),
            pl.BlockSpec((hps, BLOCK, BLOCK), lambda b, p, i: (p, 0, 0)),
        ],
        out_specs=pl.BlockSpec((1, BLOCK, lanes), lambda b, p, i: (b, i, p)),
        out_shape=jax.ShapeDtypeStruct((B, S, D_ATTN), BF16),
        scratch_shapes=[pltpu.VMEM((hps, nb, BLOCK), F32), pltpu.VMEM((2, hps, BLOCK, BLOCK), F32),
                        pltpu.VMEM((hps, 1, BLOCK), F32), pltpu.VMEM((hps, V_ROWS, BLOCK), F32)],
        compiler_params=pltpu.CompilerParams(
            dimension_semantics=("parallel", "parallel", "arbitrary"), vmem_limit_bytes=VMEM_LIMIT),
        name="moba_attn",
    )(bias_far, qT, k, vT, kmean, bias_d, bias_p)


def _out_body(x_ref, oa_ref, or_ref, ga_ref, gr_ref, bg_ref, pa_ref, pr_ref, wo_ref, n2_ref,
              w1_ref, w2_ref, y_ref):
    a = jnp.dot(oa_ref[...], pa_ref[...], preferred_element_type=F32)
    r = jnp.dot(or_ref[...], pr_ref[...], preferred_element_type=F32)
    g_a = jax.nn.sigmoid(ga_ref[...].astype(F32) + bg_ref[0:1, :])
    g_r = jax.nn.sigmoid(gr_ref[...].astype(F32) + bg_ref[1:2, :])
    merged = (g_a * a + g_r * r).astype(BF16)
    x1 = x_ref[...] + jnp.dot(merged, wo_ref[...], preferred_element_type=F32)
    ms = jnp.mean(x1 * x1, axis=-1, keepdims=True)
    h2 = (x1 * lax.rsqrt(ms + EPS) * n2_ref[...]).astype(BF16)
    d_ff = w1_ref.shape[1]
    cw = 1024
    acc = x1
    for c in range(d_ff // cw):
        t = jnp.dot(h2, w1_ref[:, c * cw:(c + 1) * cw], preferred_element_type=F32)
        t = jnp.maximum(t, 0.0)
        acc = acc + jnp.dot((t * t).astype(BF16), w2_ref[c * cw:(c + 1) * cw, :],
                            preferred_element_type=F32)
    y_ref[...] = acc


def _out(x2, o_attn, o_rnn, gates, bg, pa, pr, wo, n2, w1, w2, tm):
    T, D = x2.shape
    d_rnn = o_rnn.shape[1]
    grid = (T // tm,)
    return pl.pallas_call(
        _out_body,
        grid=grid,
        in_specs=[
            pl.BlockSpec((tm, D), lambda t: (t, 0)),
            pl.BlockSpec((tm, D_ATTN), lambda t: (t, 0)),
            pl.BlockSpec((tm, d_rnn), lambda t: (t, 0)),
            pl.BlockSpec((tm, D), lambda t: (t, 0)),
            pl.BlockSpec((tm, D), lambda t: (t, 1)),
            _const_spec(bg.shape), _const_spec(pa.shape), _const_spec(pr.shape),
            _const_spec(wo.shape), _const_spec(n2.shape), _const_spec(w1.shape),
            _const_spec(w2.shape),
        ],
        out_specs=pl.BlockSpec((tm, D), lambda t: (t, 0)),
        out_shape=jax.ShapeDtypeStruct((T, D), F32),
        compiler_params=pltpu.CompilerParams(
            dimension_semantics=("parallel",), vmem_limit_bytes=VMEM_LIMIT),
        name="merge_mlp",
    )(x2, o_attn, o_rnn, gates, gates, bg, pa, pr, wo, n2, w1, w2)


def _t5_bucket(rel):
    max_exact = NUM_BUCKETS // 2
    n = jnp.maximum(rel, 0)
    nf = jnp.maximum(n, 1).astype(F32)
    large = max_exact + (jnp.log(nf / max_exact) / math.log(MAX_DISTANCE / max_exact)
                         * (NUM_BUCKETS - max_exact)).astype(jnp.int32)
    large = jnp.minimum(large, NUM_BUCKETS - 1)
    return jnp.where(n < max_exact, n, large)


def _bias_tables(rel_bias):
    offs = jnp.arange(BLOCK)
    rel_own = offs[None, :] - offs[:, None]
    buckets = jnp.arange(NUM_BUCKETS)

    def lookup(rel):
        onehot = (_t5_bucket(rel)[..., None] == buckets).astype(F32)
        return jnp.einsum('kqb,bh->hkq', onehot, rel_bias, precision=lax.Precision.HIGHEST)

    bias_d = jnp.where(rel_own >= 0, lookup(rel_own) * LOG2E, NEG)
    bias_p = lookup(rel_own + BLOCK) * LOG2E
    bias_far = rel_bias[NUM_BUCKETS - 1] * LOG2E
    return bias_d.astype(F32), bias_p.astype(F32), bias_far.astype(F32)


def _block_diag(w):
    per = GROUP // RNN_BLOCK_DIM
    w4 = w.reshape(RNN_BLOCKS // per, per, RNN_BLOCK_DIM, RNN_BLOCK_DIM)
    eye = jnp.eye(per, dtype=w.dtype)
    return jnp.einsum('gade,ab->gadbe', w4, eye).reshape(RNN_BLOCKS // per, GROUP, GROUP)


def _layer(x, norm1_w, w_in, b_gate, q_norm_w, k_norm_w, bias_tabs, conv_w, conv_b,
           w_rg_a, b_rg_a, w_rg_i, b_rg_i, lru_lambda, w_proj_attn, w_proj_rnn,
           w_out, norm2_w, w_ff1, w_ff2, tm_in, tr, tm_out):
    B, S, D = x.shape
    d_rnn = conv_w.shape[1]
    c0, c1, c2 = D_ATTN, 2 * D_ATTN, 3 * D_ATTN
    wqT = w_in[:, :c0].T.astype(BF16)
    wk = w_in[:, c0:c1].astype(BF16)
    wvT = w_in[:, c1:c2].T.astype(BF16)
    c3 = c2 + 2 * d_rnn
    wxy = w_in[:, c2:c3].astype(BF16)
    wg = w_in[:, c3:].astype(BF16)
    qw = (q_norm_w * (HEAD_DIM ** -0.5 * LOG2E)).reshape(1, HEAD_DIM, 1)
    kw = jnp.tile(k_norm_w, N_HEADS).reshape(1, D_ATTN)
    head_of = np.arange(D_ATTN) // HEAD_DIM
    gmat = jnp.asarray(head_of[:, None] == head_of[None, :], dtype=BF16)

    qT, k, vT, kmean, gates, o_rnn = _inproj_rnn(
        x, norm1_w.reshape(1, D), wqT, wk, wvT, wxy, wg, qw, kw, gmat, conv_w, conv_b.reshape(1, d_rnn),
        _block_diag(w_rg_a).astype(BF16), _block_diag(w_rg_i).astype(BF16),
        b_rg_a.reshape(1, d_rnn), b_rg_i.reshape(1, d_rnn), lru_lambda.reshape(1, d_rnn), tm_in, tr)
    o_attn = _attention(qT, k, vT, kmean.reshape(B, S // BLOCK, D_ATTN), *bias_tabs, HEADS_PER_STEP)
    y = _out(x.reshape(B * S, D), o_attn.reshape(B * S, D_ATTN), o_rnn.reshape(B * S, d_rnn),
             gates.reshape(B * S, -1), b_gate, w_proj_attn.astype(BF16), w_proj_rnn.astype(BF16),
             w_out.astype(BF16), norm2_w.reshape(1, D), w_ff1.astype(BF16), w_ff2.astype(BF16), tm_out)
    return y.reshape(B, S, D)


def kernel(x, norm1_w, w_in, b_gate, q_norm_w, k_norm_w, rel_bias, conv_w, conv_b, w_rg_a, b_rg_a,
           w_rg_i, b_rg_i, lru_lambda, w_proj_attn, w_proj_rnn, w_out, norm2_w, w_ff1, w_ff2):
    S = x.shape[1]
    assert S % BLOCK == 0
    bias_tabs = _bias_tables(rel_bias)
    tm_in = min(512, S)
    tr = min(256, S)
    tm_out = min(256, S)
    for l in range(norm1_w.shape[0]):
        x = _layer(x, norm1_w[l], w_in[l], b_gate[l], q_norm_w[l], k_norm_w[l], bias_tabs,
                   conv_w[l], conv_b[l], w_rg_a[l], b_rg_a[l], w_rg_i[l], b_rg_i[l], lru_lambda[l],
                   w_proj_attn[l], w_proj_rnn[l], w_out[l], norm2_w[l], w_ff1[l], w_ff2[l],
                   tm_in, tr, tm_out)
    return x
```

```python
import functools
import math

import numpy as np
import jax
import jax.numpy as jnp
from jax import lax
from jax.experimental import pallas as pl
from jax.experimental.pallas import tpu as pltpu

N_HEADS = 8
HEAD_DIM = 64
D_ATTN = N_HEADS * HEAD_DIM
BLOCK = 256
TOPK = 3
NUM_BUCKETS = 32
MAX_DISTANCE = 128
RNN_BLOCKS = 16
RNN_BLOCK_DIM = 64
CONV_WIDTH = 4
LRU_C = 8.0
EPS = 1e-6
NEG = -1e30
LOG2E = math.log2(math.e)
HEADS_PER_STEP = 8
TILE_HEADS = 4
GROUP = 256
SUBLANES = 8
BF16_ROWS = 16
V_ROWS = HEAD_DIM + BF16_ROWS
VMEM_LIMIT = 56 * 1024 * 1024

F32 = jnp.float32
BF16 = jnp.bfloat16
_NT = (((1,), (1,)), ((), ()))


def _const_spec(shape):
    nd = len(shape)
    return pl.BlockSpec(shape, lambda *_: (0,) * nd, pipeline_mode=pl.Buffered(1))


def _inproj_rnn_body(x_ref, n1_ref, wqT_ref, wk_ref, wvT_ref, wxy_ref, wg_ref, qw_ref, kw_ref, g_ref,
                     perm_ref, permT_ref, cw_ref, cb_ref, wa_ref, wi_ref, ba_ref, bi_ref, lam_ref,
                     qT_ref, k_ref, vT_ref, km_ref, gates_ref, orn_ref,
                     xs_ref, y_ref, xc_ref, a_ref, u_ref, tail_ref, hc_ref):
    tm = x_ref.shape[1]
    d = cw_ref.shape[1]
    tr = perm_ref.shape[0]
    seg = tr // SUBLANES
    taps = CONV_WIDTH - 1

    @pl.when(pl.program_id(1) == 0)
    def _():
        tail_ref[...] = jnp.zeros_like(tail_ref)
        hc_ref[...] = jnp.zeros_like(hc_ref)

    x = x_ref[0]
    ms = jnp.mean(x * x, axis=-1, keepdims=True)
    h = (x * lax.rsqrt(ms + EPS) * n1_ref[...]).astype(BF16)

    def rnn_inputs(t):
        hp = jnp.dot(perm_ref[...], h[t * tr:(t + 1) * tr], preferred_element_type=F32).astype(BF16)
        xs_ref[t] = jnp.dot(hp, wxy_ref[:, :d], preferred_element_type=F32)
        y_ref[t] = jnp.dot(hp, wxy_ref[:, d:], preferred_element_type=F32)

    def rnn_conv(t):
        def slab(k):
            return xs_ref[t, k * SUBLANES:(k + 1) * SUBLANES, :]

        first_seg = lax.broadcasted_iota(jnp.int32, (SUBLANES, d), 0) == 0
        last = [slab(seg - taps + j) for j in range(taps)]
        window = [jnp.where(first_seg, pltpu.roll(tail_ref[j], 1, 0), pltpu.roll(last[j], 1, 0))
                  for j in range(taps)]
        for j in range(taps):
            tail_ref[j] = last[j]
        cw = [cw_ref[j:j + 1, :] for j in range(CONV_WIDTH)]
        cb = cb_ref[...]
        for k in range(seg):
            cur = slab(k)
            acc = cb + window[0] * cw[0]
            for j in range(1, taps):
                acc = acc + window[j] * cw[j]
            xc_ref[t, k * SUBLANES:(k + 1) * SUBLANES, :] = acc + cur * cw[taps]
            window = window[1:] + [cur]

    def rnn_gate_mm(t):
        xb = xc_ref[t].astype(BF16)
        for g in range(d // GROUP):
            cols = slice(g * GROUP, (g + 1) * GROUP)
            a_ref[t, :, cols] = jnp.dot(xb[:, cols], wa_ref[g], preferred_element_type=F32)
            u_ref[t, :, cols] = jnp.dot(xb[:, cols], wi_ref[g], preferred_element_type=F32)

    def rnn_nonlin(t):
        xc = xc_ref[t]
        t_r = jnp.tanh(0.5 * (a_ref[t] + ba_ref[...]))
        t_i = jnp.tanh(0.5 * (u_ref[t] + bi_ref[...]))
        nlam = -lam_ref[...]
        softplus = jnp.maximum(nlam, 0.0) + jnp.log1p(jnp.exp(-jnp.abs(nlam)))
        half_c = softplus * (-0.5 * LRU_C * LOG2E)
        a = jnp.exp2(t_r * half_c + half_c)
        a_ref[t] = a
        gap = 1.0 - a * a
        half_x = 0.5 * xc
        u_ref[t] = (gap * lax.rsqrt(jnp.maximum(gap, 1e-30))) * (half_x * t_i + half_x)

    def rnn_scan(t):
        hcur = u_ref[t, 0:SUBLANES, :]
        prod = a_ref[t, 0:SUBLANES, :]
        for k in range(1, seg):
            rows = slice(k * SUBLANES, (k + 1) * SUBLANES)
            a_k = a_ref[t, rows, :]
            hcur = a_k * hcur + u_ref[t, rows, :]
            prod = a_k * prod
            u_ref[t, rows, :] = hcur
            a_ref[t, rows, :] = prod
        entry = [hc_ref[...]]
        for s in range(SUBLANES):
            entry.append(prod[s:s + 1, :] * entry[s] + hcur[s:s + 1, :])
        hc_ref[...] = entry[SUBLANES]
        h_in = jnp.concatenate(entry[:SUBLANES], axis=0)
        for k in range(seg):
            rows = slice(k * SUBLANES, (k + 1) * SUBLANES)
            u_ref[t, rows, :] = a_ref[t, rows, :] * h_in + u_ref[t, rows, :]

    def rnn_output(t):
        y = y_ref[t]
        k1 = -2.0 * math.sqrt(2.0 / math.pi) * LOG2E
        gelu = y / (1.0 + jnp.exp2(y * (k1 + (k1 * 0.044715) * (y * y))))
        o_slab = (u_ref[t] * gelu).astype(BF16)
        orn_ref[0, t * tr:(t + 1) * tr, :] = jnp.dot(
            permT_ref[...], o_slab, preferred_element_type=F32).astype(BF16)

    def proj_q():
        qT = lax.dot_general(wqT_ref[...], h, _NT, preferred_element_type=F32)
        q3 = qT.reshape(N_HEADS, HEAD_DIM, tm)
        qss = jnp.mean(q3 * q3, axis=1, keepdims=True)
        qn = q3 * lax.rsqrt(qss + EPS) * qw_ref[...]
        qT_ref[0] = qn.reshape(D_ATTN, tm).astype(BF16)

    def proj_v():
        vT = lax.dot_general(wvT_ref[...], h, _NT, preferred_element_type=F32).astype(BF16)
        ones_rows = (lax.broadcasted_iota(jnp.int32, (BF16_ROWS, tm), 0) == 0).astype(BF16)
        for hd in range(N_HEADS):
            vT_ref[0, hd * V_ROWS:hd * V_ROWS + HEAD_DIM, :] = vT[hd * HEAD_DIM:(hd + 1) * HEAD_DIM]
            vT_ref[0, hd * V_ROWS + HEAD_DIM:(hd + 1) * V_ROWS, :] = ones_rows

    def proj_k():
        k = jnp.dot(h, wk_ref[...], preferred_element_type=F32)
        k2 = k * k
        hi = k2.astype(BF16)
        lo = (k2 - hi.astype(F32)).astype(BF16)
        kss = (jnp.dot(hi, g_ref[...], preferred_element_type=F32)
               + jnp.dot(lo, g_ref[...], preferred_element_type=F32))
        kn = k * lax.rsqrt(kss * (1.0 / HEAD_DIM) + EPS) * kw_ref[...]
        k_ref[0] = kn.astype(BF16)
        for bi in range(tm // BLOCK):
            km_ref[0, bi] = jnp.mean(kn[bi * BLOCK:(bi + 1) * BLOCK], axis=0, keepdims=True)

    gate_chunk = 512

    def proj_gate(c):
        cols = slice(c * gate_chunk, (c + 1) * gate_chunk)
        gates_ref[0, :, cols] = jnp.dot(h, wg_ref[:, cols], preferred_element_type=F32).astype(BF16)

    assert tm == 2 * tr and wg_ref.shape[1] == 4 * gate_chunk
    rnn_inputs(0)
    proj_q()
    rnn_conv(0)
    rnn_gate_mm(0)
    proj_v()
    proj_gate(0)
    rnn_inputs(1)
    rnn_nonlin(0)
    rnn_scan(0)
    rnn_output(0)
    proj_k()
    rnn_conv(1)
    rnn_gate_mm(1)
    proj_gate(1)
    proj_gate(2)
    proj_gate(3)
    rnn_nonlin(1)
    rnn_scan(1)
    rnn_output(1)


def _inproj_rnn(x, n1, wqT, wk, wvT, wxy, wg, qw, kw, gmat, conv_w, conv_b, wa_bd, wi_bd, b_a, b_i, lam,
                tm, tr):
    B, S, D = x.shape
    d = conv_w.shape[1]
    d_g = wg.shape[1]
    nb = S // BLOCK
    n_sub = tm // tr
    grid = (B, S // tm)
    seg = tr // SUBLANES
    t = np.arange(tr)
    perm_np = np.zeros((tr, tr), np.float32)
    perm_np[(t % seg) * SUBLANES + t // seg, t] = 1.0
    perm = jnp.asarray(perm_np, BF16)
    permT = jnp.asarray(perm_np.T, BF16)
    consts = (n1, wqT, wk, wvT, wxy, wg, qw, kw, gmat, perm, permT, conv_w, conv_b, wa_bd, wi_bd, b_a, b_i, lam)
    return pl.pallas_call(
        _inproj_rnn_body,
        grid=grid,
        in_specs=[pl.BlockSpec((1, tm, D), lambda b, s: (b, s, 0))] + [_const_spec(c.shape) for c in consts],
        out_specs=[
            pl.BlockSpec((1, D_ATTN, tm), lambda b, s: (b, 0, s)),
            pl.BlockSpec((1, tm, D_ATTN), lambda b, s: (b, s, 0)),
            pl.BlockSpec((1, N_HEADS * V_ROWS, tm), lambda b, s: (b, 0, s)),
            pl.BlockSpec((1, tm // BLOCK, 1, D_ATTN), lambda b, s: (b, s, 0, 0)),
            pl.BlockSpec((1, tm, d_g), lambda b, s: (b, s, 0)),
            pl.BlockSpec((1, tm, d), lambda b, s: (b, s, 0)),
        ],
        out_shape=[
            jax.ShapeDtypeStruct((B, D_ATTN, S), BF16),
            jax.ShapeDtypeStruct((B, S, D_ATTN), BF16),
            jax.ShapeDtypeStruct((B, N_HEADS * V_ROWS, S), BF16),
            jax.ShapeDtypeStruct((B, nb, 1, D_ATTN), F32),
            jax.ShapeDtypeStruct((B, S, d_g), BF16),
            jax.ShapeDtypeStruct((B, S, d), BF16),
        ],
        scratch_shapes=[pltpu.VMEM((n_sub, tr, d), F32) for _ in range(5)] + [
            pltpu.VMEM((CONV_WIDTH - 1, SUBLANES, d), F32), pltpu.VMEM((1, d), F32)],
        compiler_params=pltpu.CompilerParams(
            dimension_semantics=("parallel", "arbitrary"), vmem_limit_bytes=VMEM_LIMIT),
        name="inproj_rglru",
    )(x, *consts)


def _attn_body(far_ref, qT_ref, k_ref, vT_ref, km_ref, bd_ref, bp_ref, o_ref, sel_ref, qk_ref, m_ref, acc_ref):
    p = pl.program_id(1)
    i = pl.program_id(2)
    nb = km_ref.shape[1]
    hps = bd_ref.shape[0]
    heads = range(hps)
    groups = range(hps // TILE_HEADS)
    lanes = TILE_HEADS * HEAD_DIM
    row = lax.broadcasted_iota(jnp.int32, (lanes, BLOCK), 0)
    blk = lax.broadcasted_iota(jnp.int32, (nb, BLOCK), 0)
    lane_head = lax.broadcasted_iota(jnp.int32, (nb, lanes), 1) // HEAD_DIM
    past = blk < i
    qz = []
    for g in groups:
        qall = qT_ref[0, g * lanes:(g + 1) * lanes, :]
        qz += [jnp.where((row >= t * HEAD_DIM) & (row < (t + 1) * HEAD_DIM), qall, jnp.zeros_like(qall))
               for t in range(TILE_HEADS)]

        km = km_ref[0, :, g * lanes:(g + 1) * lanes]
        km_hi = km.astype(BF16)
        km_r = km - km_hi.astype(F32)
        km_mid = km_r.astype(BF16)
        km_lo = (km_r - km_mid.astype(F32)).astype(BF16)
        stacked = jnp.concatenate(
            [jnp.where(lane_head == t, piece, jnp.zeros_like(piece))
             for piece in (km_hi, km_mid, km_lo) for t in range(TILE_HEADS)], axis=0)
        gates = jnp.dot(stacked, qall, preferred_element_type=F32)
        for t in range(TILE_HEADS):
            hh = g * TILE_HEADS + t
            gate = (gates[t * nb:(t + 1) * nb] + gates[(TILE_HEADS + t) * nb:(TILE_HEADS + t + 1) * nb]
                    + gates[(2 * TILE_HEADS + t) * nb:(2 * TILE_HEADS + t + 1) * nb])
            gate = jnp.where(past, gate, -jnp.inf)
            rank = jnp.zeros((nb, BLOCK), jnp.int32)
            for m in range(nb):
                gm = gate[m:m + 1, :]
                rank = rank + jnp.where(gm > gate, 1, jnp.where((gm == gate) & (blk > m), 1, 0))
            sel_ref[hh] = jnp.where(past & (rank < TOPK), far_ref[p * hps + hh], NEG)

    def keys(j, g):
        return k_ref[0, pl.ds(pl.multiple_of(j * BLOCK, BLOCK), BLOCK), g * lanes:(g + 1) * lanes]

    def values(j, hh):
        return vT_ref[0, hh * V_ROWS:(hh + 1) * V_ROWS,
                      pl.ds(pl.multiple_of(j * BLOCK, BLOCK), BLOCK)]

    def issue(slot, j):
        for g in groups:
            kb = keys(j, g)
            for hh in range(g * TILE_HEADS, (g + 1) * TILE_HEADS):
                qk_ref[slot, hh] = jnp.dot(kb, qz[hh], preferred_element_type=F32)

    def step(slot, j, bias, row, nxt_slot, nxt_j):
        for g in groups:
            kb = keys(nxt_j, g)
            for hh in range(g * TILE_HEADS, (g + 1) * TILE_HEADS):
                qk_ref[nxt_slot, hh] = jnp.dot(kb, qz[hh], preferred_element_type=F32)
                m_prev = m_ref[hh]
                s = qk_ref[slot, hh] if bias is None else qk_ref[slot, hh] + bias(hh)
                r = row(hh)
                m_new = jnp.maximum(m_prev, jnp.max(s, axis=0, keepdims=True) + r)
                m_ref[hh] = m_new
                alpha = jnp.exp2(m_prev - m_new)
                pb = jnp.exp2((s - (m_new - r)).astype(BF16))
                acc_ref[hh] = alpha * acc_ref[hh] + jnp.dot(values(j, hh), pb, preferred_element_type=F32)

    jp = jnp.maximum(i - 1, 0)
    n_old = jp
    m_ref[...] = jnp.full(m_ref.shape, NEG, F32)
    acc_ref[...] = jnp.zeros(acc_ref.shape, F32)
    zero_row = jnp.zeros((1, BLOCK), F32)
    issue(0, i)
    step(0, i, lambda hh: bd_ref[hh], lambda hh: zero_row, 1, jp)
    step(1, jp, lambda hh: bp_ref[hh],
         lambda hh: jnp.where(sel_ref[hh, pl.ds(jp, 1), :] > 0.5 * NEG, 0.0, NEG), 0, 0)

    def older_pair(t, carry):
        ja = 2 * t
        jb = ja + 1
        step(0, ja, None, lambda hh: sel_ref[hh, pl.ds(ja, 1), :], 1, jb)
        step(1, jb, None, lambda hh: jnp.where(jb < n_old, sel_ref[hh, pl.ds(jb, 1), :], NEG), 0, ja + 2)
        return carry

    lax.fori_loop(0, (n_old + 1) // 2, older_pair, 0)
    outs = [acc_ref[hh, :HEAD_DIM, :] * (1.0 / acc_ref[hh, HEAD_DIM:HEAD_DIM + 1, :]) for hh in heads]
    o_ref[0] = jnp.concatenate(outs, axis=0).T.astype(BF16)


def _attention(qT, k, vT, kmean, bias_d, bias_p, bias_far, hps):
    B, _, S = qT.shape
    nb = S // BLOCK
    lanes = hps * HEAD_DIM
    grid = (B, N_HEADS // hps, nb)
    assert hps % TILE_HEADS == 0
    return pl.pallas_call(
        _attn_body,
        grid=grid,
        in_specs=[
            pl.BlockSpec(memory_space=pltpu.SMEM),
            pl.BlockSpec((1, lanes, BLOCK), lambda b, p, i: (b, p, i)),
            pl.BlockSpec((1, S, lanes), lambda b, p, i: (b, 0, p)),
            pl.BlockSpec((1, hps * V_ROWS, S), lambda b, p, i: (b, p, 0)),
            pl.BlockSpec((1, nb, lanes), lambda b, p, i: (b, 0, p)),
            pl.BlockSpec((hps, BLOCK, BLOCK), lambda b, p, i: (p, 0, 0)),
            pl.BlockSpec((hps, BLOCK, BLOCK), lambda b, p, i: (p, 0, 0)),
        ],
        out_specs=pl.BlockSpec((1, BLOCK, lanes), lambda b, p, i: (b, i, p)),
        out_shape=jax.ShapeDtypeStruct((B, S, D_ATTN), BF16),
        scratch_shapes=[pltpu.VMEM((hps, nb, BLOCK), F32), pltpu.VMEM((2, hps, BLOCK, BLOCK), F32),
                        pltpu.VMEM((hps, 1, BLOCK), F32), pltpu.VMEM((hps, V_ROWS, BLOCK), F32)],
        compiler_params=pltpu.CompilerParams(
            dimension_semantics=("parallel", "parallel", "arbitrary"), vmem_limit_bytes=VMEM_LIMIT),
        name="moba_attn",
    )(bias_far, qT, k, vT, kmean, bias_d, bias_p)


def _out_body(x_ref, oa_ref, or_ref, ga_ref, gr_ref, bg_ref, pa_ref, pr_ref, wo_ref, n2_ref,
              w1_ref, w2_ref, y_ref):
    a = jnp.dot(oa_ref[...], pa_ref[...], preferred_element_type=F32)
    r = jnp.dot(or_ref[...], pr_ref[...], preferred_element_type=F32)
    g_a = jax.nn.sigmoid(ga_ref[...].astype(F32) + bg_ref[0:1, :])
    g_r = jax.nn.sigmoid(gr_ref[...].astype(F32) + bg_ref[1:2, :])
    merged = (g_a * a + g_r * r).astype(BF16)
    x1 = x_ref[...] + jnp.dot(merged, wo_ref[...], preferred_element_type=F32)
    ms = jnp.mean(x1 * x1, axis=-1, keepdims=True)
    h2 = (x1 * lax.rsqrt(ms + EPS) * n2_ref[...]).astype(BF16)
    d_ff = w1_ref.shape[1]
    cw = 1024
    acc = x1
    for c in range(d_ff // cw):
        t = jnp.dot(h2, w1_ref[:, c * cw:(c + 1) * cw], preferred_element_type=F32)
        t = jnp.maximum(t, 0.0)
        acc = acc + jnp.dot((t * t).astype(BF16), w2_ref[c * cw:(c + 1) * cw, :],
                            preferred_element_type=F32)
    y_ref[...] = acc


def _out(x2, o_attn, o_rnn, gates, bg, pa, pr, wo, n2, w1, w2, tm):
    T, D = x2.shape
    d_rnn = o_rnn.shape[1]
    grid = (T // tm,)
    return pl.pallas_call(
        _out_body,
        grid=grid,
        in_specs=[
            pl.BlockSpec((tm, D), lambda t: (t, 0)),
            pl.BlockSpec((tm, D_ATTN), lambda t: (t, 0)),
            pl.BlockSpec((tm, d_rnn), lambda t: (t, 0)),
            pl.BlockSpec((tm, D), lambda t: (t, 0)),
            pl.BlockSpec((tm, D), lambda t: (t, 1)),
            _const_spec(bg.shape), _const_spec(pa.shape), _const_spec(pr.shape),
            _const_spec(wo.shape), _const_spec(n2.shape), _const_spec(w1.shape),
            _const_spec(w2.shape),
        ],
        out_specs=pl.BlockSpec((tm, D), lambda t: (t, 0)),
        out_shape=jax.ShapeDtypeStruct((T, D), F32),
        compiler_params=pltpu.CompilerParams(
            dimension_semantics=("parallel",), vmem_limit_bytes=VMEM_LIMIT),
        name="merge_mlp",
    )(x2, o_attn, o_rnn, gates, gates, bg, pa, pr, wo, n2, w1, w2)


def _t5_bucket(rel):
    max_exact = NUM_BUCKETS // 2
    n = jnp.maximum(rel, 0)
    nf = jnp.maximum(n, 1).astype(F32)
    large = max_exact + (jnp.log(nf / max_exact) / math.log(MAX_DISTANCE / max_exact)
                         * (NUM_BUCKETS - max_exact)).astype(jnp.int32)
    large = jnp.minimum(large, NUM_BUCKETS - 1)
    return jnp.where(n < max_exact, n, large)


def _bias_tables(rel_bias):
    offs = jnp.arange(BLOCK)
    rel_own = offs[None, :] - offs[:, None]
    buckets = jnp.arange(NUM_BUCKETS)

    def lookup(rel):
        onehot = (_t5_bucket(rel)[..., None] == buckets).astype(F32)
        return jnp.einsum('kqb,bh->hkq', onehot, rel_bias, precision=lax.Precision.HIGHEST)

    bias_d = jnp.where(rel_own >= 0, lookup(rel_own) * LOG2E, NEG)
    bias_p = lookup(rel_own + BLOCK) * LOG2E
    bias_far = rel_bias[NUM_BUCKETS - 1] * LOG2E
    return bias_d.astype(F32), bias_p.astype(F32), bias_far.astype(F32)


def _block_diag(w):
    per = GROUP // RNN_BLOCK_DIM
    w4 = w.reshape(RNN_BLOCKS // per, per, RNN_BLOCK_DIM, RNN_BLOCK_DIM)
    eye = jnp.eye(per, dtype=w.dtype)
    return jnp.einsum('gade,ab->gadbe', w4, eye).reshape(RNN_BLOCKS // per, GROUP, GROUP)


def _layer(x, norm1_w, w_in, b_gate, q_norm_w, k_norm_w, bias_tabs, conv_w, conv_b,
           w_rg_a, b_rg_a, w_rg_i, b_rg_i, lru_lambda, w_proj_attn, w_proj_rnn,
           w_out, norm2_w, w_ff1, w_ff2, tm_in, tr, tm_out):
    B, S, D = x.shape
    d_rnn = conv_w.shape[1]
    c0, c1, c2 = D_ATTN, 2 * D_ATTN, 3 * D_ATTN
    wqT = w_in[:, :c0].T.astype(BF16)
    wk = w_in[:, c0:c1].astype(BF16)
    wvT = w_in[:, c1:c2].T.astype(BF16)
    c3 = c2 + 2 * d_rnn
    wxy = w_in[:, c2:c3].astype(BF16)
    wg = w_in[:, c3:].astype(BF16)
    qw = (q_norm_w * (HEAD_DIM ** -0.5 * LOG2E)).reshape(1, HEAD_DIM, 1)
    kw = jnp.tile(k_norm_w, N_HEADS).reshape(1, D_ATTN)
    head_of = np.arange(D_ATTN) // HEAD_DIM
    gmat = jnp.asarray(head_of[:, None] == head_of[None, :], dtype=BF16)

    qT, k, vT, kmean, gates, o_rnn = _inproj_rnn(
        x, norm1_w.reshape(1, D), wqT, wk, wvT, wxy, wg, qw, kw, gmat, conv_w, conv_b.reshape(1, d_rnn),
        _block_diag(w_rg_a).astype(BF16), _block_diag(w_rg_i).astype(BF16),
        b_rg_a.reshape(1, d_rnn), b_rg_i.reshape(1, d_rnn), lru_lambda.reshape(1, d_rnn), tm_in, tr)
    o_attn = _attention(qT, k, vT, kmean.reshape(B, S // BLOCK, D_ATTN), *bias_tabs, HEADS_PER_STEP)
    y = _out(x.reshape(B * S, D), o_attn.reshape(B * S, D_ATTN), o_rnn.reshape(B * S, d_rnn),
             gates.reshape(B * S, -1), b_gate, w_proj_attn.astype(BF16), w_proj_rnn.astype(BF16),
             w_out.astype(BF16), norm2_w.reshape(1, D), w_ff1.astype(BF16), w_ff2.astype(BF16), tm_out)
    return y.reshape(B, S, D)


def kernel(x, norm1_w, w_in, b_gate, q_norm_w, k_norm_w, rel_bias, conv_w, conv_b, w_rg_a, b_rg_a,
           w_rg_i, b_rg_i, lru_lambda, w_proj_attn, w_proj_rnn, w_out, norm2_w, w_ff1, w_ff2):
    S = x.shape[1]
    assert S % BLOCK == 0
    bias_tabs = _bias_tables(rel_bias)
    tm_in = min(512, S)
    tr = min(256, S)
    tm_out = min(256, S)
    for l in range(norm1_w.shape[0]):
        x = _layer(x, norm1_w[l], w_in[l], b_gate[l], q_norm_w[l], k_norm_w[l], bias_tabs,
                   conv_w[l], conv_b[l], w_rg_a[l], b_rg_a[l], w_rg_i[l], b_rg_i[l], lru_lambda[l],
                   w_proj_attn[l], w_proj_rnn[l], w_out[l], norm2_w[l], w_ff1[l], w_ff2[l],
                   tm_in, tr, tm_out)
    return x
```

```python
import math

import numpy as np
import jax
import jax.numpy as jnp
from jax import lax
from jax.experimental import pallas as pl
from jax.experimental.pallas import tpu as pltpu

N_HEADS = 8
HEAD_DIM = 64
D_ATTN = N_HEADS * HEAD_DIM
BLOCK = 256
TOPK = 3
NUM_BUCKETS = 32
MAX_DISTANCE = 128
RNN_BLOCKS = 16
RNN_BLOCK_DIM = 64
CONV_WIDTH = 4
LRU_C = 8.0
EPS = 1e-6
NEG = -1e30
LOG2E = math.log2(math.e)
HEADS_PER_STEP = 8
TILE_HEADS = 4
GROUP = 256
SUBLANES = 8
BF16_ROWS = 16
V_ROWS = HEAD_DIM + BF16_ROWS
VMEM_LIMIT = 56 * 1024 * 1024

F32 = jnp.float32
BF16 = jnp.bfloat16
_NT = (((1,), (1,)), ((), ()))


def _const_spec(shape):
    nd = len(shape)
    return pl.BlockSpec(shape, lambda *_: (0,) * nd, pipeline_mode=pl.Buffered(1))


def _inproj_rnn_body(x_ref, n1_ref, wqT_ref, wk_ref, wvT_ref, wxy_ref, wg_ref, qw_ref, kw_ref, g_ref,
                     perm_ref, permT_ref, cw_ref, cb_ref, wa_ref, wi_ref, ba_ref, bi_ref, lam_ref,
                     w1_ref, w2_ref,
                     qT_ref, k_ref, vT_ref, km_ref, gates_ref, orn_ref, w1b_ref, w2b_ref,
                     xs_ref, y_ref, xc_ref, a_ref, u_ref, tail_ref, hc_ref):
    tm = x_ref.shape[1]
    d = cw_ref.shape[1]
    tr = perm_ref.shape[0]
    seg = tr // SUBLANES
    taps = CONV_WIDTH - 1

    @pl.when(pl.program_id(1) == 0)
    def _():
        tail_ref[...] = jnp.zeros_like(tail_ref)
        hc_ref[...] = jnp.zeros_like(hc_ref)

    w1b_ref[...] = w1_ref[...].astype(BF16)
    w2b_ref[...] = w2_ref[...].astype(BF16)

    x = x_ref[0]
    ms = jnp.mean(x * x, axis=-1, keepdims=True)
    h = (x * lax.rsqrt(ms + EPS) * n1_ref[...]).astype(BF16)

    def rnn_inputs(t):
        hp = jnp.dot(perm_ref[...], h[t * tr:(t + 1) * tr], preferred_element_type=F32).astype(BF16)
        xs_ref[t] = jnp.dot(hp, wxy_ref[:, :d], preferred_element_type=F32)
        y_ref[t] = jnp.dot(hp, wxy_ref[:, d:], preferred_element_type=F32)

    def rnn_conv(t):
        def slab(k):
            return xs_ref[t, k * SUBLANES:(k + 1) * SUBLANES, :]

        first_seg = lax.broadcasted_iota(jnp.int32, (SUBLANES, d), 0) == 0
        last = [slab(seg - taps + j) for j in range(taps)]
        window = [jnp.where(first_seg, pltpu.roll(tail_ref[j], 1, 0), pltpu.roll(last[j], 1, 0))
                  for j in range(taps)]
        for j in range(taps):
            tail_ref[j] = last[j]
        cw = [cw_ref[j:j + 1, :] for j in range(CONV_WIDTH)]
        cb = cb_ref[...]
        for k in range(seg):
            cur = slab(k)
            acc = cb + window[0] * cw[0]
            for j in range(1, taps):
                acc = acc + window[j] * cw[j]
            xc_ref[t, k * SUBLANES:(k + 1) * SUBLANES, :] = acc + cur * cw[taps]
            window = window[1:] + [cur]

    def rnn_gate_mm(t):
        xb = xc_ref[t].astype(BF16)
        for g in range(d // GROUP):
            cols = slice(g * GROUP, (g + 1) * GROUP)
            a_ref[t, :, cols] = jnp.dot(xb[:, cols], wa_ref[g], preferred_element_type=F32)
            u_ref[t, :, cols] = jnp.dot(xb[:, cols], wi_ref[g], preferred_element_type=F32)

    def rnn_nonlin(t):
        xc = xc_ref[t]
        t_r = jnp.tanh(0.5 * (a_ref[t] + ba_ref[...]))
        t_i = jnp.tanh(0.5 * (u_ref[t] + bi_ref[...]))
        nlam = -lam_ref[...]
        softplus = jnp.maximum(nlam, 0.0) + jnp.log1p(jnp.exp(-jnp.abs(nlam)))
        half_c = softplus * (-0.5 * LRU_C * LOG2E)
        a = jnp.exp2(t_r * half_c + half_c)
        a_ref[t] = a
        gap = 1.0 - a * a
        half_x = 0.5 * xc
        u_ref[t] = (gap * lax.rsqrt(jnp.maximum(gap, 1e-30))) * (half_x * t_i + half_x)

    def rnn_scan(t):
        hcur = u_ref[t, 0:SUBLANES, :]
        prod = a_ref[t, 0:SUBLANES, :]
        for k in range(1, seg):
            rows = slice(k * SUBLANES, (k + 1) * SUBLANES)
            a_k = a_ref[t, rows, :]
            hcur = a_k * hcur + u_ref[t, rows, :]
            prod = a_k * prod
            u_ref[t, rows, :] = hcur
            a_ref[t, rows, :] = prod
        entry = [hc_ref[...]]
        for s in range(SUBLANES):
            entry.append(prod[s:s + 1, :] * entry[s] + hcur[s:s + 1, :])
        hc_ref[...] = entry[SUBLANES]
        h_in = jnp.concatenate(entry[:SUBLANES], axis=0)
        for k in range(seg):
            rows = slice(k * SUBLANES, (k + 1) * SUBLANES)
            u_ref[t, rows, :] = a_ref[t, rows, :] * h_in + u_ref[t, rows, :]

    def rnn_output(t):
        y = y_ref[t]
        k1 = -2.0 * math.sqrt(2.0 / math.pi) * LOG2E
        gelu = y / (1.0 + jnp.exp2(y * (k1 + (k1 * 0.044715) * (y * y))))
        o_slab = (u_ref[t] * gelu).astype(BF16)
        orn_ref[0, t * tr:(t + 1) * tr, :] = jnp.dot(
            permT_ref[...], o_slab, preferred_element_type=F32).astype(BF16)

    def proj_q():
        qT = lax.dot_general(wqT_ref[...], h, _NT, preferred_element_type=F32)
        q3 = qT.reshape(N_HEADS, HEAD_DIM, tm)
        qss = jnp.mean(q3 * q3, axis=1, keepdims=True)
        qn = q3 * lax.rsqrt(qss + EPS) * qw_ref[...]
        qT_ref[0] = qn.reshape(D_ATTN, tm).astype(BF16)

    def proj_v():
        vT = lax.dot_general(wvT_ref[...], h, _NT, preferred_element_type=F32).astype(BF16)
        ones_rows = (lax.broadcasted_iota(jnp.int32, (BF16_ROWS, tm), 0) == 0).astype(BF16)
        for hd in range(N_HEADS):
            vT_ref[0, hd * V_ROWS:hd * V_ROWS + HEAD_DIM, :] = vT[hd * HEAD_DIM:(hd + 1) * HEAD_DIM]
            vT_ref[0, hd * V_ROWS + HEAD_DIM:(hd + 1) * V_ROWS, :] = ones_rows

    def proj_k():
        k = jnp.dot(h, wk_ref[...], preferred_element_type=F32)
        k2 = k * k
        hi = k2.astype(BF16)
        lo = (k2 - hi.astype(F32)).astype(BF16)
        kss = (jnp.dot(hi, g_ref[...], preferred_element_type=F32)
               + jnp.dot(lo, g_ref[...], preferred_element_type=F32))
        kn = k * lax.rsqrt(kss * (1.0 / HEAD_DIM) + EPS) * kw_ref[...]
        k_ref[0] = kn.astype(BF16)
        for bi in range(tm // BLOCK):
            km_ref[0, bi] = jnp.mean(kn[bi * BLOCK:(bi + 1) * BLOCK], axis=0, keepdims=True)

    gate_chunk = 512

    def proj_gate(c):
        cols = slice(c * gate_chunk, (c + 1) * gate_chunk)
        gates_ref[0, :, cols] = jnp.dot(h, wg_ref[:, cols], preferred_element_type=F32).astype(BF16)

    assert tm == 2 * tr and wg_ref.shape[1] == 4 * gate_chunk
    rnn_inputs(0)
    proj_q()
    rnn_conv(0)
    rnn_gate_mm(0)
    proj_v()
    proj_gate(0)
    rnn_inputs(1)
    rnn_nonlin(0)
    rnn_scan(0)
    rnn_output(0)
    proj_k()
    rnn_conv(1)
    rnn_gate_mm(1)
    proj_gate(1)
    proj_gate(2)
    proj_gate(3)
    rnn_nonlin(1)
    rnn_scan(1)
    rnn_output(1)


def _inproj_rnn(x, n1, wqT, wk, wvT, wxy, wg, qw, kw, gmat, conv_w, conv_b, wa_bd, wi_bd, b_a, b_i, lam,
                w_ff1, w_ff2, tm, tr):
    B, S, D = x.shape
    d = conv_w.shape[1]
    d_g = wg.shape[1]
    nb = S // BLOCK
    n_sub = tm // tr
    grid = (B, S // tm)
    seg = tr // SUBLANES
    t = np.arange(tr)
    perm_np = np.zeros((tr, tr), np.float32)
    perm_np[(t % seg) * SUBLANES + t // seg, t] = 1.0
    perm = jnp.asarray(perm_np, BF16)
    permT = jnp.asarray(perm_np.T, BF16)
    consts = (n1, wqT, wk, wvT, wxy, wg, qw, kw, gmat, perm, permT, conv_w, conv_b, wa_bd, wi_bd, b_a, b_i, lam)
    n_s = S // tm
    n_steps = B * n_s
    r1, r2 = w_ff1.shape[0] // n_steps, w_ff2.shape[0] // n_steps
    assert r1 * n_steps == w_ff1.shape[0] and r2 * n_steps == w_ff2.shape[0]
    assert r1 % BF16_ROWS == 0 and r2 % BF16_ROWS == 0
    w1_spec = pl.BlockSpec((r1, w_ff1.shape[1]), lambda b, s: (b * n_s + s, 0))
    w2_spec = pl.BlockSpec((r2, w_ff2.shape[1]), lambda b, s: (b * n_s + s, 0))
    return pl.pallas_call(
        _inproj_rnn_body,
        grid=grid,
        in_specs=([pl.BlockSpec((1, tm, D), lambda b, s: (b, s, 0))] + [_const_spec(c.shape) for c in consts]
                  + [w1_spec, w2_spec]),
        out_specs=[
            pl.BlockSpec((1, D_ATTN, tm), lambda b, s: (b, 0, s)),
            pl.BlockSpec((1, tm, D_ATTN), lambda b, s: (b, s, 0)),
            pl.BlockSpec((1, N_HEADS * V_ROWS, tm), lambda b, s: (b, 0, s)),
            pl.BlockSpec((1, tm // BLOCK, 1, D_ATTN), lambda b, s: (b, s, 0, 0)),
            pl.BlockSpec((1, tm, d_g), lambda b, s: (b, s, 0)),
            pl.BlockSpec((1, tm, d), lambda b, s: (b, s, 0)),
            w1_spec, w2_spec,
        ],
        out_shape=[
            jax.ShapeDtypeStruct((B, D_ATTN, S), BF16),
            jax.ShapeDtypeStruct((B, S, D_ATTN), BF16),
            jax.ShapeDtypeStruct((B, N_HEADS * V_ROWS, S), BF16),
            jax.ShapeDtypeStruct((B, nb, 1, D_ATTN), F32),
            jax.ShapeDtypeStruct((B, S, d_g), BF16),
            jax.ShapeDtypeStruct((B, S, d), BF16),
            jax.ShapeDtypeStruct(w_ff1.shape, BF16),
            jax.ShapeDtypeStruct(w_ff2.shape, BF16),
        ],
        scratch_shapes=[pltpu.VMEM((n_sub, tr, d), F32) for _ in range(5)] + [
            pltpu.VMEM((CONV_WIDTH - 1, SUBLANES, d), F32), pltpu.VMEM((1, d), F32)],
        compiler_params=pltpu.CompilerParams(
            dimension_semantics=("parallel", "arbitrary"), vmem_limit_bytes=VMEM_LIMIT),
        name="inproj_rglru",
    )(x, *consts, w_ff1, w_ff2)


def _attn_body(far_ref, qT_ref, k_ref, vT_ref, km_ref, bd_ref, bp_ref, o_ref, sel_ref, qk_ref, m_ref, acc_ref):
    p = pl.program_id(1)
    i = pl.program_id(2)
    nb = km_ref.shape[1]
    hps = bd_ref.shape[0]
    heads = range(hps)
    groups = range(hps // TILE_HEADS)
    lanes = TILE_HEADS * HEAD_DIM
    row = lax.broadcasted_iota(jnp.int32, (lanes, BLOCK), 0)
    blk = lax.broadcasted_iota(jnp.int32, (nb, BLOCK), 0)
    lane_head = lax.broadcasted_iota(jnp.int32, (nb, lanes), 1) // HEAD_DIM
    past = blk < i
    qz = []
    for g in groups:
        qall = qT_ref[0, g * lanes:(g + 1) * lanes, :]
        qz += [jnp.where((row >= t * HEAD_DIM) & (row < (t + 1) * HEAD_DIM), qall, jnp.zeros_like(qall))
               for t in range(TILE_HEADS)]

        km = km_ref[0, :, g * lanes:(g + 1) * lanes]
        km_hi = km.astype(BF16)
        km_r = km - km_hi.astype(F32)
        km_mid = km_r.astype(BF16)
        km_lo = (km_r - km_mid.astype(F32)).astype(BF16)
        stacked = jnp.concatenate(
            [jnp.where(lane_head == t, piece, jnp.zeros_like(piece))
             for piece in (km_hi, km_mid, km_lo) for t in range(TILE_HEADS)], axis=0)
        gates = jnp.dot(stacked, qall, preferred_element_type=F32)
        for t in range(TILE_HEADS):
            hh = g * TILE_HEADS + t
            gate = (gates[t * nb:(t + 1) * nb] + gates[(TILE_HEADS + t) * nb:(TILE_HEADS + t + 1) * nb]
                    + gates[(2 * TILE_HEADS + t) * nb:(2 * TILE_HEADS + t + 1) * nb])
            gate = jnp.where(past, gate, -jnp.inf)
            rank = jnp.zeros((nb, BLOCK), jnp.int32)
            for m in range(nb):
                gm = gate[m:m + 1, :]
                rank = rank + jnp.where(gm > gate, 1, jnp.where((gm == gate) & (blk > m), 1, 0))
            sel_ref[hh] = jnp.where(past & (rank < TOPK), far_ref[p * hps + hh], NEG)

    def keys(j, g):
        return k_ref[0, pl.ds(pl.multiple_of(j * BLOCK, BLOCK), BLOCK), g * lanes:(g + 1) * lanes]

    def values(j, hh):
        return vT_ref[0, hh * V_ROWS:(hh + 1) * V_ROWS,
                      pl.ds(pl.multiple_of(j * BLOCK, BLOCK), BLOCK)]

    def issue(slot, j):
        for g in groups:
            kb = keys(j, g)
            for hh in range(g * TILE_HEADS, (g + 1) * TILE_HEADS):
                qk_ref[slot, hh] = jnp.dot(kb, qz[hh], preferred_element_type=F32)

    def step(slot, j, bias, row, nxt_slot, nxt_j):
        for g in groups:
            kb = keys(nxt_j, g)
            for hh in range(g * TILE_HEADS, (g + 1) * TILE_HEADS):
                qk_ref[nxt_slot, hh] = jnp.dot(kb, qz[hh], preferred_element_type=F32)
                m_prev = m_ref[hh]
                s = qk_ref[slot, hh] if bias is None else qk_ref[slot, hh] + bias(hh)
                r = row(hh)
                m_new = jnp.maximum(m_prev, jnp.max(s, axis=0, keepdims=True) + r)
                m_ref[hh] = m_new
                alpha = jnp.exp2(m_prev - m_new)
                pb = jnp.exp2((s - (m_new - r)).astype(BF16))
                acc_ref[hh] = alpha * acc_ref[hh] + jnp.dot(values(j, hh), pb, preferred_element_type=F32)

    jp = jnp.maximum(i - 1, 0)
    n_old = jp
    m_ref[...] = jnp.full(m_ref.shape, NEG, F32)
    acc_ref[...] = jnp.zeros(acc_ref.shape, F32)
    zero_row = jnp.zeros((1, BLOCK), F32)
    issue(0, i)
    step(0, i, lambda hh: bd_ref[hh], lambda hh: zero_row, 1, jp)
    step(1, jp, lambda hh: bp_ref[hh],
         lambda hh: jnp.where(sel_ref[hh, pl.ds(jp, 1), :] > 0.5 * NEG, 0.0, NEG), 0, 0)

    def older_pair(t, carry):
        ja = 2 * t
        jb = ja + 1
        step(0, ja, None, lambda hh: sel_ref[hh, pl.ds(ja, 1), :], 1, jb)
        step(1, jb, None, lambda hh: jnp.where(jb < n_old, sel_ref[hh, pl.ds(jb, 1), :], NEG), 0, ja + 2)
        return carry

    lax.fori_loop(0, (n_old + 1) // 2, older_pair, 0)
    outs = [acc_ref[hh, :HEAD_DIM, :] * (1.0 / acc_ref[hh, HEAD_DIM:HEAD_DIM + 1, :]) for hh in heads]
    o_ref[0] = jnp.concatenate(outs, axis=0).T.astype(BF16)


def _attention(qT, k, vT, kmean, bias_d, bias_p, bias_far, hps):
    B, _, S = qT.shape
    nb = S // BLOCK
    lanes = hps * HEAD_DIM
    grid = (B, N_HEADS // hps, nb)
    assert hps % TILE_HEADS == 0
    return pl.pallas_call(
        _attn_body,
        grid=grid,
        in_specs=[
            pl.BlockSpec(memory_space=pltpu.SMEM),
            pl.BlockSpec((1, lanes, BLOCK), lambda b, p, i: (b, p, i)),
            pl.BlockSpec((1, S, lanes), lambda b, p, i: (b, 0, p)),
            pl.BlockSpec((1, hps * V_ROWS, S), lambda b, p, i: (b, p, 0)),
            pl.BlockSpec((1, nb, lanes), lambda b, p, i: (b, 0, p)),
            pl.BlockSpec((hps, BLOCK, BLOCK), lambda b, p, i: (p, 0, 0)),
            pl.BlockSpec((hps, BLOCK, BLOCK), lambda b, p, i: (p, 0, 0)),
        ],
        out_specs=pl.BlockSpec((1, BLOCK, lanes), lambda b, p, i: (b, i, p)),
        out_shape=jax.ShapeDtypeStruct((B, S, D_ATTN), BF16),
        scratch_shapes=[pltpu.VMEM((hps, nb, BLOCK), F32), pltpu.VMEM((2, hps, BLOCK, BLOCK), F32),
                        pltpu.VMEM((hps, 1, BLOCK), F32), pltpu.VMEM((hps, V_ROWS, BLOCK), F32)],
        compiler_params=pltpu.CompilerParams(
            dimension_semantics=("parallel", "parallel", "arbitrary"), vmem_limit_bytes=VMEM_LIMIT),
        name="moba_attn",
    )(bias_far, qT, k, vT, kmean, bias_d, bias_p)


def _out_body(x_ref, oa_ref, or_ref, ga_ref, gr_ref, bg_ref, pa_ref, pr_ref, wo_ref, n2_ref,
              w1_ref, w2_ref, y_ref):
    a = jnp.dot(oa_ref[...], pa_ref[...], preferred_element_type=F32)
    r = jnp.dot(or_ref[...], pr_ref[...], preferred_element_type=F32)
    g_a = jax.nn.sigmoid(ga_ref[...].astype(F32) + bg_ref[0:1, :])
    g_r = jax.nn.sigmoid(gr_ref[...].astype(F32) + bg_ref[1:2, :])
    merged = (g_a * a + g_r * r).astype(BF16)
    x1 = x_ref[...] + jnp.dot(merged, wo_ref[...], preferred_element_type=F32)
    ms = jnp.mean(x1 * x1, axis=-1, keepdims=True)
    h2 = (x1 * lax.rsqrt(ms + EPS) * n2_ref[...]).astype(BF16)
    d_ff = w1_ref.shape[1]
    cw = 1024
    acc = x1
    for c in range(d_ff // cw):
        t = jnp.dot(h2, w1_ref[:, c * cw:(c + 1) * cw], preferred_element_type=F32)
        t = jnp.maximum(t, 0.0)
        acc = acc + jnp.dot((t * t).astype(BF16), w2_ref[c * cw:(c + 1) * cw, :],
                            preferred_element_type=F32)
    y_ref[...] = acc


def _out(x2, o_attn, o_rnn, gates, bg, pa, pr, wo, n2, w1, w2, tm):
    T, D = x2.shape
    d_rnn = o_rnn.shape[1]
    grid = (T // tm,)
    return pl.pallas_call(
        _out_body,
        grid=grid,
        in_specs=[
            pl.BlockSpec((tm, D), lambda t: (t, 0)),
            pl.BlockSpec((tm, D_ATTN), lambda t: (t, 0)),
            pl.BlockSpec((tm, d_rnn), lambda t: (t, 0)),
            pl.BlockSpec((tm, D), lambda t: (t, 0)),
            pl.BlockSpec((tm, D), lambda t: (t, 1)),
            _const_spec(bg.shape), _const_spec(pa.shape), _const_spec(pr.shape),
            _const_spec(wo.shape), _const_spec(n2.shape), _const_spec(w1.shape),
            _const_spec(w2.shape),
        ],
        out_specs=pl.BlockSpec((tm, D), lambda t: (t, 0)),
        out_shape=jax.ShapeDtypeStruct((T, D), F32),
        compiler_params=pltpu.CompilerParams(
            dimension_semantics=("parallel",), vmem_limit_bytes=VMEM_LIMIT),
        name="merge_mlp",
    )(x2, o_attn, o_rnn, gates, gates, bg, pa, pr, wo, n2, w1, w2)


def _t5_bucket(rel):
    max_exact = NUM_BUCKETS // 2
    n = jnp.maximum(rel, 0)
    nf = jnp.maximum(n, 1).astype(F32)
    large = max_exact + (jnp.log(nf / max_exact) / math.log(MAX_DISTANCE / max_exact)
                         * (NUM_BUCKETS - max_exact)).astype(jnp.int32)
    large = jnp.minimum(large, NUM_BUCKETS - 1)
    return jnp.where(n < max_exact, n, large)


def _bias_tables(rel_bias):
    offs = jnp.arange(BLOCK)
    rel_own = offs[None, :] - offs[:, None]
    buckets = jnp.arange(NUM_BUCKETS)

    def lookup(rel):
        onehot = (_t5_bucket(rel)[..., None] == buckets).astype(F32)
        return jnp.einsum('kqb,bh->hkq', onehot, rel_bias, precision=lax.Precision.HIGHEST)

    bias_d = jnp.where(rel_own >= 0, lookup(rel_own) * LOG2E, NEG)
    bias_p = lookup(rel_own + BLOCK) * LOG2E
    bias_far = rel_bias[NUM_BUCKETS - 1] * LOG2E
    return bias_d.astype(F32), bias_p.astype(F32), bias_far.astype(F32)


def _block_diag(w):
    per = GROUP // RNN_BLOCK_DIM
    w4 = w.reshape(RNN_BLOCKS // per, per, RNN_BLOCK_DIM, RNN_BLOCK_DIM)
    eye = jnp.eye(per, dtype=w.dtype)
    return jnp.einsum('gade,ab->gadbe', w4, eye).reshape(RNN_BLOCKS // per, GROUP, GROUP)


def _layer(x, norm1_w, w_in, b_gate, q_norm_w, k_norm_w, bias_tabs, conv_w, conv_b,
           w_rg_a, b_rg_a, w_rg_i, b_rg_i, lru_lambda, w_proj_attn, w_proj_rnn,
           w_out, norm2_w, w_ff1, w_ff2, tm_in, tr, tm_out):
    B, S, D = x.shape
    d_rnn = conv_w.shape[1]
    c0, c1, c2 = D_ATTN, 2 * D_ATTN, 3 * D_ATTN
    wqT = w_in[:, :c0].T.astype(BF16)
    wk = w_in[:, c0:c1].astype(BF16)
    wvT = w_in[:, c1:c2].T.astype(BF16)
    c3 = c2 + 2 * d_rnn
    wxy = w_in[:, c2:c3].astype(BF16)
    wg = w_in[:, c3:].astype(BF16)
    qw = (q_norm_w * (HEAD_DIM ** -0.5 * LOG2E)).reshape(1, HEAD_DIM, 1)
    kw = jnp.tile(k_norm_w, N_HEADS).reshape(1, D_ATTN)
    head_of = np.arange(D_ATTN) // HEAD_DIM
    gmat = jnp.asarray(head_of[:, None] == head_of[None, :], dtype=BF16)

    qT, k, vT, kmean, gates, o_rnn, w1b, w2b = _inproj_rnn(
        x, norm1_w.reshape(1, D), wqT, wk, wvT, wxy, wg, qw, kw, gmat, conv_w, conv_b.reshape(1, d_rnn),
        _block_diag(w_rg_a).astype(BF16), _block_diag(w_rg_i).astype(BF16),
        b_rg_a.reshape(1, d_rnn), b_rg_i.reshape(1, d_rnn), lru_lambda.reshape(1, d_rnn),
        w_ff1, w_ff2, tm_in, tr)
    o_attn = _attention(qT, k, vT, kmean.reshape(B, S // BLOCK, D_ATTN), *bias_tabs, HEADS_PER_STEP)
    y = _out(x.reshape(B * S, D), o_attn.reshape(B * S, D_ATTN), o_rnn.reshape(B * S, d_rnn),
             gates.reshape(B * S, -1), b_gate, w_proj_attn.astype(BF16), w_proj_rnn.astype(BF16),
             w_out.astype(BF16), norm2_w.reshape(1, D), w1b, w2b, tm_out)
    return y.reshape(B, S, D)


def kernel(x, norm1_w, w_in, b_gate, q_norm_w, k_norm_w, rel_bias, conv_w, conv_b, w_rg_a, b_rg_a,
           w_rg_i, b_rg_i, lru_lambda, w_proj_attn, w_proj_rnn, w_out, norm2_w, w_ff1, w_ff2):
    S = x.shape[1]
    assert S % BLOCK == 0
    bias_tabs = _bias_tables(rel_bias)
    tm_in = min(512, S)
    tr = min(256, S)
    tm_out = min(512, S)
    for l in range(norm1_w.shape[0]):
        x = _layer(x, norm1_w[l], w_in[l], b_gate[l], q_norm_w[l], k_norm_w[l], bias_tabs,
                   conv_w[l], conv_b[l], w_rg_a[l], b_rg_a[l], w_rg_i[l], b_rg_i[l], lru_lambda[l],
                   w_proj_attn[l], w_proj_rnn[l], w_out[l], norm2_w[l], w_ff1[l], w_ff2[l],
                   tm_in, tr, tm_out)
    return x
```

```python
import math

import numpy as np
import jax
import jax.numpy as jnp
from jax import lax
from jax.experimental import pallas as pl
from jax.experimental.pallas import tpu as pltpu

N_HEADS = 8
HEAD_DIM = 64
D_ATTN = N_HEADS * HEAD_DIM
BLOCK = 256
TOPK = 3
NUM_BUCKETS = 32
MAX_DISTANCE = 128
RNN_BLOCKS = 16
RNN_BLOCK_DIM = 64
CONV_WIDTH = 4
LRU_C = 8.0
EPS = 1e-6
NEG = -1e30
LOG2E = math.log2(math.e)
HEADS_PER_STEP = 8
TILE_HEADS = 4
GROUP = 256
SUBLANES = 8
BF16_ROWS = 16
V_ROWS = HEAD_DIM + BF16_ROWS
VMEM_LIMIT = 56 * 1024 * 1024

F32 = jnp.float32
BF16 = jnp.bfloat16
_NT = (((1,), (1,)), ((), ()))


def _const_spec(shape):
    nd = len(shape)
    return pl.BlockSpec(shape, lambda *_: (0,) * nd, pipeline_mode=pl.Buffered(1))


def _inproj_rnn_body(x_ref, n1_ref, wqT_ref, wk_ref, wvT_ref, wxy_ref, wg_ref, qw_ref, kw_ref, g_ref,
                     perm_ref, permT_ref, cw_ref, cb_ref, wa_ref, wi_ref, ba_ref, bi_ref, lam_ref,
                     w1_ref, w2_ref,
                     qT_ref, k_ref, vT_ref, km_ref, gates_ref, orn_ref, w1b_ref, w2b_ref,
                     xs_ref, y_ref, xc_ref, a_ref, u_ref, tail_ref, hc_ref):
    tm = x_ref.shape[1]
    d = cw_ref.shape[1]
    tr = perm_ref.shape[0]
    seg = tr // SUBLANES
    taps = CONV_WIDTH - 1

    @pl.when(pl.program_id(1) == 0)
    def _():
        tail_ref[...] = jnp.zeros_like(tail_ref)
        hc_ref[...] = jnp.zeros_like(hc_ref)

    w1b_ref[...] = w1_ref[...].astype(BF16)
    w2b_ref[...] = w2_ref[...].astype(BF16)

    x = x_ref[0]
    ms = jnp.mean(x * x, axis=-1, keepdims=True)
    h = (x * lax.rsqrt(ms + EPS) * n1_ref[...]).astype(BF16)

    def rnn_inputs(t):
        hp = jnp.dot(perm_ref[...], h[t * tr:(t + 1) * tr], preferred_element_type=F32).astype(BF16)
        xs_ref[t] = jnp.dot(hp, wxy_ref[:, :d], preferred_element_type=F32)
        y_ref[t] = jnp.dot(hp, wxy_ref[:, d:], preferred_element_type=F32)

    def rnn_conv(t):
        def slab(k):
            return xs_ref[t, k * SUBLANES:(k + 1) * SUBLANES, :]

        first_seg = lax.broadcasted_iota(jnp.int32, (SUBLANES, d), 0) == 0
        last = [slab(seg - taps + j) for j in range(taps)]
        window = [jnp.where(first_seg, pltpu.roll(tail_ref[j], 1, 0), pltpu.roll(last[j], 1, 0))
                  for j in range(taps)]
        for j in range(taps):
            tail_ref[j] = last[j]
        cw = [cw_ref[j:j + 1, :] for j in range(CONV_WIDTH)]
        cb = cb_ref[...]
        for k in range(seg):
            cur = slab(k)
            acc = cb + window[0] * cw[0]
            for j in range(1, taps):
                acc = acc + window[j] * cw[j]
            xc_ref[t, k * SUBLANES:(k + 1) * SUBLANES, :] = acc + cur * cw[taps]
            window = window[1:] + [cur]

    def rnn_gate_mm(t):
        xb = xc_ref[t].astype(BF16)
        for g in range(d // GROUP):
            cols = slice(g * GROUP, (g + 1) * GROUP)
            a_ref[t, :, cols] = jnp.dot(xb[:, cols], wa_ref[g], preferred_element_type=F32)
            u_ref[t, :, cols] = jnp.dot(xb[:, cols], wi_ref[g], preferred_element_type=F32)

    def rnn_nonlin(t):
        xc = xc_ref[t]
        t_r = jnp.tanh(0.5 * (a_ref[t] + ba_ref[...]))
        t_i = jnp.tanh(0.5 * (u_ref[t] + bi_ref[...]))
        nlam = -lam_ref[...]
        softplus = jnp.maximum(nlam, 0.0) + jnp.log1p(jnp.exp(-jnp.abs(nlam)))
        half_c = softplus * (-0.5 * LRU_C * LOG2E)
        a = jnp.exp2(t_r * half_c + half_c)
        a_ref[t] = a
        gap = 1.0 - a * a
        half_x = 0.5 * xc
        u_ref[t] = (gap * lax.rsqrt(jnp.maximum(gap, 1e-30))) * (half_x * t_i + half_x)

    def rnn_scan(t):
        hcur = u_ref[t, 0:SUBLANES, :]
        prod = a_ref[t, 0:SUBLANES, :]
        for k in range(1, seg):
            rows = slice(k * SUBLANES, (k + 1) * SUBLANES)
            a_k = a_ref[t, rows, :]
            hcur = a_k * hcur + u_ref[t, rows, :]
            prod = a_k * prod
            u_ref[t, rows, :] = hcur
            a_ref[t, rows, :] = prod
        entry = [hc_ref[...]]
        for s in range(SUBLANES):
            entry.append(prod[s:s + 1, :] * entry[s] + hcur[s:s + 1, :])
        hc_ref[...] = entry[SUBLANES]
        h_in = jnp.concatenate(entry[:SUBLANES], axis=0)
        for k in range(seg):
            rows = slice(k * SUBLANES, (k + 1) * SUBLANES)
            u_ref[t, rows, :] = a_ref[t, rows, :] * h_in + u_ref[t, rows, :]

    def rnn_output(t):
        y = y_ref[t]
        k1 = -2.0 * math.sqrt(2.0 / math.pi) * LOG2E
        gelu = y / (1.0 + jnp.exp2(y * (k1 + (k1 * 0.044715) * (y * y))))
        o_slab = (u_ref[t] * gelu).astype(BF16)
        orn_ref[0, t * tr:(t + 1) * tr, :] = jnp.dot(
            permT_ref[...], o_slab, preferred_element_type=F32).astype(BF16)

    def proj_q():
        qT = lax.dot_general(wqT_ref[...], h, _NT, preferred_element_type=F32)
        q3 = qT.reshape(N_HEADS, HEAD_DIM, tm)
        qss = jnp.mean(q3 * q3, axis=1, keepdims=True)
        qn = q3 * lax.rsqrt(qss + EPS) * qw_ref[...]
        qT_ref[0] = qn.reshape(D_ATTN, tm).astype(BF16)

    def proj_v():
        vT = lax.dot_general(wvT_ref[...], h, _NT, preferred_element_type=F32).astype(BF16)
        ones_rows = (lax.broadcasted_iota(jnp.int32, (BF16_ROWS, tm), 0) == 0).astype(BF16)
        for hd in range(N_HEADS):
            vT_ref[0, hd * V_ROWS:hd * V_ROWS + HEAD_DIM, :] = vT[hd * HEAD_DIM:(hd + 1) * HEAD_DIM]
            vT_ref[0, hd * V_ROWS + HEAD_DIM:(hd + 1) * V_ROWS, :] = ones_rows

    def proj_k():
        k = jnp.dot(h, wk_ref[...], preferred_element_type=F32)
        k2 = k * k
        hi = k2.astype(BF16)
        lo = (k2 - hi.astype(F32)).astype(BF16)
        kss = (jnp.dot(hi, g_ref[...], preferred_element_type=F32)
               + jnp.dot(lo, g_ref[...], preferred_element_type=F32))
        kn = k * lax.rsqrt(kss * (1.0 / HEAD_DIM) + EPS) * kw_ref[...]
        k_ref[0] = kn.astype(BF16)
        for bi in range(tm // BLOCK):
            km_ref[0, bi] = jnp.mean(kn[bi * BLOCK:(bi + 1) * BLOCK], axis=0, keepdims=True)

    gate_chunk = 512

    def proj_gate(c):
        cols = slice(c * gate_chunk, (c + 1) * gate_chunk)
        gates_ref[0, :, cols] = jnp.dot(h, wg_ref[:, cols], preferred_element_type=F32).astype(BF16)

    assert tm == 2 * tr and wg_ref.shape[1] == 4 * gate_chunk
    rnn_inputs(0)
    proj_q()
    rnn_conv(0)
    rnn_gate_mm(0)
    proj_v()
    proj_gate(0)
    rnn_inputs(1)
    rnn_nonlin(0)
    rnn_scan(0)
    rnn_output(0)
    proj_k()
    rnn_conv(1)
    rnn_gate_mm(1)
    proj_gate(1)
    proj_gate(2)
    proj_gate(3)
    rnn_nonlin(1)
    rnn_scan(1)
    rnn_output(1)


def _inproj_rnn(x, n1, wqT, wk, wvT, wxy, wg, qw, kw, gmat, conv_w, conv_b, wa_bd, wi_bd, b_a, b_i, lam,
                w_ff1, w_ff2, tm, tr):
    B, S, D = x.shape
    d = conv_w.shape[1]
    d_g = wg.shape[1]
    nb = S // BLOCK
    n_sub = tm // tr
    grid = (B, S // tm)
    seg = tr // SUBLANES
    t = np.arange(tr)
    perm_np = np.zeros((tr, tr), np.float32)
    perm_np[(t % seg) * SUBLANES + t // seg, t] = 1.0
    perm = jnp.asarray(perm_np, BF16)
    permT = jnp.asarray(perm_np.T, BF16)
    consts = (n1, wqT, wk, wvT, wxy, wg, qw, kw, gmat, perm, permT, conv_w, conv_b, wa_bd, wi_bd, b_a, b_i, lam)
    n_s = S // tm
    n_steps = B * n_s
    r1, r2 = w_ff1.shape[0] // n_steps, w_ff2.shape[0] // n_steps
    assert r1 * n_steps == w_ff1.shape[0] and r2 * n_steps == w_ff2.shape[0]
    assert r1 % BF16_ROWS == 0 and r2 % BF16_ROWS == 0
    w1_spec = pl.BlockSpec((r1, w_ff1.shape[1]), lambda b, s: (b * n_s + s, 0))
    w2_spec = pl.BlockSpec((r2, w_ff2.shape[1]), lambda b, s: (b * n_s + s, 0))
    return pl.pallas_call(
        _inproj_rnn_body,
        grid=grid,
        in_specs=([pl.BlockSpec((1, tm, D), lambda b, s: (b, s, 0))] + [_const_spec(c.shape) for c in consts]
                  + [w1_spec, w2_spec]),
        out_specs=[
            pl.BlockSpec((1, D_ATTN, tm), lambda b, s: (b, 0, s)),
            pl.BlockSpec((1, tm, D_ATTN), lambda b, s: (b, s, 0)),
            pl.BlockSpec((1, N_HEADS * V_ROWS, tm), lambda b, s: (b, 0, s)),
            pl.BlockSpec((1, tm // BLOCK, 1, D_ATTN), lambda b, s: (b, s, 0, 0)),
            pl.BlockSpec((1, tm, d_g), lambda b, s: (b, s, 0)),
            pl.BlockSpec((1, tm, d), lambda b, s: (b, s, 0)),
            w1_spec, w2_spec,
        ],
        out_shape=[
            jax.ShapeDtypeStruct((B, D_ATTN, S), BF16),
            jax.ShapeDtypeStruct((B, S, D_ATTN), BF16),
            jax.ShapeDtypeStruct((B, N_HEADS * V_ROWS, S), BF16),
            jax.ShapeDtypeStruct((B, nb, 1, D_ATTN), F32),
            jax.ShapeDtypeStruct((B, S, d_g), BF16),
            jax.ShapeDtypeStruct((B, S, d), BF16),
            jax.ShapeDtypeStruct(w_ff1.shape, BF16),
            jax.ShapeDtypeStruct(w_ff2.shape, BF16),
        ],
        scratch_shapes=[pltpu.VMEM((n_sub, tr, d), F32) for _ in range(5)] + [
            pltpu.VMEM((CONV_WIDTH - 1, SUBLANES, d), F32), pltpu.VMEM((1, d), F32)],
        compiler_params=pltpu.CompilerParams(
            dimension_semantics=("parallel", "arbitrary"), vmem_limit_bytes=VMEM_LIMIT),
        name="inproj_rglru",
    )(x, *consts, w_ff1, w_ff2)


def _attn_body(far_ref, qT_ref, k_ref, vT_ref, km_ref, bd_ref, bp_ref, o_ref, sel_ref, qk_ref, m_ref, acc_ref):
    p = pl.program_id(1)
    i = pl.program_id(2)
    nb = km_ref.shape[1]
    hps = bd_ref.shape[0]
    heads = range(hps)
    groups = range(hps // TILE_HEADS)
    lanes = TILE_HEADS * HEAD_DIM
    row = lax.broadcasted_iota(jnp.int32, (lanes, BLOCK), 0)
    blk = lax.broadcasted_iota(jnp.int32, (nb, BLOCK), 0)
    lane_head = lax.broadcasted_iota(jnp.int32, (nb, lanes), 1) // HEAD_DIM
    blk_f = blk.astype(F32)
    past = blk < i
    qz = []
    for g in groups:
        qall = qT_ref[0, g * lanes:(g + 1) * lanes, :]
        qz += [jnp.where((row >= t * HEAD_DIM) & (row < (t + 1) * HEAD_DIM), qall, jnp.zeros_like(qall))
               for t in range(TILE_HEADS)]

        km = km_ref[0, :, g * lanes:(g + 1) * lanes]
        km_hi = km.astype(BF16)
        km_r = km - km_hi.astype(F32)
        km_mid = km_r.astype(BF16)
        km_lo = (km_r - km_mid.astype(F32)).astype(BF16)
        stacked = jnp.concatenate(
            [jnp.where(lane_head == t, piece, jnp.zeros_like(piece))
             for piece in (km_hi, km_mid, km_lo) for t in range(TILE_HEADS)], axis=0)
        gates = jnp.dot(stacked, qall, preferred_element_type=F32)
        for t in range(TILE_HEADS):
            hh = g * TILE_HEADS + t
            gate = (gates[t * nb:(t + 1) * nb] + gates[(TILE_HEADS + t) * nb:(TILE_HEADS + t + 1) * nb]
                    + gates[(2 * TILE_HEADS + t) * nb:(2 * TILE_HEADS + t + 1) * nb])
            gate = jnp.where(past, gate, -jnp.inf)
            chosen = jnp.zeros((nb, BLOCK), F32)
            for _ in range(TOPK):
                best = jnp.max(gate, axis=0, keepdims=True)
                first = jnp.min(jnp.where(gate == best, blk_f, float(nb)), axis=0, keepdims=True)
                hit = blk_f == first
                gate = jnp.where(hit, -jnp.inf, gate)
                chosen = jnp.where(hit, 1.0, chosen)
            sel_ref[hh] = jnp.where(past & (chosen > 0.5), far_ref[p * hps + hh], NEG)

    def keys(j, g):
        return k_ref[0, pl.ds(pl.multiple_of(j * BLOCK, BLOCK), BLOCK), g * lanes:(g + 1) * lanes]

    def values(j, hh):
        return vT_ref[0, hh * V_ROWS:(hh + 1) * V_ROWS,
                      pl.ds(pl.multiple_of(j * BLOCK, BLOCK), BLOCK)]

    def issue(slot, j):
        for g in groups:
            kb = keys(j, g)
            for hh in range(g * TILE_HEADS, (g + 1) * TILE_HEADS):
                qk_ref[slot, hh] = jnp.dot(kb, qz[hh], preferred_element_type=F32)

    def step(slot, j, bias, row, nxt_slot, nxt_j):
        for g in groups:
            kb = keys(nxt_j, g)
            for hh in range(g * TILE_HEADS, (g + 1) * TILE_HEADS):
                qk_ref[nxt_slot, hh] = jnp.dot(kb, qz[hh], preferred_element_type=F32)
                m_prev = m_ref[hh]
                s = qk_ref[slot, hh] if bias is None else qk_ref[slot, hh] + bias(hh)
                r = row(hh)
                m_new = jnp.maximum(m_prev, jnp.max(s, axis=0, keepdims=True) + r)
                m_ref[hh] = m_new
                alpha = jnp.exp2(m_prev - m_new)
                pb = jnp.exp2((s - (m_new - r)).astype(BF16))
                acc_ref[hh] = alpha * acc_ref[hh] + jnp.dot(values(j, hh), pb, preferred_element_type=F32)

    jp = jnp.maximum(i - 1, 0)
    n_old = jp
    m_ref[...] = jnp.full(m_ref.shape, NEG, F32)
    acc_ref[...] = jnp.zeros(acc_ref.shape, F32)
    zero_row = jnp.zeros((1, BLOCK), F32)
    issue(0, i)
    step(0, i, lambda hh: bd_ref[hh], lambda hh: zero_row, 1, jp)
    step(1, jp, lambda hh: bp_ref[hh],
         lambda hh: jnp.where(sel_ref[hh, pl.ds(jp, 1), :] > 0.5 * NEG, 0.0, NEG), 0, 0)

    def older_pair(t, carry):
        ja = 2 * t
        jb = ja + 1
        step(0, ja, None, lambda hh: sel_ref[hh, pl.ds(ja, 1), :], 1, jb)
        step(1, jb, None, lambda hh: jnp.where(jb < n_old, sel_ref[hh, pl.ds(jb, 1), :], NEG), 0, ja + 2)
        return carry

    lax.fori_loop(0, (n_old + 1) // 2, older_pair, 0)
    outs = [acc_ref[hh, :HEAD_DIM, :] * (1.0 / acc_ref[hh, HEAD_DIM:HEAD_DIM + 1, :]) for hh in heads]
    o_ref[0] = jnp.concatenate(outs, axis=0).T.astype(BF16)


def _attention(qT, k, vT, kmean, bias_d, bias_p, bias_far, hps):
    B, _, S = qT.shape
    nb = S // BLOCK
    lanes = hps * HEAD_DIM
    grid = (B, N_HEADS // hps, nb)
    assert hps % TILE_HEADS == 0
    return pl.pallas_call(
        _attn_body,
        grid=grid,
        in_specs=[
            pl.BlockSpec(memory_space=pltpu.SMEM),
            pl.BlockSpec((1, lanes, BLOCK), lambda b, p, i: (b, p, i)),
            pl.BlockSpec((1, S, lanes), lambda b, p, i: (b, 0, p)),
            pl.BlockSpec((1, hps * V_ROWS, S), lambda b, p, i: (b, p, 0)),
            pl.BlockSpec((1, nb, lanes), lambda b, p, i: (b, 0, p)),
            pl.BlockSpec((hps, BLOCK, BLOCK), lambda b, p, i: (p, 0, 0)),
            pl.BlockSpec((hps, BLOCK, BLOCK), lambda b, p, i: (p, 0, 0)),
        ],
        out_specs=pl.BlockSpec((1, BLOCK, lanes), lambda b, p, i: (b, i, p)),
        out_shape=jax.ShapeDtypeStruct((B, S, D_ATTN), BF16),
        scratch_shapes=[pltpu.VMEM((hps, nb, BLOCK), F32), pltpu.VMEM((2, hps, BLOCK, BLOCK), F32),
                        pltpu.VMEM((hps, 1, BLOCK), F32), pltpu.VMEM((hps, V_ROWS, BLOCK), F32)],
        compiler_params=pltpu.CompilerParams(
            dimension_semantics=("parallel", "parallel", "arbitrary"), vmem_limit_bytes=VMEM_LIMIT),
        name="moba_attn",
    )(bias_far, qT, k, vT, kmean, bias_d, bias_p)


def _out_body(x_ref, oa_ref, or_ref, ga_ref, gr_ref, bg_ref, pa_ref, pr_ref, wo_ref, n2_ref,
              w1_ref, w2_ref, y_ref):
    a = jnp.dot(oa_ref[...], pa_ref[...], preferred_element_type=F32)
    r = jnp.dot(or_ref[...], pr_ref[...], preferred_element_type=F32)
    g_a = jax.nn.sigmoid(ga_ref[...].astype(F32) + bg_ref[0:1, :])
    g_r = jax.nn.sigmoid(gr_ref[...].astype(F32) + bg_ref[1:2, :])
    merged = (g_a * a + g_r * r).astype(BF16)
    x1 = x_ref[...] + jnp.dot(merged, wo_ref[...], preferred_element_type=F32)
    ms = jnp.mean(x1 * x1, axis=-1, keepdims=True)
    h2 = (x1 * lax.rsqrt(ms + EPS) * n2_ref[...]).astype(BF16)
    d_ff = w1_ref.shape[1]
    cw = 1024
    acc = x1
    for c in range(d_ff // cw):
        t = jnp.dot(h2, w1_ref[:, c * cw:(c + 1) * cw], preferred_element_type=F32)
        t = jnp.maximum(t, 0.0)
        acc = acc + jnp.dot((t * t).astype(BF16), w2_ref[c * cw:(c + 1) * cw, :],
                            preferred_element_type=F32)
    y_ref[...] = acc


def _out(x2, o_attn, o_rnn, gates, bg, pa, pr, wo, n2, w1, w2, tm):
    T, D = x2.shape
    d_rnn = o_rnn.shape[1]
    grid = (T // tm,)
    return pl.pallas_call(
        _out_body,
        grid=grid,
        in_specs=[
            pl.BlockSpec((tm, D), lambda t: (t, 0)),
            pl.BlockSpec((tm, D_ATTN), lambda t: (t, 0)),
            pl.BlockSpec((tm, d_rnn), lambda t: (t, 0)),
            pl.BlockSpec((tm, D), lambda t: (t, 0)),
            pl.BlockSpec((tm, D), lambda t: (t, 1)),
            _const_spec(bg.shape), _const_spec(pa.shape), _const_spec(pr.shape),
            _const_spec(wo.shape), _const_spec(n2.shape), _const_spec(w1.shape),
            _const_spec(w2.shape),
        ],
        out_specs=pl.BlockSpec((tm, D), lambda t: (t, 0)),
        out_shape=jax.ShapeDtypeStruct((T, D), F32),
        compiler_params=pltpu.CompilerParams(
            dimension_semantics=("parallel",), vmem_limit_bytes=VMEM_LIMIT),
        name="merge_mlp",
    )(x2, o_attn, o_rnn, gates, gates, bg, pa, pr, wo, n2, w1, w2)


def _t5_bucket(rel):
    max_exact = NUM_BUCKETS // 2
    n = jnp.maximum(rel, 0)
    nf = jnp.maximum(n, 1).astype(F32)
    large = max_exact + (jnp.log(nf / max_exact) / math.log(MAX_DISTANCE / max_exact)
                         * (NUM_BUCKETS - max_exact)).astype(jnp.int32)
    large = jnp.minimum(large, NUM_BUCKETS - 1)
    return jnp.where(n < max_exact, n, large)


def _bias_tables(rel_bias):
    offs = jnp.arange(BLOCK)
    rel_own = offs[None, :] - offs[:, None]
    buckets = jnp.arange(NUM_BUCKETS)

    def lookup(rel):
        onehot = (_t5_bucket(rel)[..., None] == buckets).astype(F32)
        return jnp.einsum('kqb,bh->hkq', onehot, rel_bias, precision=lax.Precision.HIGHEST)

    bias_d = jnp.where(rel_own >= 0, lookup(rel_own) * LOG2E, NEG)
    bias_p = lookup(rel_own + BLOCK) * LOG2E
    bias_far = rel_bias[NUM_BUCKETS - 1] * LOG2E
    return bias_d.astype(F32), bias_p.astype(F32), bias_far.astype(F32)


def _block_diag(w):
    per = GROUP // RNN_BLOCK_DIM
    w4 = w.reshape(RNN_BLOCKS // per, per, RNN_BLOCK_DIM, RNN_BLOCK_DIM)
    eye = jnp.eye(per, dtype=w.dtype)
    return jnp.einsum('gade,ab->gadbe', w4, eye).reshape(RNN_BLOCKS // per, GROUP, GROUP)


def _layer(x, norm1_w, w_in, b_gate, q_norm_w, k_norm_w, bias_tabs, conv_w, conv_b,
           w_rg_a, b_rg_a, w_rg_i, b_rg_i, lru_lambda, w_proj_attn, w_proj_rnn,
           w_out, norm2_w, w_ff1, w_ff2, tm_in, tr, tm_out):
    B, S, D = x.shape
    d_rnn = conv_w.shape[1]
    c0, c1, c2 = D_ATTN, 2 * D_ATTN, 3 * D_ATTN
    wqT = w_in[:, :c0].T.astype(BF16)
    wk = w_in[:, c0:c1].astype(BF16)
    wvT = w_in[:, c1:c2].T.astype(BF16)
    c3 = c2 + 2 * d_rnn
    wxy = w_in[:, c2:c3].astype(BF16)
    wg = w_in[:, c3:].astype(BF16)
    qw = (q_norm_w * (HEAD_DIM ** -0.5 * LOG2E)).reshape(1, HEAD_DIM, 1)
    kw = jnp.tile(k_norm_w, N_HEADS).reshape(1, D_ATTN)
    head_of = np.arange(D_ATTN) // HEAD_DIM
    gmat = jnp.asarray(head_of[:, None] == head_of[None, :], dtype=BF16)

    qT, k, vT, kmean, gates, o_rnn, w1b, w2b = _inproj_rnn(
        x, norm1_w.reshape(1, D), wqT, wk, wvT, wxy, wg, qw, kw, gmat, conv_w, conv_b.reshape(1, d_rnn),
        _block_diag(w_rg_a).astype(BF16), _block_diag(w_rg_i).astype(BF16),
        b_rg_a.reshape(1, d_rnn), b_rg_i.reshape(1, d_rnn), lru_lambda.reshape(1, d_rnn),
        w_ff1, w_ff2, tm_in, tr)
    o_attn = _attention(qT, k, vT, kmean.reshape(B, S // BLOCK, D_ATTN), *bias_tabs, HEADS_PER_STEP)
    y = _out(x.reshape(B * S, D), o_attn.reshape(B * S, D_ATTN), o_rnn.reshape(B * S, d_rnn),
             gates.reshape(B * S, -1), b_gate, w_proj_attn.astype(BF16), w_proj_rnn.astype(BF16),
             w_out.astype(BF16), norm2_w.reshape(1, D), w1b, w2b, tm_out)
    return y.reshape(B, S, D)


def kernel(x, norm1_w, w_in, b_gate, q_norm_w, k_norm_w, rel_bias, conv_w, conv_b, w_rg_a, b_rg_a,
           w_rg_i, b_rg_i, lru_lambda, w_proj_attn, w_proj_rnn, w_out, norm2_w, w_ff1, w_ff2):
    S = x.shape[1]
    assert S % BLOCK == 0
    bias_tabs = _bias_tables(rel_bias)
    tm_in = min(512, S)
    tr = min(256, S)
    tm_out = min(512, S)
    for l in range(norm1_w.shape[0]):
        x = _layer(x, norm1_w[l], w_in[l], b_gate[l], q_norm_w[l], k_norm_w[l], bias_tabs,
                   conv_w[l], conv_b[l], w_rg_a[l], b_rg_a[l], w_rg_i[l], b_rg_i[l], lru_lambda[l],
                   w_proj_attn[l], w_proj_rnn[l], w_out[l], norm2_w[l], w_ff1[l], w_ff2[l],
                   tm_in, tr, tm_out)
    return x
```

```python
import math

import numpy as np
import jax
import jax.numpy as jnp
from jax import lax
from jax.experimental import pallas as pl
from jax.experimental.pallas import tpu as pltpu

N_HEADS = 8
HEAD_DIM = 64
D_ATTN = N_HEADS * HEAD_DIM
BLOCK = 256
TOPK = 3
NUM_BUCKETS = 32
MAX_DISTANCE = 128
RNN_BLOCKS = 16
RNN_BLOCK_DIM = 64
CONV_WIDTH = 4
LRU_C = 8.0
EPS = 1e-6
NEG = -1e30
LOG2E = math.log2(math.e)
HEADS_PER_STEP = 8
TILE_HEADS = 4
AHEAD = 2
GROUP = 256
SUBLANES = 8
BF16_ROWS = 16
V_ROWS = HEAD_DIM + BF16_ROWS
VMEM_LIMIT = 56 * 1024 * 1024

F32 = jnp.float32
BF16 = jnp.bfloat16
_NT = (((1,), (1,)), ((), ()))


def _const_spec(shape):
    nd = len(shape)
    return pl.BlockSpec(shape, lambda *_: (0,) * nd, pipeline_mode=pl.Buffered(1))


def _inproj_rnn_body(x_ref, n1_ref, wqT_ref, wk_ref, wvT_ref, wxy_ref, wg_ref, qw_ref, kw_ref, g_ref,
                     perm_ref, permT_ref, cw_ref, cb_ref, wa_ref, wi_ref, ba_ref, bi_ref, lam_ref,
                     w1_ref, w2_ref,
                     qT_ref, k_ref, vT_ref, km_ref, gates_ref, orn_ref, w1b_ref, w2b_ref,
                     xs_ref, y_ref, xc_ref, a_ref, u_ref, tail_ref, hc_ref):
    tm = x_ref.shape[1]
    d = cw_ref.shape[1]
    tr = perm_ref.shape[0]
    seg = tr // SUBLANES
    taps = CONV_WIDTH - 1

    @pl.when(pl.program_id(1) == 0)
    def _():
        tail_ref[...] = jnp.zeros_like(tail_ref)
        hc_ref[...] = jnp.zeros_like(hc_ref)

    w1b_ref[...] = w1_ref[...].astype(BF16)
    w2b_ref[...] = w2_ref[...].astype(BF16)

    x = x_ref[0]
    ms = jnp.mean(x * x, axis=-1, keepdims=True)
    h = (x * lax.rsqrt(ms + EPS) * n1_ref[...]).astype(BF16)

    def rnn_inputs(t):
        hp = jnp.dot(perm_ref[...], h[t * tr:(t + 1) * tr], preferred_element_type=F32).astype(BF16)
        xs_ref[t] = jnp.dot(hp, wxy_ref[:, :d], preferred_element_type=F32)
        y_ref[t] = jnp.dot(hp, wxy_ref[:, d:], preferred_element_type=F32)

    def rnn_conv(t):
        def slab(k):
            return xs_ref[t, k * SUBLANES:(k + 1) * SUBLANES, :]

        first_seg = lax.broadcasted_iota(jnp.int32, (SUBLANES, d), 0) == 0
        last = [slab(seg - taps + j) for j in range(taps)]
        window = [jnp.where(first_seg, pltpu.roll(tail_ref[j], 1, 0), pltpu.roll(last[j], 1, 0))
                  for j in range(taps)]
        for j in range(taps):
            tail_ref[j] = last[j]
        cw = [cw_ref[j:j + 1, :] for j in range(CONV_WIDTH)]
        cb = cb_ref[...]
        for k in range(seg):
            cur = slab(k)
            acc = cb + window[0] * cw[0]
            for j in range(1, taps):
                acc = acc + window[j] * cw[j]
            xc_ref[t, k * SUBLANES:(k + 1) * SUBLANES, :] = acc + cur * cw[taps]
            window = window[1:] + [cur]

    def rnn_gate_mm(t):
        xb = xc_ref[t].astype(BF16)
        for g in range(d // GROUP):
            cols = slice(g * GROUP, (g + 1) * GROUP)
            a_ref[t, :, cols] = jnp.dot(xb[:, cols], wa_ref[g], preferred_element_type=F32)
            u_ref[t, :, cols] = jnp.dot(xb[:, cols], wi_ref[g], preferred_element_type=F32)

    def rnn_nonlin(t):
        xc = xc_ref[t]
        t_r = jnp.tanh(0.5 * (a_ref[t] + ba_ref[...]))
        t_i = jnp.tanh(0.5 * (u_ref[t] + bi_ref[...]))
        nlam = -lam_ref[...]
        softplus = jnp.maximum(nlam, 0.0) + jnp.log1p(jnp.exp(-jnp.abs(nlam)))
        half_c = softplus * (-0.5 * LRU_C * LOG2E)
        a = jnp.exp2(t_r * half_c + half_c)
        a_ref[t] = a
        gap = 1.0 - a * a
        half_x = 0.5 * xc
        u_ref[t] = (gap * lax.rsqrt(jnp.maximum(gap, 1e-30))) * (half_x * t_i + half_x)

    def rnn_scan(t):
        hcur = u_ref[t, 0:SUBLANES, :]
        prod = a_ref[t, 0:SUBLANES, :]
        for k in range(1, seg):
            rows = slice(k * SUBLANES, (k + 1) * SUBLANES)
            a_k = a_ref[t, rows, :]
            hcur = a_k * hcur + u_ref[t, rows, :]
            prod = a_k * prod
            u_ref[t, rows, :] = hcur
            a_ref[t, rows, :] = prod
        entry = [hc_ref[...]]
        for s in range(SUBLANES):
            entry.append(prod[s:s + 1, :] * entry[s] + hcur[s:s + 1, :])
        hc_ref[...] = entry[SUBLANES]
        h_in = jnp.concatenate(entry[:SUBLANES], axis=0)
        for k in range(seg):
            rows = slice(k * SUBLANES, (k + 1) * SUBLANES)
            u_ref[t, rows, :] = a_ref[t, rows, :] * h_in + u_ref[t, rows, :]

    def rnn_output(t):
        y = y_ref[t]
        k1 = -2.0 * math.sqrt(2.0 / math.pi) * LOG2E
        gelu = y / (1.0 + jnp.exp2(y * (k1 + (k1 * 0.044715) * (y * y))))
        o_slab = (u_ref[t] * gelu).astype(BF16)
        orn_ref[0, t * tr:(t + 1) * tr, :] = jnp.dot(
            permT_ref[...], o_slab, preferred_element_type=F32).astype(BF16)

    def proj_q():
        qT = lax.dot_general(wqT_ref[...], h, _NT, preferred_element_type=F32)
        q3 = qT.reshape(N_HEADS, HEAD_DIM, tm)
        qss = jnp.mean(q3 * q3, axis=1, keepdims=True)
        qn = q3 * lax.rsqrt(qss + EPS) * qw_ref[...]
        qT_ref[0] = qn.reshape(D_ATTN, tm).astype(BF16)

    def proj_v():
        vT = lax.dot_general(wvT_ref[...], h, _NT, preferred_element_type=F32).astype(BF16)
        ones_rows = (lax.broadcasted_iota(jnp.int32, (BF16_ROWS, tm), 0) == 0).astype(BF16)
        for hd in range(N_HEADS):
            vT_ref[0, hd * V_ROWS:hd * V_ROWS + HEAD_DIM, :] = vT[hd * HEAD_DIM:(hd + 1) * HEAD_DIM]
            vT_ref[0, hd * V_ROWS + HEAD_DIM:(hd + 1) * V_ROWS, :] = ones_rows

    def proj_k():
        k = jnp.dot(h, wk_ref[...], preferred_element_type=F32)
        k2 = k * k
        hi = k2.astype(BF16)
        lo = (k2 - hi.astype(F32)).astype(BF16)
        kss = (jnp.dot(hi, g_ref[...], preferred_element_type=F32)
               + jnp.dot(lo, g_ref[...], preferred_element_type=F32))
        kn = k * lax.rsqrt(kss * (1.0 / HEAD_DIM) + EPS) * kw_ref[...]
        k_ref[0] = kn.astype(BF16)
        for bi in range(tm // BLOCK):
            km_ref[0, bi] = jnp.mean(kn[bi * BLOCK:(bi + 1) * BLOCK], axis=0, keepdims=True)

    gate_chunk = 512

    def proj_gate(c):
        cols = slice(c * gate_chunk, (c + 1) * gate_chunk)
        gates_ref[0, :, cols] = jnp.dot(h, wg_ref[:, cols], preferred_element_type=F32).astype(BF16)

    assert tm == 2 * tr and wg_ref.shape[1] == 4 * gate_chunk
    rnn_inputs(0)
    proj_q()
    rnn_conv(0)
    rnn_gate_mm(0)
    proj_v()
    proj_gate(0)
    rnn_inputs(1)
    rnn_nonlin(0)
    rnn_scan(0)
    rnn_output(0)
    proj_k()
    rnn_conv(1)
    rnn_gate_mm(1)
    proj_gate(1)
    proj_gate(2)
    proj_gate(3)
    rnn_nonlin(1)
    rnn_scan(1)
    rnn_output(1)


def _inproj_rnn(x, n1, wqT, wk, wvT, wxy, wg, qw, kw, gmat, conv_w, conv_b, wa_bd, wi_bd, b_a, b_i, lam,
                w_ff1, w_ff2, tm, tr):
    B, S, D = x.shape
    d = conv_w.shape[1]
    d_g = wg.shape[1]
    nb = S // BLOCK
    n_sub = tm // tr
    grid = (B, S // tm)
    seg = tr // SUBLANES
    t = np.arange(tr)
    perm_np = np.zeros((tr, tr), np.float32)
    perm_np[(t % seg) * SUBLANES + t // seg, t] = 1.0
    perm = jnp.asarray(perm_np, BF16)
    permT = jnp.asarray(perm_np.T, BF16)
    consts = (n1, wqT, wk, wvT, wxy, wg, qw, kw, gmat, perm, permT, conv_w, conv_b, wa_bd, wi_bd, b_a, b_i, lam)
    n_s = S // tm
    n_steps = B * n_s
    r1, r2 = w_ff1.shape[0] // n_steps, w_ff2.shape[0] // n_steps
    assert r1 * n_steps == w_ff1.shape[0] and r2 * n_steps == w_ff2.shape[0]
    assert r1 % BF16_ROWS == 0 and r2 % BF16_ROWS == 0
    w1_spec = pl.BlockSpec((r1, w_ff1.shape[1]), lambda b, s: (b * n_s + s, 0))
    w2_spec = pl.BlockSpec((r2, w_ff2.shape[1]), lambda b, s: (b * n_s + s, 0))
    return pl.pallas_call(
        _inproj_rnn_body,
        grid=grid,
        in_specs=([pl.BlockSpec((1, tm, D), lambda b, s: (b, s, 0))] + [_const_spec(c.shape) for c in consts]
                  + [w1_spec, w2_spec]),
        out_specs=[
            pl.BlockSpec((1, D_ATTN, tm), lambda b, s: (b, 0, s)),
            pl.BlockSpec((1, tm, D_ATTN), lambda b, s: (b, s, 0)),
            pl.BlockSpec((1, N_HEADS * V_ROWS, tm), lambda b, s: (b, 0, s)),
            pl.BlockSpec((1, tm // BLOCK, 1, D_ATTN), lambda b, s: (b, s, 0, 0)),
            pl.BlockSpec((1, tm, d_g), lambda b, s: (b, s, 0)),
            pl.BlockSpec((1, tm, d), lambda b, s: (b, s, 0)),
            w1_spec, w2_spec,
        ],
        out_shape=[
            jax.ShapeDtypeStruct((B, D_ATTN, S), BF16),
            jax.ShapeDtypeStruct((B, S, D_ATTN), BF16),
            jax.ShapeDtypeStruct((B, N_HEADS * V_ROWS, S), BF16),
            jax.ShapeDtypeStruct((B, nb, 1, D_ATTN), F32),
            jax.ShapeDtypeStruct((B, S, d_g), BF16),
            jax.ShapeDtypeStruct((B, S, d), BF16),
            jax.ShapeDtypeStruct(w_ff1.shape, BF16),
            jax.ShapeDtypeStruct(w_ff2.shape, BF16),
        ],
        scratch_shapes=[pltpu.VMEM((n_sub, tr, d), F32) for _ in range(5)] + [
            pltpu.VMEM((CONV_WIDTH - 1, SUBLANES, d), F32), pltpu.VMEM((1, d), F32)],
        compiler_params=pltpu.CompilerParams(
            dimension_semantics=("parallel", "arbitrary"), vmem_limit_bytes=VMEM_LIMIT),
        name="inproj_rglru",
    )(x, *consts, w_ff1, w_ff2)


def _attn_body(far_ref, qT_ref, k_ref, vT_ref, km_ref, bd_ref, bp_ref, o_ref, sel_ref, qk_ref, m_ref, acc_ref):
    p = pl.program_id(1)
    i = pl.program_id(2)
    nb = km_ref.shape[1]
    hps = bd_ref.shape[0]
    heads = range(hps)
    groups = range(hps // TILE_HEADS)
    lanes = TILE_HEADS * HEAD_DIM
    row = lax.broadcasted_iota(jnp.int32, (lanes, BLOCK), 0)
    blk = lax.broadcasted_iota(jnp.int32, (nb, BLOCK), 0)
    lane_head = lax.broadcasted_iota(jnp.int32, (nb, lanes), 1) // HEAD_DIM
    blk_f = blk.astype(F32)
    past = blk < i
    qz = []
    for g in groups:
        qall = qT_ref[0, g * lanes:(g + 1) * lanes, :]
        qz += [jnp.where((row >= t * HEAD_DIM) & (row < (t + 1) * HEAD_DIM), qall, jnp.zeros_like(qall))
               for t in range(TILE_HEADS)]

        km = km_ref[0, :, g * lanes:(g + 1) * lanes]
        km_hi = km.astype(BF16)
        km_r = km - km_hi.astype(F32)
        km_mid = km_r.astype(BF16)
        km_lo = (km_r - km_mid.astype(F32)).astype(BF16)
        stacked = jnp.concatenate(
            [jnp.where(lane_head == t, piece, jnp.zeros_like(piece))
             for piece in (km_hi, km_mid, km_lo) for t in range(TILE_HEADS)], axis=0)
        gates = jnp.dot(stacked, qall, preferred_element_type=F32)
        for t in range(TILE_HEADS):
            hh = g * TILE_HEADS + t
            gate = (gates[t * nb:(t + 1) * nb] + gates[(TILE_HEADS + t) * nb:(TILE_HEADS + t + 1) * nb]
                    + gates[(2 * TILE_HEADS + t) * nb:(2 * TILE_HEADS + t + 1) * nb])
            gate = jnp.where(past, gate, -jnp.inf)
            chosen = jnp.zeros((nb, BLOCK), F32)
            for _ in range(TOPK):
                best = jnp.max(gate, axis=0, keepdims=True)
                first = jnp.min(jnp.where(gate == best, blk_f, float(nb)), axis=0, keepdims=True)
                hit = blk_f == first
                gate = jnp.where(hit, -jnp.inf, gate)
                chosen = jnp.where(hit, 1.0, chosen)
            sel_ref[hh] = jnp.where(past & (chosen > 0.5), far_ref[p * hps + hh], NEG)

    def keys(j, g):
        return k_ref[0, pl.ds(pl.multiple_of(j * BLOCK, BLOCK), BLOCK), g * lanes:(g + 1) * lanes]

    def values(j, hh):
        return vT_ref[0, hh * V_ROWS:(hh + 1) * V_ROWS,
                      pl.ds(pl.multiple_of(j * BLOCK, BLOCK), BLOCK)]

    def scores(kb, hh, bias):
        qk = jnp.dot(kb, qz[hh], preferred_element_type=F32)
        return qk if bias is None else qk + bias[hh]

    def step(slot, j, row, nxt_slot, nxt_j, nxt_bias=None):
        kbs = [keys(nxt_j, g) for g in groups]

        def issue(hh):
            qk_ref[nxt_slot, hh] = scores(kbs[hh // TILE_HEADS], hh, nxt_bias)

        for hh in range(AHEAD):
            issue(hh)
        for hh in heads:
            if hh + AHEAD < hps:
                issue(hh + AHEAD)
            m_prev = m_ref[hh]
            s = qk_ref[slot, hh]
            r = row(hh)
            m_new = jnp.maximum(m_prev, jnp.max(s, axis=0, keepdims=True) + r)
            m_ref[hh] = m_new
            alpha = jnp.exp2(m_prev - m_new)
            pb = jnp.exp2((s - (m_new - r)).astype(BF16))
            acc_ref[hh] = alpha * acc_ref[hh] + jnp.dot(values(j, hh), pb, preferred_element_type=F32)

    jp = jnp.maximum(i - 1, 0)
    n_old = jp
    m_ref[...] = jnp.full(m_ref.shape, NEG, F32)
    acc_ref[...] = jnp.zeros(acc_ref.shape, F32)
    zero_row = jnp.zeros((1, BLOCK), F32)
    for g in groups:
        kb = keys(i, g)
        for hh in range(g * TILE_HEADS, (g + 1) * TILE_HEADS):
            qk_ref[0, hh] = scores(kb, hh, bd_ref)
    step(0, i, lambda hh: zero_row, 1, jp, bp_ref)
    step(1, jp, lambda hh: jnp.where(sel_ref[hh, pl.ds(jp, 1), :] > 0.5 * NEG, 0.0, NEG), 0, 0)

    def older_pair(t, carry):
        ja = 2 * t
        jb = ja + 1
        step(0, ja, lambda hh: sel_ref[hh, pl.ds(ja, 1), :], 1, jb)
        step(1, jb, lambda hh: jnp.where(jb < n_old, sel_ref[hh, pl.ds(jb, 1), :], NEG), 0, ja + 2)
        return carry

    lax.fori_loop(0, (n_old + 1) // 2, older_pair, 0)
    outs = [acc_ref[hh, :HEAD_DIM, :] * (1.0 / acc_ref[hh, HEAD_DIM:HEAD_DIM + 1, :]) for hh in heads]
    o_ref[0] = jnp.concatenate(outs, axis=0).T.astype(BF16)


def _attention(qT, k, vT, kmean, bias_d, bias_p, bias_far, hps):
    B, _, S = qT.shape
    nb = S // BLOCK
    lanes = hps * HEAD_DIM
    grid = (B, N_HEADS // hps, nb)
    assert hps % TILE_HEADS == 0
    return pl.pallas_call(
        _attn_body,
        grid=grid,
        in_specs=[
            pl.BlockSpec(memory_space=pltpu.SMEM),
            pl.BlockSpec((1, lanes, BLOCK), lambda b, p, i: (b, p, i)),
            pl.BlockSpec((1, S, lanes), lambda b, p, i: (b, 0, p)),
            pl.BlockSpec((1, hps * V_ROWS, S), lambda b, p, i: (b, p, 0)),
            pl.BlockSpec((1, nb, lanes), lambda b, p, i: (b, 0, p)),
            pl.BlockSpec((hps, BLOCK, BLOCK), lambda b, p, i: (p, 0, 0)),
            pl.BlockSpec((hps, BLOCK, BLOCK), lambda b, p, i: (p, 0, 0)),
        ],
        out_specs=pl.BlockSpec((1, BLOCK, lanes), lambda b, p, i: (b, i, p)),
        out_shape=jax.ShapeDtypeStruct((B, S, D_ATTN), BF16),
        scratch_shapes=[pltpu.VMEM((hps, nb, BLOCK), F32), pltpu.VMEM((2, hps, BLOCK, BLOCK), F32),
                        pltpu.VMEM((hps, 1, BLOCK), F32), pltpu.VMEM((hps, V_ROWS, BLOCK), F32)],
        compiler_params=pltpu.CompilerParams(
            dimension_semantics=("parallel", "parallel", "arbitrary"), vmem_limit_bytes=VMEM_LIMIT),
        name="moba_attn",
    )(bias_far, qT, k, vT, kmean, bias_d, bias_p)


def _out_body(x_ref, oa_ref, or_ref, ga_ref, gr_ref, bg_ref, pa_ref, pr_ref, wo_ref, n2_ref,
              w1_ref, w2_ref, y_ref):
    a = jnp.dot(oa_ref[...], pa_ref[...], preferred_element_type=F32)
    r = jnp.dot(or_ref[...], pr_ref[...], preferred_element_type=F32)
    g_a = jax.nn.sigmoid(ga_ref[...].astype(F32) + bg_ref[0:1, :])
    g_r = jax.nn.sigmoid(gr_ref[...].astype(F32) + bg_ref[1:2, :])
    merged = (g_a * a + g_r * r).astype(BF16)
    x1 = x_ref[...] + jnp.dot(merged, wo_ref[...], preferred_element_type=F32)
    ms = jnp.mean(x1 * x1, axis=-1, keepdims=True)
    h2 = (x1 * lax.rsqrt(ms + EPS) * n2_ref[...]).astype(BF16)
    d_ff = w1_ref.shape[1]
    cw = 1024
    acc = x1
    for c in range(d_ff // cw):
        t = jnp.dot(h2, w1_ref[:, c * cw:(c + 1) * cw], preferred_element_type=F32)
        t = jnp.maximum(t, 0.0)
        acc = acc + jnp.dot((t * t).astype(BF16), w2_ref[c * cw:(c + 1) * cw, :],
                            preferred_element_type=F32)
    y_ref[...] = acc


def _out(x2, o_attn, o_rnn, gates, bg, pa, pr, wo, n2, w1, w2, tm):
    T, D = x2.shape
    d_rnn = o_rnn.shape[1]
    grid = (T // tm,)
    return pl.pallas_call(
        _out_body,
        grid=grid,
        in_specs=[
            pl.BlockSpec((tm, D), lambda t: (t, 0)),
            pl.BlockSpec((tm, D_ATTN), lambda t: (t, 0)),
            pl.BlockSpec((tm, d_rnn), lambda t: (t, 0)),
            pl.BlockSpec((tm, D), lambda t: (t, 0)),
            pl.BlockSpec((tm, D), lambda t: (t, 1)),
            _const_spec(bg.shape), _const_spec(pa.shape), _const_spec(pr.shape),
            _const_spec(wo.shape), _const_spec(n2.shape), _const_spec(w1.shape),
            _const_spec(w2.shape),
        ],
        out_specs=pl.BlockSpec((tm, D), lambda t: (t, 0)),
        out_shape=jax.ShapeDtypeStruct((T, D), F32),
        compiler_params=pltpu.CompilerParams(
            dimension_semantics=("parallel",), vmem_limit_bytes=VMEM_LIMIT),
        name="merge_mlp",
    )(x2, o_attn, o_rnn, gates, gates, bg, pa, pr, wo, n2, w1, w2)


def _t5_bucket(rel):
    max_exact = NUM_BUCKETS // 2
    n = jnp.maximum(rel, 0)
    nf = jnp.maximum(n, 1).astype(F32)
    large = max_exact + (jnp.log(nf / max_exact) / math.log(MAX_DISTANCE / max_exact)
                         * (NUM_BUCKETS - max_exact)).astype(jnp.int32)
    large = jnp.minimum(large, NUM_BUCKETS - 1)
    return jnp.where(n < max_exact, n, large)


def _bias_tables(rel_bias):
    offs = jnp.arange(BLOCK)
    rel_own = offs[None, :] - offs[:, None]
    buckets = jnp.arange(NUM_BUCKETS)

    def lookup(rel):
        onehot = (_t5_bucket(rel)[..., None] == buckets).astype(F32)
        return jnp.einsum('kqb,bh->hkq', onehot, rel_bias, precision=lax.Precision.HIGHEST)

    bias_d = jnp.where(rel_own >= 0, lookup(rel_own) * LOG2E, NEG)
    bias_p = lookup(rel_own + BLOCK) * LOG2E
    bias_far = rel_bias[NUM_BUCKETS - 1] * LOG2E
    return bias_d.astype(F32), bias_p.astype(F32), bias_far.astype(F32)


def _block_diag(w):
    per = GROUP // RNN_BLOCK_DIM
    w4 = w.reshape(RNN_BLOCKS // per, per, RNN_BLOCK_DIM, RNN_BLOCK_DIM)
    eye = jnp.eye(per, dtype=w.dtype)
    return jnp.einsum('gade,ab->gadbe', w4, eye).reshape(RNN_BLOCKS // per, GROUP, GROUP)


def _layer(x, norm1_w, w_in, b_gate, q_norm_w, k_norm_w, bias_tabs, conv_w, conv_b,
           w_rg_a, b_rg_a, w_rg_i, b_rg_i, lru_lambda, w_proj_attn, w_proj_rnn,
           w_out, norm2_w, w_ff1, w_ff2, tm_in, tr, tm_out):
    B, S, D = x.shape
    d_rnn = conv_w.shape[1]
    c0, c1, c2 = D_ATTN, 2 * D_ATTN, 3 * D_ATTN
    wqT = w_in[:, :c0].T.astype(BF16)
    wk = w_in[:, c0:c1].astype(BF16)
    wvT = w_in[:, c1:c2].T.astype(BF16)
    c3 = c2 + 2 * d_rnn
    wxy = w_in[:, c2:c3].astype(BF16)
    wg = w_in[:, c3:].astype(BF16)
    qw = (q_norm_w * (HEAD_DIM ** -0.5 * LOG2E)).reshape(1, HEAD_DIM, 1)
    kw = jnp.tile(k_norm_w, N_HEADS).reshape(1, D_ATTN)
    head_of = np.arange(D_ATTN) // HEAD_DIM
    gmat = jnp.asarray(head_of[:, None] == head_of[None, :], dtype=BF16)

    qT, k, vT, kmean, gates, o_rnn, w1b, w2b = _inproj_rnn(
        x, norm1_w.reshape(1, D), wqT, wk, wvT, wxy, wg, qw, kw, gmat, conv_w, conv_b.reshape(1, d_rnn),
        _block_diag(w_rg_a).astype(BF16), _block_diag(w_rg_i).astype(BF16),
        b_rg_a.reshape(1, d_rnn), b_rg_i.reshape(1, d_rnn), lru_lambda.reshape(1, d_rnn),
        w_ff1, w_ff2, tm_in, tr)
    o_attn = _attention(qT, k, vT, kmean.reshape(B, S // BLOCK, D_ATTN), *bias_tabs, HEADS_PER_STEP)
    y = _out(x.reshape(B * S, D), o_attn.reshape(B * S, D_ATTN), o_rnn.reshape(B * S, d_rnn),
             gates.reshape(B * S, -1), b_gate, w_proj_attn.astype(BF16), w_proj_rnn.astype(BF16),
             w_out.astype(BF16), norm2_w.reshape(1, D), w1b, w2b, tm_out)
    return y.reshape(B, S, D)


def kernel(x, norm1_w, w_in, b_gate, q_norm_w, k_norm_w, rel_bias, conv_w, conv_b, w_rg_a, b_rg_a,
           w_rg_i, b_rg_i, lru_lambda, w_proj_attn, w_proj_rnn, w_out, norm2_w, w_ff1, w_ff2):
    S = x.shape[1]
    assert S % BLOCK == 0
    bias_tabs = _bias_tables(rel_bias)
    tm_in = min(512, S)
    tr = min(256, S)
    tm_out = min(512, S)
    for l in range(norm1_w.shape[0]):
        x = _layer(x, norm1_w[l], w_in[l], b_gate[l], q_norm_w[l], k_norm_w[l], bias_tabs,
                   conv_w[l], conv_b[l], w_rg_a[l], b_rg_a[l], w_rg_i[l], b_rg_i[l], lru_lambda[l],
                   w_proj_attn[l], w_proj_rnn[l], w_out[l], norm2_w[l], w_ff1[l], w_ff2[l],
                   tm_in, tr, tm_out)
    return x
```

```python
import math

import numpy as np
import jax
import jax.numpy as jnp
from jax import lax
from jax.experimental import pallas as pl
from jax.experimental.pallas import tpu as pltpu

N_HEADS = 8
HEAD_DIM = 64
D_ATTN = N_HEADS * HEAD_DIM
BLOCK = 256
TOPK = 3
NUM_BUCKETS = 32
MAX_DISTANCE = 128
RNN_BLOCKS = 16
RNN_BLOCK_DIM = 64
CONV_WIDTH = 4
LRU_C = 8.0
EPS = 1e-6
NEG = -1e30
LOG2E = math.log2(math.e)
HEADS_PER_STEP = 8
TILE_HEADS = 4
AHEAD = 2
Q_BLOCKS_PER_STEP = 4
GROUP = 256
SUBLANES = 8
BF16_ROWS = 16
V_ROWS = HEAD_DIM + BF16_ROWS
VMEM_LIMIT = 56 * 1024 * 1024

F32 = jnp.float32
BF16 = jnp.bfloat16
_NT = (((1,), (1,)), ((), ()))


def _const_spec(shape):
    nd = len(shape)
    return pl.BlockSpec(shape, lambda *_: (0,) * nd, pipeline_mode=pl.Buffered(1))


def _inproj_rnn_body(x_ref, n1_ref, wqT_ref, wk_ref, wvT_ref, wxy_ref, wg_ref, qw_ref, kw_ref, g_ref,
                     perm_ref, permT_ref, cw_ref, cb_ref, wa_ref, wi_ref, ba_ref, bi_ref, lam_ref,
                     w1_ref, w2_ref,
                     qT_ref, k_ref, vT_ref, km_ref, gates_ref, orn_ref, w1b_ref, w2b_ref,
                     xs_ref, y_ref, xc_ref, a_ref, u_ref, tail_ref, hc_ref):
    tm = x_ref.shape[1]
    d = cw_ref.shape[1]
    tr = perm_ref.shape[0]
    seg = tr // SUBLANES
    taps = CONV_WIDTH - 1

    @pl.when(pl.program_id(1) == 0)
    def _():
        tail_ref[...] = jnp.zeros_like(tail_ref)
        hc_ref[...] = jnp.zeros_like(hc_ref)

    w1b_ref[...] = w1_ref[...].astype(BF16)
    w2b_ref[...] = w2_ref[...].astype(BF16)

    x = x_ref[0]
    ms = jnp.mean(x * x, axis=-1, keepdims=True)
    h = (x * lax.rsqrt(ms + EPS) * n1_ref[...]).astype(BF16)

    def rnn_inputs(t):
        hp = jnp.dot(perm_ref[...], h[t * tr:(t + 1) * tr], preferred_element_type=F32).astype(BF16)
        xs_ref[t] = jnp.dot(hp, wxy_ref[:, :d], preferred_element_type=F32)
        y_ref[t] = jnp.dot(hp, wxy_ref[:, d:], preferred_element_type=F32)

    def rnn_conv(t):
        def slab(k):
            return xs_ref[t, k * SUBLANES:(k + 1) * SUBLANES, :]

        first_seg = lax.broadcasted_iota(jnp.int32, (SUBLANES, d), 0) == 0
        last = [slab(seg - taps + j) for j in range(taps)]
        window = [jnp.where(first_seg, pltpu.roll(tail_ref[j], 1, 0), pltpu.roll(last[j], 1, 0))
                  for j in range(taps)]
        for j in range(taps):
            tail_ref[j] = last[j]
        cw = [0.5 * cw_ref[j:j + 1, :] for j in range(CONV_WIDTH)]
        cb = 0.5 * cb_ref[...]
        for k in range(seg):
            cur = slab(k)
            acc = cb + window[0] * cw[0]
            for j in range(1, taps):
                acc = acc + window[j] * cw[j]
            xc_ref[t, k * SUBLANES:(k + 1) * SUBLANES, :] = acc + cur * cw[taps]
            window = window[1:] + [cur]

    def rnn_gate_mm(t):
        xb = xc_ref[t].astype(BF16)
        for g in range(d // GROUP):
            cols = slice(g * GROUP, (g + 1) * GROUP)
            a_ref[t, :, cols] = jnp.dot(xb[:, cols], wa_ref[g], preferred_element_type=F32)
            u_ref[t, :, cols] = jnp.dot(xb[:, cols], wi_ref[g], preferred_element_type=F32)

    def rnn_nonlin(t):
        half_x = xc_ref[t]
        t_r = jnp.tanh(a_ref[t] + 0.5 * ba_ref[...])
        t_i = jnp.tanh(u_ref[t] + 0.5 * bi_ref[...])
        nlam = -lam_ref[...]
        softplus = jnp.maximum(nlam, 0.0) + jnp.log1p(jnp.exp(-jnp.abs(nlam)))
        half_c = softplus * (-0.5 * LRU_C * LOG2E)
        a = jnp.exp2(t_r * half_c + half_c)
        a_ref[t] = a
        gap = 1.0 - a * a
        u_ref[t] = (gap * lax.rsqrt(jnp.maximum(gap, 1e-30))) * (half_x * t_i + half_x)

    def rnn_scan(t):
        hcur = u_ref[t, 0:SUBLANES, :]
        prod = a_ref[t, 0:SUBLANES, :]
        for k in range(1, seg):
            rows = slice(k * SUBLANES, (k + 1) * SUBLANES)
            a_k = a_ref[t, rows, :]
            hcur = a_k * hcur + u_ref[t, rows, :]
            prod = a_k * prod
            u_ref[t, rows, :] = hcur
            a_ref[t, rows, :] = prod
        entry = [hc_ref[...]]
        for s in range(SUBLANES):
            entry.append(prod[s:s + 1, :] * entry[s] + hcur[s:s + 1, :])
        hc_ref[...] = entry[SUBLANES]
        h_in = jnp.concatenate(entry[:SUBLANES], axis=0)
        for k in range(seg):
            rows = slice(k * SUBLANES, (k + 1) * SUBLANES)
            u_ref[t, rows, :] = a_ref[t, rows, :] * h_in + u_ref[t, rows, :]

    def rnn_output(t):
        y = y_ref[t]
        k1 = -2.0 * math.sqrt(2.0 / math.pi) * LOG2E
        gelu = y / (1.0 + jnp.exp2(y * (k1 + (k1 * 0.044715) * (y * y))))
        o_slab = (u_ref[t] * gelu).astype(BF16)
        orn_ref[0, t * tr:(t + 1) * tr, :] = jnp.dot(
            permT_ref[...], o_slab, preferred_element_type=F32).astype(BF16)

    def proj_q():
        qT = lax.dot_general(wqT_ref[...], h, _NT, preferred_element_type=F32)
        q3 = qT.reshape(N_HEADS, HEAD_DIM, tm)
        qss = jnp.mean(q3 * q3, axis=1, keepdims=True)
        qn = q3 * lax.rsqrt(qss + EPS) * qw_ref[...]
        qT_ref[0] = qn.reshape(D_ATTN, tm).astype(BF16)

    def proj_v():
        vT = lax.dot_general(wvT_ref[...], h, _NT, preferred_element_type=F32).astype(BF16)
        ones_rows = (lax.broadcasted_iota(jnp.int32, (BF16_ROWS, tm), 0) == 0).astype(BF16)
        for hd in range(N_HEADS):
            vT_ref[0, hd * V_ROWS:hd * V_ROWS + HEAD_DIM, :] = vT[hd * HEAD_DIM:(hd + 1) * HEAD_DIM]
            vT_ref[0, hd * V_ROWS + HEAD_DIM:(hd + 1) * V_ROWS, :] = ones_rows

    def proj_k():
        k = jnp.dot(h, wk_ref[...], preferred_element_type=F32)
        k2 = k * k
        hi = k2.astype(BF16)
        lo = (k2 - hi.astype(F32)).astype(BF16)
        kss = (jnp.dot(hi, g_ref[...], preferred_element_type=F32)
               + jnp.dot(lo, g_ref[...], preferred_element_type=F32))
        kn = k * lax.rsqrt(kss * (1.0 / HEAD_DIM) + EPS) * kw_ref[...]
        k_ref[0] = kn.astype(BF16)
        for bi in range(tm // BLOCK):
            km_ref[0, bi] = jnp.mean(kn[bi * BLOCK:(bi + 1) * BLOCK], axis=0, keepdims=True)

    gate_chunk = 512

    def proj_gate(c):
        cols = slice(c * gate_chunk, (c + 1) * gate_chunk)
        gates_ref[0, :, cols] = jnp.dot(h, wg_ref[:, cols], preferred_element_type=F32).astype(BF16)

    assert tm == 2 * tr and wg_ref.shape[1] == 4 * gate_chunk
    rnn_inputs(0)
    proj_q()
    rnn_conv(0)
    rnn_gate_mm(0)
    proj_v()
    proj_gate(0)
    rnn_inputs(1)
    rnn_nonlin(0)
    rnn_scan(0)
    rnn_output(0)
    proj_k()
    rnn_conv(1)
    rnn_gate_mm(1)
    proj_gate(1)
    proj_gate(2)
    proj_gate(3)
    rnn_nonlin(1)
    rnn_scan(1)
    rnn_output(1)


def _inproj_rnn(x, n1, wqT, wk, wvT, wxy, wg, qw, kw, gmat, conv_w, conv_b, wa_bd, wi_bd, b_a, b_i, lam,
                w_ff1, w_ff2, tm, tr):
    B, S, D = x.shape
    d = conv_w.shape[1]
    d_g = wg.shape[1]
    nb = S // BLOCK
    n_sub = tm // tr
    grid = (B, S // tm)
    seg = tr // SUBLANES
    t = np.arange(tr)
    perm_np = np.zeros((tr, tr), np.float32)
    perm_np[(t % seg) * SUBLANES + t // seg, t] = 1.0
    perm = jnp.asarray(perm_np, BF16)
    permT = jnp.asarray(perm_np.T, BF16)
    consts = (n1, wqT, wk, wvT, wxy, wg, qw, kw, gmat, perm, permT, conv_w, conv_b, wa_bd, wi_bd, b_a, b_i, lam)
    n_s = S // tm
    n_steps = B * n_s
    r1, r2 = w_ff1.shape[0] // n_steps, w_ff2.shape[0] // n_steps
    assert r1 * n_steps == w_ff1.shape[0] and r2 * n_steps == w_ff2.shape[0]
    assert r1 % BF16_ROWS == 0 and r2 % BF16_ROWS == 0
    w1_spec = pl.BlockSpec((r1, w_ff1.shape[1]), lambda b, s: (b * n_s + s, 0))
    w2_spec = pl.BlockSpec((r2, w_ff2.shape[1]), lambda b, s: (b * n_s + s, 0))
    return pl.pallas_call(
        _inproj_rnn_body,
        grid=grid,
        in_specs=([pl.BlockSpec((1, tm, D), lambda b, s: (b, s, 0))] + [_const_spec(c.shape) for c in consts]
                  + [w1_spec, w2_spec]),
        out_specs=[
            pl.BlockSpec((1, D_ATTN, tm), lambda b, s: (b, 0, s)),
            pl.BlockSpec((1, tm, D_ATTN), lambda b, s: (b, s, 0)),
            pl.BlockSpec((1, N_HEADS * V_ROWS, tm), lambda b, s: (b, 0, s)),
            pl.BlockSpec((1, tm // BLOCK, 1, D_ATTN), lambda b, s: (b, s, 0, 0)),
            pl.BlockSpec((1, tm, d_g), lambda b, s: (b, s, 0)),
            pl.BlockSpec((1, tm, d), lambda b, s: (b, s, 0)),
            w1_spec, w2_spec,
        ],
        out_shape=[
            jax.ShapeDtypeStruct((B, D_ATTN, S), BF16),
            jax.ShapeDtypeStruct((B, S, D_ATTN), BF16),
            jax.ShapeDtypeStruct((B, N_HEADS * V_ROWS, S), BF16),
            jax.ShapeDtypeStruct((B, nb, 1, D_ATTN), F32),
            jax.ShapeDtypeStruct((B, S, d_g), BF16),
            jax.ShapeDtypeStruct((B, S, d), BF16),
            jax.ShapeDtypeStruct(w_ff1.shape, BF16),
            jax.ShapeDtypeStruct(w_ff2.shape, BF16),
        ],
        scratch_shapes=[pltpu.VMEM((n_sub, tr, d), F32) for _ in range(5)] + [
            pltpu.VMEM((CONV_WIDTH - 1, SUBLANES, d), F32), pltpu.VMEM((1, d), F32)],
        compiler_params=pltpu.CompilerParams(
            dimension_semantics=("parallel", "arbitrary"), vmem_limit_bytes=VMEM_LIMIT),
        name="inproj_rglru",
    )(x, *consts, w_ff1, w_ff2)


def _attn_body(*refs):
    for sub in range(Q_BLOCKS_PER_STEP):
        _attn_block(sub, *refs)


def _attn_block(sub, far_ref, qT_ref, k_ref, vT_ref, km_ref, bd_ref, bp_ref, o_ref, sel_ref, qk_ref, m_ref, acc_ref):
    p = pl.program_id(1)
    i = pl.program_id(2) * Q_BLOCKS_PER_STEP + sub
    q_cols = slice(sub * BLOCK, (sub + 1) * BLOCK)
    nb = km_ref.shape[1]
    hps = bd_ref.shape[0]
    heads = range(hps)
    groups = range(hps // TILE_HEADS)
    lanes = TILE_HEADS * HEAD_DIM
    row = lax.broadcasted_iota(jnp.int32, (lanes, BLOCK), 0)
    blk = lax.broadcasted_iota(jnp.int32, (nb, BLOCK), 0)
    lane_head = lax.broadcasted_iota(jnp.int32, (nb, lanes), 1) // HEAD_DIM
    blk_f = blk.astype(F32)
    past = blk < i
    qz = []
    for g in groups:
        qall = qT_ref[0, g * lanes:(g + 1) * lanes, q_cols]
        qz += [jnp.where((row >= t * HEAD_DIM) & (row < (t + 1) * HEAD_DIM), qall, jnp.zeros_like(qall))
               for t in range(TILE_HEADS)]

        km = km_ref[0, :, g * lanes:(g + 1) * lanes]
        km_hi = km.astype(BF16)
        km_r = km - km_hi.astype(F32)
        km_mid = km_r.astype(BF16)
        km_lo = (km_r - km_mid.astype(F32)).astype(BF16)
        stacked = jnp.concatenate(
            [jnp.where(lane_head == t, piece, jnp.zeros_like(piece))
             for piece in (km_hi, km_mid, km_lo) for t in range(TILE_HEADS)], axis=0)
        gates = jnp.dot(stacked, qall, preferred_element_type=F32)
        for t in range(TILE_HEADS):
            hh = g * TILE_HEADS + t
            gate = (gates[t * nb:(t + 1) * nb] + gates[(TILE_HEADS + t) * nb:(TILE_HEADS + t + 1) * nb]
                    + gates[(2 * TILE_HEADS + t) * nb:(2 * TILE_HEADS + t + 1) * nb])
            gate = jnp.where(past, gate, -jnp.inf)
            chosen = jnp.zeros((nb, BLOCK), F32)
            for _ in range(TOPK):
                best = jnp.max(gate, axis=0, keepdims=True)
                first = jnp.min(jnp.where(gate == best, blk_f, float(nb)), axis=0, keepdims=True)
                hit = blk_f == first
                gate = jnp.where(hit, -jnp.inf, gate)
                chosen = jnp.where(hit, 1.0, chosen)
            sel_ref[hh] = jnp.where(past & (chosen > 0.5), far_ref[p * hps + hh], NEG)

    def keys(j, g):
        return k_ref[0, pl.ds(pl.multiple_of(j * BLOCK, BLOCK), BLOCK), g * lanes:(g + 1) * lanes]

    def values(j, hh):
        return vT_ref[0, hh * V_ROWS:(hh + 1) * V_ROWS,
                      pl.ds(pl.multiple_of(j * BLOCK, BLOCK), BLOCK)]

    def scores(kb, hh, bias):
        qk = jnp.dot(kb, qz[hh], preferred_element_type=F32)
        return qk if bias is None else qk + bias[hh]

    def step(slot, j, row, nxt_slot, nxt_j, nxt_bias=None):
        kbs = [keys(nxt_j, g) for g in groups]

        def issue(hh):
            qk_ref[nxt_slot, hh] = scores(kbs[hh // TILE_HEADS], hh, nxt_bias)

        for hh in range(AHEAD):
            issue(hh)
        for hh in heads:
            if hh + AHEAD < hps:
                issue(hh + AHEAD)
            m_prev = m_ref[hh]
            s = qk_ref[slot, hh]
            r = row(hh)
            m_new = jnp.maximum(m_prev, jnp.max(s, axis=0, keepdims=True) + r)
            m_ref[hh] = m_new
            alpha = jnp.exp2(m_prev - m_new)
            pb = jnp.exp2((s - (m_new - r)).astype(BF16))
            acc_ref[hh] = alpha * acc_ref[hh] + jnp.dot(values(j, hh), pb, preferred_element_type=F32)

    jp = jnp.maximum(i - 1, 0)
    n_old = jp
    m_ref[...] = jnp.full(m_ref.shape, NEG, F32)
    acc_ref[...] = jnp.zeros(acc_ref.shape, F32)
    zero_row = jnp.zeros((1, BLOCK), F32)
    for g in groups:
        kb = keys(i, g)
        for hh in range(g * TILE_HEADS, (g + 1) * TILE_HEADS):
            qk_ref[0, hh] = scores(kb, hh, bd_ref)
    step(0, i, lambda hh: zero_row, 1, jp, bp_ref)
    step(1, jp, lambda hh: jnp.where(sel_ref[hh, pl.ds(jp, 1), :] > 0.5 * NEG, 0.0, NEG), 0, 0)

    def older_pair(t, carry):
        ja = 2 * t
        jb = ja + 1
        step(0, ja, lambda hh: sel_ref[hh, pl.ds(ja, 1), :], 1, jb)
        step(1, jb, lambda hh: jnp.where(jb < n_old, sel_ref[hh, pl.ds(jb, 1), :], NEG), 0, ja + 2)
        return carry

    lax.fori_loop(0, (n_old + 1) // 2, older_pair, 0)
    outs = [acc_ref[hh, :HEAD_DIM, :] * (1.0 / acc_ref[hh, HEAD_DIM:HEAD_DIM + 1, :]) for hh in heads]
    o_ref[0, q_cols, :] = jnp.concatenate(outs, axis=0).T.astype(BF16)


def _attention(qT, k, vT, kmean, bias_d, bias_p, bias_far, hps):
    B, _, S = qT.shape
    nb = S // BLOCK
    lanes = hps * HEAD_DIM
    qb = Q_BLOCKS_PER_STEP
    grid = (B, N_HEADS // hps, nb // qb)
    assert hps % TILE_HEADS == 0 and nb % qb == 0
    return pl.pallas_call(
        _attn_body,
        grid=grid,
        in_specs=[
            pl.BlockSpec(memory_space=pltpu.SMEM),
            pl.BlockSpec((1, lanes, qb * BLOCK), lambda b, p, i: (b, p, i)),
            pl.BlockSpec((1, S, lanes), lambda b, p, i: (b, 0, p)),
            pl.BlockSpec((1, hps * V_ROWS, S), lambda b, p, i: (b, p, 0)),
            pl.BlockSpec((1, nb, lanes), lambda b, p, i: (b, 0, p)),
            pl.BlockSpec((hps, BLOCK, BLOCK), lambda b, p, i: (p, 0, 0)),
            pl.BlockSpec((hps, BLOCK, BLOCK), lambda b, p, i: (p, 0, 0)),
        ],
        out_specs=pl.BlockSpec((1, qb * BLOCK, lanes), lambda b, p, i: (b, i, p)),
        out_shape=jax.ShapeDtypeStruct((B, S, D_ATTN), BF16),
        scratch_shapes=[pltpu.VMEM((hps, nb, BLOCK), F32), pltpu.VMEM((2, hps, BLOCK, BLOCK), F32),
                        pltpu.VMEM((hps, 1, BLOCK), F32), pltpu.VMEM((hps, V_ROWS, BLOCK), F32)],
        compiler_params=pltpu.CompilerParams(
            dimension_semantics=("parallel", "parallel", "arbitrary"), vmem_limit_bytes=VMEM_LIMIT),
        name="moba_attn",
    )(bias_far, qT, k, vT, kmean, bias_d, bias_p)


def _out_body(x_ref, oa_ref, or_ref, ga_ref, gr_ref, bg_ref, pa_ref, pr_ref, wo_ref, n2_ref,
              w1_ref, w2_ref, y_ref):
    a = jnp.dot(oa_ref[...], pa_ref[...], preferred_element_type=F32)
    r = jnp.dot(or_ref[...], pr_ref[...], preferred_element_type=F32)
    g_a = jax.nn.sigmoid(ga_ref[...].astype(F32) + bg_ref[0:1, :])
    g_r = jax.nn.sigmoid(gr_ref[...].astype(F32) + bg_ref[1:2, :])
    merged = (g_a * a + g_r * r).astype(BF16)
    x1 = x_ref[...] + jnp.dot(merged, wo_ref[...], preferred_element_type=F32)
    ms = jnp.mean(x1 * x1, axis=-1, keepdims=True)
    h2 = (x1 * lax.rsqrt(ms + EPS) * n2_ref[...]).astype(BF16)
    d_ff = w1_ref.shape[1]
    cw = 1024
    acc = x1
    for c in range(d_ff // cw):
        t = jnp.dot(h2, w1_ref[:, c * cw:(c + 1) * cw], preferred_element_type=F32)
        t = jnp.maximum(t, 0.0)
        acc = acc + jnp.dot((t * t).astype(BF16), w2_ref[c * cw:(c + 1) * cw, :],
                            preferred_element_type=F32)
    y_ref[...] = acc


def _out(x2, o_attn, o_rnn, gates, bg, pa, pr, wo, n2, w1, w2, tm):
    T, D = x2.shape
    d_rnn = o_rnn.shape[1]
    grid = (T // tm,)
    return pl.pallas_call(
        _out_body,
        grid=grid,
        in_specs=[
            pl.BlockSpec((tm, D), lambda t: (t, 0)),
            pl.BlockSpec((tm, D_ATTN), lambda t: (t, 0)),
            pl.BlockSpec((tm, d_rnn), lambda t: (t, 0)),
            pl.BlockSpec((tm, D), lambda t: (t, 0)),
            pl.BlockSpec((tm, D), lambda t: (t, 1)),
            _const_spec(bg.shape), _const_spec(pa.shape), _const_spec(pr.shape),
            _const_spec(wo.shape), _const_spec(n2.shape), _const_spec(w1.shape),
            _const_spec(w2.shape),
        ],
        out_specs=pl.BlockSpec((tm, D), lambda t: (t, 0)),
        out_shape=jax.ShapeDtypeStruct((T, D), F32),
        compiler_params=pltpu.CompilerParams(
            dimension_semantics=("parallel",), vmem_limit_bytes=VMEM_LIMIT),
        name="merge_mlp",
    )(x2, o_attn, o_rnn, gates, gates, bg, pa, pr, wo, n2, w1, w2)


def _t5_bucket(rel):
    max_exact = NUM_BUCKETS // 2
    n = jnp.maximum(rel, 0)
    nf = jnp.maximum(n, 1).astype(F32)
    large = max_exact + (jnp.log(nf / max_exact) / math.log(MAX_DISTANCE / max_exact)
                         * (NUM_BUCKETS - max_exact)).astype(jnp.int32)
    large = jnp.minimum(large, NUM_BUCKETS - 1)
    return jnp.where(n < max_exact, n, large)


def _bias_tables(rel_bias):
    offs = jnp.arange(BLOCK)
    rel_own = offs[None, :] - offs[:, None]
    buckets = jnp.arange(NUM_BUCKETS)

    def lookup(rel):
        onehot = (_t5_bucket(rel)[..., None] == buckets).astype(F32)
        return jnp.einsum('kqb,bh->hkq', onehot, rel_bias, precision=lax.Precision.HIGHEST)

    bias_d = jnp.where(rel_own >= 0, lookup(rel_own) * LOG2E, NEG)
    bias_p = lookup(rel_own + BLOCK) * LOG2E
    bias_far = rel_bias[NUM_BUCKETS - 1] * LOG2E
    return bias_d.astype(F32), bias_p.astype(F32), bias_far.astype(F32)


def _block_diag(w):
    per = GROUP // RNN_BLOCK_DIM
    w4 = w.reshape(RNN_BLOCKS // per, per, RNN_BLOCK_DIM, RNN_BLOCK_DIM)
    eye = jnp.eye(per, dtype=w.dtype)
    return jnp.einsum('gade,ab->gadbe', w4, eye).reshape(RNN_BLOCKS // per, GROUP, GROUP)


def _layer(x, norm1_w, w_in, b_gate, q_norm_w, k_norm_w, bias_tabs, conv_w, conv_b,
           w_rg_a, b_rg_a, w_rg_i, b_rg_i, lru_lambda, w_proj_attn, w_proj_rnn,
           w_out, norm2_w, w_ff1, w_ff2, tm_in, tr, tm_out):
    B, S, D = x.shape
    d_rnn = conv_w.shape[1]
    c0, c1, c2 = D_ATTN, 2 * D_ATTN, 3 * D_ATTN
    wqT = w_in[:, :c0].T.astype(BF16)
    wk = w_in[:, c0:c1].astype(BF16)
    wvT = w_in[:, c1:c2].T.astype(BF16)
    c3 = c2 + 2 * d_rnn
    wxy = w_in[:, c2:c3].astype(BF16)
    wg = w_in[:, c3:].astype(BF16)
    qw = (q_norm_w * (HEAD_DIM ** -0.5 * LOG2E)).reshape(1, HEAD_DIM, 1)
    kw = jnp.tile(k_norm_w, N_HEADS).reshape(1, D_ATTN)
    head_of = np.arange(D_ATTN) // HEAD_DIM
    gmat = jnp.asarray(head_of[:, None] == head_of[None, :], dtype=BF16)

    qT, k, vT, kmean, gates, o_rnn, w1b, w2b = _inproj_rnn(
        x, norm1_w.reshape(1, D), wqT, wk, wvT, wxy, wg, qw, kw, gmat, conv_w, conv_b.reshape(1, d_rnn),
        _block_diag(w_rg_a).astype(BF16), _block_diag(w_rg_i).astype(BF16),
        b_rg_a.reshape(1, d_rnn), b_rg_i.reshape(1, d_rnn), lru_lambda.reshape(1, d_rnn),
        w_ff1, w_ff2, tm_in, tr)
    o_attn = _attention(qT, k, vT, kmean.reshape(B, S // BLOCK, D_ATTN), *bias_tabs, HEADS_PER_STEP)
    y = _out(x.reshape(B * S, D), o_attn.reshape(B * S, D_ATTN), o_rnn.reshape(B * S, d_rnn),
             gates.reshape(B * S, -1), b_gate, w_proj_attn.astype(BF16), w_proj_rnn.astype(BF16),
             w_out.astype(BF16), norm2_w.reshape(1, D), w1b, w2b, tm_out)
    return y.reshape(B, S, D)


def kernel(x, norm1_w, w_in, b_gate, q_norm_w, k_norm_w, rel_bias, conv_w, conv_b, w_rg_a, b_rg_a,
           w_rg_i, b_rg_i, lru_lambda, w_proj_attn, w_proj_rnn, w_out, norm2_w, w_ff1, w_ff2):
    S = x.shape[1]
    assert S % BLOCK == 0
    bias_tabs = _bias_tables(rel_bias)
    tm_in = min(512, S)
    tr = min(256, S)
    tm_out = min(512, S)
    for l in range(norm1_w.shape[0]):
        x = _layer(x, norm1_w[l], w_in[l], b_gate[l], q_norm_w[l], k_norm_w[l], bias_tabs,
                   conv_w[l], conv_b[l], w_rg_a[l], b_rg_a[l], w_rg_i[l], b_rg_i[l], lru_lambda[l],
                   w_proj_attn[l], w_proj_rnn[l], w_out[l], norm2_w[l], w_ff1[l], w_ff2[l],
                   tm_in, tr, tm_out)
    return x
```

```python
import math

import numpy as np
import jax
import jax.numpy as jnp
from jax import lax
from jax.experimental import pallas as pl
from jax.experimental.pallas import tpu as pltpu

N_HEADS = 8
HEAD_DIM = 64
D_ATTN = N_HEADS * HEAD_DIM
BLOCK = 256
TOPK = 3
NUM_BUCKETS = 32
MAX_DISTANCE = 128
RNN_BLOCKS = 16
RNN_BLOCK_DIM = 64
CONV_WIDTH = 4
LRU_C = 8.0
EPS = 1e-6
NEG = -1e30
LOG2E = math.log2(math.e)
HEADS_PER_STEP = 8
TILE_HEADS = 4
AHEAD = 2
Q_BLOCKS_PER_STEP = 4
GROUP = 256
GATE_CHUNK = 512
FFN_CHUNK = 1024
RSQRT_FLOOR = 1e-30
SUBLANES = 8
BF16_ROWS = 16
V_ROWS = HEAD_DIM + BF16_ROWS
VMEM_LIMIT = 56 * 1024 * 1024

F32 = jnp.float32
BF16 = jnp.bfloat16
_NT = (((1,), (1,)), ((), ()))


def _const_spec(shape):
    nd = len(shape)
    return pl.BlockSpec(shape, lambda *_: (0,) * nd, pipeline_mode=pl.Buffered(1))


def _inproj_rnn_body(x_ref, n1_ref, wqT_ref, wk_ref, wvT_ref, wxy_ref, wg_ref, qw_ref, kw_ref, g_ref,
                     perm_ref, permT_ref, cw_ref, cb_ref, wa_ref, wi_ref, ba_ref, bi_ref, lam_ref,
                     w1_ref, w2_ref,
                     qT_ref, k_ref, vT_ref, km_ref, gates_ref, orn_ref, w1b_ref, w2b_ref,
                     xs_ref, y_ref, xc_ref, a_ref, u_ref, tail_ref, hc_ref):
    tm = x_ref.shape[1]
    d = cw_ref.shape[1]
    tr = perm_ref.shape[0]
    seg = tr // SUBLANES
    taps = CONV_WIDTH - 1

    @pl.when(pl.program_id(1) == 0)
    def _():
        tail_ref[...] = jnp.zeros_like(tail_ref)
        hc_ref[...] = jnp.zeros_like(hc_ref)

    w1b_ref[...] = w1_ref[...].astype(BF16)
    w2b_ref[...] = w2_ref[...].astype(BF16)

    x = x_ref[0]
    ms = jnp.mean(x * x, axis=-1, keepdims=True)
    h = (x * lax.rsqrt(ms + EPS) * n1_ref[...]).astype(BF16)

    def rnn_inputs(t):
        hp = jnp.dot(perm_ref[...], h[t * tr:(t + 1) * tr], preferred_element_type=F32).astype(BF16)
        xs_ref[t] = jnp.dot(hp, wxy_ref[:, :d], preferred_element_type=F32)
        y_ref[t] = jnp.dot(hp, wxy_ref[:, d:], preferred_element_type=F32)

    def rnn_conv(t):
        def slab(k):
            return xs_ref[t, k * SUBLANES:(k + 1) * SUBLANES, :]

        first_seg = lax.broadcasted_iota(jnp.int32, (SUBLANES, d), 0) == 0
        last = [slab(seg - taps + j) for j in range(taps)]
        window = [jnp.where(first_seg, pltpu.roll(tail_ref[j], 1, 0), pltpu.roll(last[j], 1, 0))
                  for j in range(taps)]
        for j in range(taps):
            tail_ref[j] = last[j]
        cw = [0.5 * cw_ref[j:j + 1, :] for j in range(CONV_WIDTH)]
        cb = 0.5 * cb_ref[...]
        for k in range(seg):
            cur = slab(k)
            acc = cb + window[0] * cw[0]
            for j in range(1, taps):
                acc = acc + window[j] * cw[j]
            xc_ref[t, k * SUBLANES:(k + 1) * SUBLANES, :] = acc + cur * cw[taps]
            window = window[1:] + [cur]

    def rnn_gate_mm(t):
        xb = xc_ref[t].astype(BF16)
        for g in range(d // GROUP):
            cols = slice(g * GROUP, (g + 1) * GROUP)
            a_ref[t, :, cols] = jnp.dot(xb[:, cols], wa_ref[g], preferred_element_type=F32)
            u_ref[t, :, cols] = jnp.dot(xb[:, cols], wi_ref[g], preferred_element_type=F32)

    def rnn_nonlin(t):
        half_x = xc_ref[t]
        t_r = jnp.tanh(a_ref[t] + 0.5 * ba_ref[...])
        t_i = jnp.tanh(u_ref[t] + 0.5 * bi_ref[...])
        nlam = -lam_ref[...]
        softplus = jnp.maximum(nlam, 0.0) + jnp.log1p(jnp.exp(-jnp.abs(nlam)))
        half_c = softplus * (-0.5 * LRU_C * LOG2E)
        a = jnp.exp2(t_r * half_c + half_c)
        a_ref[t] = a
        gap = 1.0 - a * a
        u_ref[t] = (gap * lax.rsqrt(jnp.maximum(gap, RSQRT_FLOOR))) * (half_x * t_i + half_x)

    def rnn_scan(t):
        hcur = u_ref[t, 0:SUBLANES, :]
        prod = a_ref[t, 0:SUBLANES, :]
        for k in range(1, seg):
            rows = slice(k * SUBLANES, (k + 1) * SUBLANES)
            a_k = a_ref[t, rows, :]
            hcur = a_k * hcur + u_ref[t, rows, :]
            prod = a_k * prod
            u_ref[t, rows, :] = hcur
            a_ref[t, rows, :] = prod
        entry = [hc_ref[...]]
        for s in range(SUBLANES):
            entry.append(prod[s:s + 1, :] * entry[s] + hcur[s:s + 1, :])
        hc_ref[...] = entry[SUBLANES]
        h_in = jnp.concatenate(entry[:SUBLANES], axis=0)
        for k in range(seg):
            rows = slice(k * SUBLANES, (k + 1) * SUBLANES)
            u_ref[t, rows, :] = a_ref[t, rows, :] * h_in + u_ref[t, rows, :]

    def rnn_output(t):
        y = y_ref[t]
        k1 = -2.0 * math.sqrt(2.0 / math.pi) * LOG2E
        gelu = y / (1.0 + jnp.exp2(y * (k1 + (k1 * 0.044715) * (y * y))))
        o_slab = (u_ref[t] * gelu).astype(BF16)
        orn_ref[0, t * tr:(t + 1) * tr, :] = jnp.dot(
            permT_ref[...], o_slab, preferred_element_type=F32).astype(BF16)

    def proj_q():
        qT = lax.dot_general(wqT_ref[...], h, _NT, preferred_element_type=F32)
        q3 = qT.reshape(N_HEADS, HEAD_DIM, tm)
        qss = jnp.mean(q3 * q3, axis=1, keepdims=True)
        qn = q3 * lax.rsqrt(qss + EPS) * qw_ref[...]
        qT_ref[0] = qn.reshape(D_ATTN, tm).astype(BF16)

    def proj_v():
        vT = lax.dot_general(wvT_ref[...], h, _NT, preferred_element_type=F32).astype(BF16)
        ones_rows = (lax.broadcasted_iota(jnp.int32, (BF16_ROWS, tm), 0) == 0).astype(BF16)
        for hd in range(N_HEADS):
            vT_ref[0, hd * V_ROWS:hd * V_ROWS + HEAD_DIM, :] = vT[hd * HEAD_DIM:(hd + 1) * HEAD_DIM]
            vT_ref[0, hd * V_ROWS + HEAD_DIM:(hd + 1) * V_ROWS, :] = ones_rows

    def proj_k():
        k = jnp.dot(h, wk_ref[...], preferred_element_type=F32)
        k2 = k * k
        hi = k2.astype(BF16)
        lo = (k2 - hi.astype(F32)).astype(BF16)
        kss = (jnp.dot(hi, g_ref[...], preferred_element_type=F32)
               + jnp.dot(lo, g_ref[...], preferred_element_type=F32))
        kn = k * lax.rsqrt(kss * (1.0 / HEAD_DIM) + EPS) * kw_ref[...]
        k_ref[0] = kn.astype(BF16)
        for bi in range(tm // BLOCK):
            km_ref[0, bi] = jnp.mean(kn[bi * BLOCK:(bi + 1) * BLOCK], axis=0, keepdims=True)

    def proj_gate(c):
        cols = slice(c * GATE_CHUNK, (c + 1) * GATE_CHUNK)
        gates_ref[0, :, cols] = jnp.dot(h, wg_ref[:, cols], preferred_element_type=F32).astype(BF16)

    assert tm == 2 * tr and wg_ref.shape[1] == 4 * GATE_CHUNK
    proj_gate(0)
    rnn_inputs(0)
    proj_q()
    rnn_conv(0)
    rnn_gate_mm(0)
    proj_v()
    rnn_inputs(1)
    rnn_nonlin(0)
    rnn_scan(0)
    rnn_output(0)
    proj_k()
    rnn_conv(1)
    rnn_gate_mm(1)
    proj_gate(1)
    proj_gate(2)
    proj_gate(3)
    rnn_nonlin(1)
    rnn_scan(1)
    rnn_output(1)


def _inproj_rnn(x, n1, wqT, wk, wvT, wxy, wg, qw, kw, gmat, conv_w, conv_b, wa_bd, wi_bd, b_a, b_i, lam,
                w_ff1, w_ff2, tm, tr):
    B, S, D = x.shape
    d = conv_w.shape[1]
    d_g = wg.shape[1]
    nb = S // BLOCK
    n_sub = tm // tr
    grid = (B, S // tm)
    seg = tr // SUBLANES
    t = np.arange(tr)
    perm_np = np.zeros((tr, tr), np.float32)
    perm_np[(t % seg) * SUBLANES + t // seg, t] = 1.0
    perm = jnp.asarray(perm_np, BF16)
    permT = jnp.asarray(perm_np.T, BF16)
    consts = (n1, wqT, wk, wvT, wxy, wg, qw, kw, gmat, perm, permT, conv_w, conv_b, wa_bd, wi_bd, b_a, b_i, lam)
    n_s = S // tm
    n_steps = B * n_s
    r1, r2 = w_ff1.shape[0] // n_steps, w_ff2.shape[0] // n_steps
    assert r1 * n_steps == w_ff1.shape[0] and r2 * n_steps == w_ff2.shape[0]
    assert r1 % BF16_ROWS == 0 and r2 % BF16_ROWS == 0
    w1_spec = pl.BlockSpec((r1, w_ff1.shape[1]), lambda b, s: (b * n_s + s, 0))
    w2_spec = pl.BlockSpec((r2, w_ff2.shape[1]), lambda b, s: (b * n_s + s, 0))
    return pl.pallas_call(
        _inproj_rnn_body,
        grid=grid,
        in_specs=([pl.BlockSpec((1, tm, D), lambda b, s: (b, s, 0))] + [_const_spec(c.shape) for c in consts]
                  + [w1_spec, w2_spec]),
        out_specs=[
            pl.BlockSpec((1, D_ATTN, tm), lambda b, s: (b, 0, s)),
            pl.BlockSpec((1, tm, D_ATTN), lambda b, s: (b, s, 0)),
            pl.BlockSpec((1, N_HEADS * V_ROWS, tm), lambda b, s: (b, 0, s)),
            pl.BlockSpec((1, tm // BLOCK, 1, D_ATTN), lambda b, s: (b, s, 0, 0)),
            pl.BlockSpec((1, tm, d_g), lambda b, s: (b, s, 0)),
            pl.BlockSpec((1, tm, d), lambda b, s: (b, s, 0)),
            w1_spec, w2_spec,
        ],
        out_shape=[
            jax.ShapeDtypeStruct((B, D_ATTN, S), BF16),
            jax.ShapeDtypeStruct((B, S, D_ATTN), BF16),
            jax.ShapeDtypeStruct((B, N_HEADS * V_ROWS, S), BF16),
            jax.ShapeDtypeStruct((B, nb, 1, D_ATTN), F32),
            jax.ShapeDtypeStruct((B, S, d_g), BF16),
            jax.ShapeDtypeStruct((B, S, d), BF16),
            jax.ShapeDtypeStruct(w_ff1.shape, BF16),
            jax.ShapeDtypeStruct(w_ff2.shape, BF16),
        ],
        scratch_shapes=[pltpu.VMEM((n_sub, tr, d), F32) for _ in range(5)] + [
            pltpu.VMEM((CONV_WIDTH - 1, SUBLANES, d), F32), pltpu.VMEM((1, d), F32)],
        compiler_params=pltpu.CompilerParams(
            dimension_semantics=("parallel", "arbitrary"), vmem_limit_bytes=VMEM_LIMIT),
        name="inproj_rglru",
    )(x, *consts, w_ff1, w_ff2)


def _attn_body(*refs):
    for sub in range(Q_BLOCKS_PER_STEP):
        _attn_block(sub, *refs)


def _attn_block(sub, far_ref, qT_ref, k_ref, vT_ref, km_ref, bd_ref, bp_ref, o_ref, sel_ref, qk_ref, m_ref, acc_ref):
    p = pl.program_id(1)
    i = pl.program_id(2) * Q_BLOCKS_PER_STEP + sub
    q_cols = slice(sub * BLOCK, (sub + 1) * BLOCK)
    nb = km_ref.shape[1]
    hps = bd_ref.shape[0]
    heads = range(hps)
    groups = range(hps // TILE_HEADS)
    lanes = TILE_HEADS * HEAD_DIM
    row = lax.broadcasted_iota(jnp.int32, (lanes, BLOCK), 0)
    blk = lax.broadcasted_iota(jnp.int32, (nb, BLOCK), 0)
    lane_head = lax.broadcasted_iota(jnp.int32, (nb, lanes), 1) // HEAD_DIM
    blk_f = blk.astype(F32)
    past = blk < i
    qz = []
    for g in groups:
        qall = qT_ref[0, g * lanes:(g + 1) * lanes, q_cols]
        qz += [jnp.where((row >= t * HEAD_DIM) & (row < (t + 1) * HEAD_DIM), qall, jnp.zeros_like(qall))
               for t in range(TILE_HEADS)]

        km = km_ref[0, :, g * lanes:(g + 1) * lanes]
        km_hi = km.astype(BF16)
        km_r = km - km_hi.astype(F32)
        km_mid = km_r.astype(BF16)
        km_lo = (km_r - km_mid.astype(F32)).astype(BF16)
        stacked = jnp.concatenate(
            [jnp.where(lane_head == t, piece, jnp.zeros_like(piece))
             for piece in (km_hi, km_mid, km_lo) for t in range(TILE_HEADS)], axis=0)
        gates = jnp.dot(stacked, qall, preferred_element_type=F32)
        for t in range(TILE_HEADS):
            hh = g * TILE_HEADS + t
            gate = (gates[t * nb:(t + 1) * nb] + gates[(TILE_HEADS + t) * nb:(TILE_HEADS + t + 1) * nb]
                    + gates[(2 * TILE_HEADS + t) * nb:(2 * TILE_HEADS + t + 1) * nb])
            gate = jnp.where(past, gate, -jnp.inf)
            chosen = jnp.zeros((nb, BLOCK), F32)
            for _ in range(TOPK):
                best = jnp.max(gate, axis=0, keepdims=True)
                first = jnp.min(jnp.where(gate == best, blk_f, float(nb)), axis=0, keepdims=True)
                hit = blk_f == first
                gate = jnp.where(hit, -jnp.inf, gate)
                chosen = jnp.where(hit, 1.0, chosen)
            sel_ref[hh] = jnp.where(past & (chosen > 0.5), far_ref[p * hps + hh], NEG)

    def keys(j, g):
        return k_ref[0, pl.ds(pl.multiple_of(j * BLOCK, BLOCK), BLOCK), g * lanes:(g + 1) * lanes]

    def values(j, hh):
        return vT_ref[0, hh * V_ROWS:(hh + 1) * V_ROWS,
                      pl.ds(pl.multiple_of(j * BLOCK, BLOCK), BLOCK)]

    def scores(kb, hh, bias):
        qk = jnp.dot(kb, qz[hh], preferred_element_type=F32)
        return qk if bias is None else qk + bias[hh]

    def step(slot, j, row, nxt_slot, nxt_j, nxt_bias=None):
        kbs = [keys(nxt_j, g) for g in groups]

        def issue(hh):
            qk_ref[nxt_slot, hh] = scores(kbs[hh // TILE_HEADS], hh, nxt_bias)

        for hh in range(AHEAD):
            issue(hh)
        for hh in heads:
            if hh + AHEAD < hps:
                issue(hh + AHEAD)
            m_prev = m_ref[hh]
            s = qk_ref[slot, hh]
            r = row(hh)
            m_new = jnp.maximum(m_prev, jnp.max(s, axis=0, keepdims=True) + r)
            m_ref[hh] = m_new
            alpha = jnp.exp2(m_prev - m_new)
            pb = jnp.exp2((s - (m_new - r)).astype(BF16))
            acc_ref[hh] = alpha * acc_ref[hh] + jnp.dot(values(j, hh), pb, preferred_element_type=F32)

    jp = jnp.maximum(i - 1, 0)
    n_old = jp
    m_ref[...] = jnp.full(m_ref.shape, NEG, F32)
    acc_ref[...] = jnp.zeros(acc_ref.shape, F32)
    zero_row = jnp.zeros((1, BLOCK), F32)
    for g in groups:
        kb = keys(i, g)
        for hh in range(g * TILE_HEADS, (g + 1) * TILE_HEADS):
            qk_ref[0, hh] = scores(kb, hh, bd_ref)
    step(0, i, lambda hh: zero_row, 1, jp, bp_ref)
    step(1, jp, lambda hh: jnp.where(sel_ref[hh, pl.ds(jp, 1), :] > 0.5 * NEG, 0.0, NEG), 0, 0)

    def older_pair(t, carry):
        ja = 2 * t
        jb = ja + 1
        step(0, ja, lambda hh: sel_ref[hh, pl.ds(ja, 1), :], 1, jb)
        step(1, jb, lambda hh: jnp.where(jb < n_old, sel_ref[hh, pl.ds(jb, 1), :], NEG), 0, ja + 2)
        return carry

    lax.fori_loop(0, (n_old + 1) // 2, older_pair, 0)
    outs = [acc_ref[hh, :HEAD_DIM, :] * (1.0 / acc_ref[hh, HEAD_DIM:HEAD_DIM + 1, :]) for hh in heads]
    o_ref[0, q_cols, :] = jnp.concatenate(outs, axis=0).T.astype(BF16)


def _attention(qT, k, vT, kmean, bias_d, bias_p, bias_far, hps):
    B, _, S = qT.shape
    nb = S // BLOCK
    lanes = hps * HEAD_DIM
    qb = Q_BLOCKS_PER_STEP
    grid = (B, N_HEADS // hps, nb // qb)
    assert hps % TILE_HEADS == 0 and nb % qb == 0
    return pl.pallas_call(
        _attn_body,
        grid=grid,
        in_specs=[
            pl.BlockSpec(memory_space=pltpu.SMEM),
            pl.BlockSpec((1, lanes, qb * BLOCK), lambda b, p, i: (b, p, i)),
            pl.BlockSpec((1, S, lanes), lambda b, p, i: (b, 0, p)),
            pl.BlockSpec((1, hps * V_ROWS, S), lambda b, p, i: (b, p, 0)),
            pl.BlockSpec((1, nb, lanes), lambda b, p, i: (b, 0, p)),
            pl.BlockSpec((hps, BLOCK, BLOCK), lambda b, p, i: (p, 0, 0)),
            pl.BlockSpec((hps, BLOCK, BLOCK), lambda b, p, i: (p, 0, 0)),
        ],
        out_specs=pl.BlockSpec((1, qb * BLOCK, lanes), lambda b, p, i: (b, i, p)),
        out_shape=jax.ShapeDtypeStruct((B, S, D_ATTN), BF16),
        scratch_shapes=[pltpu.VMEM((hps, nb, BLOCK), F32), pltpu.VMEM((2, hps, BLOCK, BLOCK), F32),
                        pltpu.VMEM((hps, 1, BLOCK), F32), pltpu.VMEM((hps, V_ROWS, BLOCK), F32)],
        compiler_params=pltpu.CompilerParams(
            dimension_semantics=("parallel", "parallel", "arbitrary"), vmem_limit_bytes=VMEM_LIMIT),
        name="moba_attn",
    )(bias_far, qT, k, vT, kmean, bias_d, bias_p)


def _out_body(x_ref, oa_ref, or_ref, ga_ref, gr_ref, bg_ref, pa_ref, pr_ref, wo_ref, n2_ref,
              w1_ref, w2_ref, y_ref):
    a = jnp.dot(oa_ref[...], pa_ref[...], preferred_element_type=F32)
    r = jnp.dot(or_ref[...], pr_ref[...], preferred_element_type=F32)
    g_a = jax.nn.sigmoid(ga_ref[...].astype(F32) + bg_ref[0:1, :])
    g_r = jax.nn.sigmoid(gr_ref[...].astype(F32) + bg_ref[1:2, :])
    merged = (g_a * a + g_r * r).astype(BF16)
    x1 = x_ref[...] + jnp.dot(merged, wo_ref[...], preferred_element_type=F32)
    ms = jnp.mean(x1 * x1, axis=-1, keepdims=True)
    h2 = (x1 * lax.rsqrt(ms + EPS) * n2_ref[...]).astype(BF16)
    d_ff = w1_ref.shape[1]
    acc = x1
    for c in range(d_ff // FFN_CHUNK):
        cols = slice(c * FFN_CHUNK, (c + 1) * FFN_CHUNK)
        t = jnp.dot(h2, w1_ref[:, cols], preferred_element_type=F32)
        t = jnp.maximum(t, 0.0)
        acc = acc + jnp.dot((t * t).astype(BF16), w2_ref[cols, :],
                            preferred_element_type=F32)
    y_ref[...] = acc


def _out(x2, o_attn, o_rnn, gates, bg, pa, pr, wo, n2, w1, w2, tm):
    T, D = x2.shape
    d_rnn = o_rnn.shape[1]
    grid = (T // tm,)
    return pl.pallas_call(
        _out_body,
        grid=grid,
        in_specs=[
            pl.BlockSpec((tm, D), lambda t: (t, 0)),
            pl.BlockSpec((tm, D_ATTN), lambda t: (t, 0)),
            pl.BlockSpec((tm, d_rnn), lambda t: (t, 0)),
            pl.BlockSpec((tm, D), lambda t: (t, 0)),
            pl.BlockSpec((tm, D), lambda t: (t, 1)),
            _const_spec(bg.shape), _const_spec(pa.shape), _const_spec(pr.shape),
            _const_spec(wo.shape), _const_spec(n2.shape), _const_spec(w1.shape),
            _const_spec(w2.shape),
        ],
        out_specs=pl.BlockSpec((tm, D), lambda t: (t, 0)),
        out_shape=jax.ShapeDtypeStruct((T, D), F32),
        compiler_params=pltpu.CompilerParams(
            dimension_semantics=("parallel",), vmem_limit_bytes=VMEM_LIMIT),
        name="merge_mlp",
    )(x2, o_attn, o_rnn, gates, gates, bg, pa, pr, wo, n2, w1, w2)


def _t5_bucket(rel):
    max_exact = NUM_BUCKETS // 2
    n = jnp.maximum(rel, 0)
    nf = jnp.maximum(n, 1).astype(F32)
    large = max_exact + (jnp.log(nf / max_exact) / math.log(MAX_DISTANCE / max_exact)
                         * (NUM_BUCKETS - max_exact)).astype(jnp.int32)
    large = jnp.minimum(large, NUM_BUCKETS - 1)
    return jnp.where(n < max_exact, n, large)


def _bias_tables(rel_bias):
    offs = jnp.arange(BLOCK)
    rel_own = offs[None, :] - offs[:, None]
    buckets = jnp.arange(NUM_BUCKETS)

    def lookup(rel):
        onehot = (_t5_bucket(rel)[..., None] == buckets).astype(F32)
        return jnp.einsum('kqb,bh->hkq', onehot, rel_bias, precision=lax.Precision.HIGHEST)

    bias_d = jnp.where(rel_own >= 0, lookup(rel_own) * LOG2E, NEG)
    bias_p = lookup(rel_own + BLOCK) * LOG2E
    bias_far = rel_bias[NUM_BUCKETS - 1] * LOG2E
    return bias_d.astype(F32), bias_p.astype(F32), bias_far.astype(F32)


def _block_diag(w):
    per = GROUP // RNN_BLOCK_DIM
    w4 = w.reshape(RNN_BLOCKS // per, per, RNN_BLOCK_DIM, RNN_BLOCK_DIM)
    eye = jnp.eye(per, dtype=w.dtype)
    return jnp.einsum('gade,ab->gadbe', w4, eye).reshape(RNN_BLOCKS // per, GROUP, GROUP)


def _layer(x, norm1_w, w_in, b_gate, q_norm_w, k_norm_w, bias_tabs, conv_w, conv_b,
           w_rg_a, b_rg_a, w_rg_i, b_rg_i, lru_lambda, w_proj_attn, w_proj_rnn,
           w_out, norm2_w, w_ff1, w_ff2, tm_in, tr, tm_out):
    B, S, D = x.shape
    d_rnn = conv_w.shape[1]
    c0, c1, c2 = D_ATTN, 2 * D_ATTN, 3 * D_ATTN
    wqT = w_in[:, :c0].T.astype(BF16)
    wk = w_in[:, c0:c1].astype(BF16)
    wvT = w_in[:, c1:c2].T.astype(BF16)
    c3 = c2 + 2 * d_rnn
    wxy = w_in[:, c2:c3].astype(BF16)
    wg = w_in[:, c3:].astype(BF16)
    qw = (q_norm_w * (HEAD_DIM ** -0.5 * LOG2E)).reshape(1, HEAD_DIM, 1)
    kw = jnp.tile(k_norm_w, N_HEADS).reshape(1, D_ATTN)
    head_of = np.arange(D_ATTN) // HEAD_DIM
    gmat = jnp.asarray(head_of[:, None] == head_of[None, :], dtype=BF16)

    qT, k, vT, kmean, gates, o_rnn, w1b, w2b = _inproj_rnn(
        x, norm1_w.reshape(1, D), wqT, wk, wvT, wxy, wg, qw, kw, gmat, conv_w, conv_b.reshape(1, d_rnn),
        _block_diag(w_rg_a).astype(BF16), _block_diag(w_rg_i).astype(BF16),
        b_rg_a.reshape(1, d_rnn), b_rg_i.reshape(1, d_rnn), lru_lambda.reshape(1, d_rnn),
        w_ff1, w_ff2, tm_in, tr)
    o_attn = _attention(qT, k, vT, kmean.reshape(B, S // BLOCK, D_ATTN), *bias_tabs, HEADS_PER_STEP)
    y = _out(x.reshape(B * S, D), o_attn.reshape(B * S, D_ATTN), o_rnn.reshape(B * S, d_rnn),
             gates.reshape(B * S, -1), b_gate, w_proj_attn.astype(BF16), w_proj_rnn.astype(BF16),
             w_out.astype(BF16), norm2_w.reshape(1, D), w1b, w2b, tm_out)
    return y.reshape(B, S, D)


def kernel(x, norm1_w, w_in, b_gate, q_norm_w, k_norm_w, rel_bias, conv_w, conv_b, w_rg_a, b_rg_a,
           w_rg_i, b_rg_i, lru_lambda, w_proj_attn, w_proj_rnn, w_out, norm2_w, w_ff1, w_ff2):
    S = x.shape[1]
    assert S % BLOCK == 0
    bias_tabs = _bias_tables(rel_bias)
    tm_in = min(512, S)
    tr = min(256, S)
    tm_out = min(512, S)
    for l in range(norm1_w.shape[0]):
        x = _layer(x, norm1_w[l], w_in[l], b_gate[l], q_norm_w[l], k_norm_w[l], bias_tabs,
                   conv_w[l], conv_b[l], w_rg_a[l], b_rg_a[l], w_rg_i[l], b_rg_i[l], lru_lambda[l],
                   w_proj_attn[l], w_proj_rnn[l], w_out[l], norm2_w[l], w_ff1[l], w_ff2[l],
                   tm_in, tr, tm_out)
    return x
```

```python
import math

import numpy as np
import jax
import jax.numpy as jnp
from jax import lax
from jax.experimental import pallas as pl
from jax.experimental.pallas import tpu as pltpu

N_HEADS = 8
HEAD_DIM = 64
D_ATTN = N_HEADS * HEAD_DIM
BLOCK = 256
TOPK = 3
NUM_BUCKETS = 32
MAX_DISTANCE = 128
RNN_BLOCKS = 16
RNN_BLOCK_DIM = 64
CONV_WIDTH = 4
LRU_C = 8.0
EPS = 1e-6
NEG = -1e30
LOG2E = math.log2(math.e)
HEADS_PER_STEP = 8
TILE_HEADS = 4
AHEAD = 2
Q_BLOCKS_PER_STEP = 4
GROUP = 256
STAGE_COLS = 512
SUBLANES = 8
BF16_ROWS = 16
V_ROWS = HEAD_DIM + BF16_ROWS
VMEM_LIMIT = 56 * 1024 * 1024

F32 = jnp.float32
BF16 = jnp.bfloat16
_NT = (((1,), (1,)), ((), ()))


def _const_spec(shape):
    nd = len(shape)
    return pl.BlockSpec(shape, lambda *_: (0,) * nd, pipeline_mode=pl.Buffered(1))


def _inproj_rnn_body(x_ref, win_ref, n1_ref, qw_ref, kw_ref, g_ref,
                     perm_ref, permT_ref, cw_ref, cb_ref, wa_ref, wi_ref, ba_ref, bi_ref, lam_ref,
                     w1_ref, w2_ref,
                     qT_ref, k_ref, vT_ref, km_ref, gates_ref, orn_ref, w1b_ref, w2b_ref,
                     xs_ref, y_ref, xc_ref, a_ref, u_ref, tail_ref, hc_ref,
                     wqT_ref, wk_ref, wvT_ref, wxy_ref, wg_ref, stage_ref, stage_sem):
    tm = x_ref.shape[1]
    d = cw_ref.shape[1]
    tr = perm_ref.shape[0]
    seg = tr // SUBLANES
    taps = CONV_WIDTH - 1

    @pl.when(pl.program_id(1) == 0)
    def _():
        tail_ref[...] = jnp.zeros_like(tail_ref)
        hc_ref[...] = jnp.zeros_like(hc_ref)

    @pl.when((pl.program_id(0) == 0) & (pl.program_id(1) == 0))
    def _():
        n_chunks = win_ref.shape[1] // STAGE_COLS
        dests = ([(wqT_ref, None), (wk_ref, None), (wvT_ref, None)]
                 + [(wxy_ref, c) for c in range(wxy_ref.shape[1] // STAGE_COLS)]
                 + [(wg_ref, c) for c in range(wg_ref.shape[1] // STAGE_COLS)])
        assert len(dests) == n_chunks and D_ATTN == STAGE_COLS

        def chunk_copy(c):
            return pltpu.make_async_copy(win_ref.at[:, pl.ds(c * STAGE_COLS, STAGE_COLS)],
                                         stage_ref.at[c % 2], stage_sem.at[c % 2])

        chunk_copy(0).start()
        for c, (dst, col) in enumerate(dests):
            if c + 1 < n_chunks:
                chunk_copy(c + 1).start()
            chunk_copy(c).wait()
            w = stage_ref[c % 2]
            if dst is wqT_ref or dst is wvT_ref:
                dst[...] = w.T.astype(BF16)
            elif col is None:
                dst[...] = w.astype(BF16)
            else:
                dst[:, col * STAGE_COLS:(col + 1) * STAGE_COLS] = w.astype(BF16)

    w1b_ref[...] = w1_ref[...].astype(BF16)
    w2b_ref[...] = w2_ref[...].astype(BF16)

    x = x_ref[0]
    ms = jnp.mean(x * x, axis=-1, keepdims=True)
    h = (x * lax.rsqrt(ms + EPS) * n1_ref[...]).astype(BF16)

    def rnn_inputs(t):
        hp = jnp.dot(perm_ref[...], h[t * tr:(t + 1) * tr], preferred_element_type=F32).astype(BF16)
        xs_ref[t] = jnp.dot(hp, wxy_ref[:, :d], preferred_element_type=F32)
        y_ref[t] = jnp.dot(hp, wxy_ref[:, d:], preferred_element_type=F32)

    def rnn_conv(t):
        def slab(k):
            return xs_ref[t, k * SUBLANES:(k + 1) * SUBLANES, :]

        first_seg = lax.broadcasted_iota(jnp.int32, (SUBLANES, d), 0) == 0
        last = [slab(seg - taps + j) for j in range(taps)]
        window = [jnp.where(first_seg, pltpu.roll(tail_ref[j], 1, 0), pltpu.roll(last[j], 1, 0))
                  for j in range(taps)]
        for j in range(taps):
            tail_ref[j] = last[j]
        cw = [0.5 * cw_ref[j:j + 1, :] for j in range(CONV_WIDTH)]
        cb = 0.5 * cb_ref[...]
        for k in range(seg):
            cur = slab(k)
            acc = cb + window[0] * cw[0]
            for j in range(1, taps):
                acc = acc + window[j] * cw[j]
            xc_ref[t, k * SUBLANES:(k + 1) * SUBLANES, :] = acc + cur * cw[taps]
            window = window[1:] + [cur]

    def rnn_gate_mm(t):
        xb = xc_ref[t].astype(BF16)
        for g in range(d // GROUP):
            cols = slice(g * GROUP, (g + 1) * GROUP)
            a_ref[t, :, cols] = jnp.dot(xb[:, cols], wa_ref[g], preferred_element_type=F32)
            u_ref[t, :, cols] = jnp.dot(xb[:, cols], wi_ref[g], preferred_element_type=F32)

    def rnn_nonlin(t):
        half_x = xc_ref[t]
        t_r = jnp.tanh(a_ref[t] + 0.5 * ba_ref[...])
        t_i = jnp.tanh(u_ref[t] + 0.5 * bi_ref[...])
        nlam = -lam_ref[...]
        softplus = jnp.maximum(nlam, 0.0) + jnp.log1p(jnp.exp(-jnp.abs(nlam)))
        half_c = softplus * (-0.5 * LRU_C * LOG2E)
        a = jnp.exp2(t_r * half_c + half_c)
        a_ref[t] = a
        gap = 1.0 - a * a
        u_ref[t] = (gap * lax.rsqrt(jnp.maximum(gap, 1e-30))) * (half_x * t_i + half_x)

    def rnn_scan(t):
        hcur = u_ref[t, 0:SUBLANES, :]
        prod = a_ref[t, 0:SUBLANES, :]
        for k in range(1, seg):
            rows = slice(k * SUBLANES, (k + 1) * SUBLANES)
            a_k = a_ref[t, rows, :]
            hcur = a_k * hcur + u_ref[t, rows, :]
            prod = a_k * prod
            u_ref[t, rows, :] = hcur
            a_ref[t, rows, :] = prod
        entry = [hc_ref[...]]
        for s in range(SUBLANES):
            entry.append(prod[s:s + 1, :] * entry[s] + hcur[s:s + 1, :])
        hc_ref[...] = entry[SUBLANES]
        h_in = jnp.concatenate(entry[:SUBLANES], axis=0)
        for k in range(seg):
            rows = slice(k * SUBLANES, (k + 1) * SUBLANES)
            u_ref[t, rows, :] = a_ref[t, rows, :] * h_in + u_ref[t, rows, :]

    def rnn_output(t):
        y = y_ref[t]
        k1 = -2.0 * math.sqrt(2.0 / math.pi) * LOG2E
        gelu = y / (1.0 + jnp.exp2(y * (k1 + (k1 * 0.044715) * (y * y))))
        o_slab = (u_ref[t] * gelu).astype(BF16)
        orn_ref[0, t * tr:(t + 1) * tr, :] = jnp.dot(
            permT_ref[...], o_slab, preferred_element_type=F32).astype(BF16)

    def proj_q():
        qT = lax.dot_general(wqT_ref[...], h, _NT, preferred_element_type=F32)
        q3 = qT.reshape(N_HEADS, HEAD_DIM, tm)
        qss = jnp.mean(q3 * q3, axis=1, keepdims=True)
        qn = q3 * lax.rsqrt(qss + EPS) * qw_ref[...]
        qT_ref[0] = qn.reshape(D_ATTN, tm).astype(BF16)

    def proj_v():
        vT = lax.dot_general(wvT_ref[...], h, _NT, preferred_element_type=F32).astype(BF16)
        ones_rows = (lax.broadcasted_iota(jnp.int32, (BF16_ROWS, tm), 0) == 0).astype(BF16)
        for hd in range(N_HEADS):
            vT_ref[0, hd * V_ROWS:hd * V_ROWS + HEAD_DIM, :] = vT[hd * HEAD_DIM:(hd + 1) * HEAD_DIM]
            vT_ref[0, hd * V_ROWS + HEAD_DIM:(hd + 1) * V_ROWS, :] = ones_rows

    def proj_k():
        k = jnp.dot(h, wk_ref[...], preferred_element_type=F32)
        k2 = k * k
        hi = k2.astype(BF16)
        lo = (k2 - hi.astype(F32)).astype(BF16)
        kss = (jnp.dot(hi, g_ref[...], preferred_element_type=F32)
               + jnp.dot(lo, g_ref[...], preferred_element_type=F32))
        kn = k * lax.rsqrt(kss * (1.0 / HEAD_DIM) + EPS) * kw_ref[...]
        k_ref[0] = kn.astype(BF16)
        for bi in range(tm // BLOCK):
            km_ref[0, bi] = jnp.mean(kn[bi * BLOCK:(bi + 1) * BLOCK], axis=0, keepdims=True)

    gate_chunk = 512

    def proj_gate(c):
        cols = slice(c * gate_chunk, (c + 1) * gate_chunk)
        gates_ref[0, :, cols] = jnp.dot(h, wg_ref[:, cols], preferred_element_type=F32).astype(BF16)

    assert tm == 2 * tr and wg_ref.shape[1] == 4 * gate_chunk
    rnn_inputs(0)
    proj_q()
    rnn_conv(0)
    rnn_gate_mm(0)
    proj_v()
    proj_gate(0)
    rnn_inputs(1)
    rnn_nonlin(0)
    rnn_scan(0)
    rnn_output(0)
    proj_k()
    rnn_conv(1)
    rnn_gate_mm(1)
    proj_gate(1)
    proj_gate(2)
    proj_gate(3)
    rnn_nonlin(1)
    rnn_scan(1)
    rnn_output(1)


def _inproj_rnn(x, w_in, n1, qw, kw, gmat, conv_w, conv_b, wa_bd, wi_bd, b_a, b_i, lam,
                w_ff1, w_ff2, tm, tr):
    B, S, D = x.shape
    d = conv_w.shape[1]
    d_g = w_in.shape[1] - 3 * D_ATTN - 2 * d
    nb = S // BLOCK
    n_sub = tm // tr
    grid = (B, S // tm)
    seg = tr // SUBLANES
    t = np.arange(tr)
    perm_np = np.zeros((tr, tr), np.float32)
    perm_np[(t % seg) * SUBLANES + t // seg, t] = 1.0
    perm = jnp.asarray(perm_np, BF16)
    permT = jnp.asarray(perm_np.T, BF16)
    consts = (n1, qw, kw, gmat, perm, permT, conv_w, conv_b, wa_bd, wi_bd, b_a, b_i, lam)
    n_s = S // tm
    n_steps = B * n_s
    r1, r2 = w_ff1.shape[0] // n_steps, w_ff2.shape[0] // n_steps
    assert r1 * n_steps == w_ff1.shape[0] and r2 * n_steps == w_ff2.shape[0]
    assert r1 % BF16_ROWS == 0 and r2 % BF16_ROWS == 0
    w1_spec = pl.BlockSpec((r1, w_ff1.shape[1]), lambda b, s: (b * n_s + s, 0))
    w2_spec = pl.BlockSpec((r2, w_ff2.shape[1]), lambda b, s: (b * n_s + s, 0))
    return pl.pallas_call(
        _inproj_rnn_body,
        grid=grid,
        in_specs=([pl.BlockSpec((1, tm, D), lambda b, s: (b, s, 0)), pl.BlockSpec(memory_space=pl.ANY)]
                  + [_const_spec(c.shape) for c in consts] + [w1_spec, w2_spec]),
        out_specs=[
            pl.BlockSpec((1, D_ATTN, tm), lambda b, s: (b, 0, s)),
            pl.BlockSpec((1, tm, D_ATTN), lambda b, s: (b, s, 0)),
            pl.BlockSpec((1, N_HEADS * V_ROWS, tm), lambda b, s: (b, 0, s)),
            pl.BlockSpec((1, tm // BLOCK, 1, D_ATTN), lambda b, s: (b, s, 0, 0)),
            pl.BlockSpec((1, tm, d_g), lambda b, s: (b, s, 0)),
            pl.BlockSpec((1, tm, d), lambda b, s: (b, s, 0)),
            w1_spec, w2_spec,
        ],
        out_shape=[
            jax.ShapeDtypeStruct((B, D_ATTN, S), BF16),
            jax.ShapeDtypeStruct((B, S, D_ATTN), BF16),
            jax.ShapeDtypeStruct((B, N_HEADS * V_ROWS, S), BF16),
            jax.ShapeDtypeStruct((B, nb, 1, D_ATTN), F32),
            jax.ShapeDtypeStruct((B, S, d_g), BF16),
            jax.ShapeDtypeStruct((B, S, d), BF16),
            jax.ShapeDtypeStruct(w_ff1.shape, BF16),
            jax.ShapeDtypeStruct(w_ff2.shape, BF16),
        ],
        scratch_shapes=[pltpu.VMEM((n_sub, tr, d), F32) for _ in range(5)] + [
            pltpu.VMEM((CONV_WIDTH - 1, SUBLANES, d), F32), pltpu.VMEM((1, d), F32),
            pltpu.VMEM((D_ATTN, D), BF16), pltpu.VMEM((D, D_ATTN), BF16), pltpu.VMEM((D_ATTN, D), BF16),
            pltpu.VMEM((D, 2 * d), BF16), pltpu.VMEM((D, d_g), BF16),
            pltpu.VMEM((2, D, STAGE_COLS), F32), pltpu.SemaphoreType.DMA((2,))],
        compiler_params=pltpu.CompilerParams(
            dimension_semantics=("arbitrary", "arbitrary"), vmem_limit_bytes=VMEM_LIMIT),
        name="inproj_rglru",
    )(x, w_in, *consts, w_ff1, w_ff2)


def _attn_body(*refs):
    for sub in range(Q_BLOCKS_PER_STEP):
        _attn_block(sub, *refs)


def _attn_block(sub, far_ref, qT_ref, k_ref, vT_ref, km_ref, bd_ref, bp_ref, o_ref, sel_ref, qk_ref, m_ref, acc_ref):
    p = pl.program_id(1)
    i = pl.program_id(2) * Q_BLOCKS_PER_STEP + sub
    q_cols = slice(sub * BLOCK, (sub + 1) * BLOCK)
    nb = km_ref.shape[1]
    hps = bd_ref.shape[0]
    heads = range(hps)
    groups = range(hps // TILE_HEADS)
    lanes = TILE_HEADS * HEAD_DIM
    row = lax.broadcasted_iota(jnp.int32, (lanes, BLOCK), 0)
    blk = lax.broadcasted_iota(jnp.int32, (nb, BLOCK), 0)
    lane_head = lax.broadcasted_iota(jnp.int32, (nb, lanes), 1) // HEAD_DIM
    blk_f = blk.astype(F32)
    past = blk < i
    qz = []
    for g in groups:
        qall = qT_ref[0, g * lanes:(g + 1) * lanes, q_cols]
        qz += [jnp.where((row >= t * HEAD_DIM) & (row < (t + 1) * HEAD_DIM), qall, jnp.zeros_like(qall))
               for t in range(TILE_HEADS)]

        km = km_ref[0, :, g * lanes:(g + 1) * lanes]
        km_hi = km.astype(BF16)
        km_r = km - km_hi.astype(F32)
        km_mid = km_r.astype(BF16)
        km_lo = (km_r - km_mid.astype(F32)).astype(BF16)
        stacked = jnp.concatenate(
            [jnp.where(lane_head == t, piece, jnp.zeros_like(piece))
             for piece in (km_hi, km_mid, km_lo) for t in range(TILE_HEADS)], axis=0)
        gates = jnp.dot(stacked, qall, preferred_element_type=F32)
        for t in range(TILE_HEADS):
            hh = g * TILE_HEADS + t
            gate = (gates[t * nb:(t + 1) * nb] + gates[(TILE_HEADS + t) * nb:(TILE_HEADS + t + 1) * nb]
                    + gates[(2 * TILE_HEADS + t) * nb:(2 * TILE_HEADS + t + 1) * nb])
            gate = jnp.where(past, gate, -jnp.inf)
            chosen = jnp.zeros((nb, BLOCK), F32)
            for _ in range(TOPK):
                best = jnp.max(gate, axis=0, keepdims=True)
                first = jnp.min(jnp.where(gate == best, blk_f, float(nb)), axis=0, keepdims=True)
                hit = blk_f == first
                gate = jnp.where(hit, -jnp.inf, gate)
                chosen = jnp.where(hit, 1.0, chosen)
            sel_ref[hh] = jnp.where(past & (chosen > 0.5), far_ref[p * hps + hh], NEG)

    def keys(j, g):
        return k_ref[0, pl.ds(pl.multiple_of(j * BLOCK, BLOCK), BLOCK), g * lanes:(g + 1) * lanes]

    def values(j, hh):
        return vT_ref[0, hh * V_ROWS:(hh + 1) * V_ROWS,
                      pl.ds(pl.multiple_of(j * BLOCK, BLOCK), BLOCK)]

    def scores(kb, hh, bias):
        qk = jnp.dot(kb, qz[hh], preferred_element_type=F32)
        return qk if bias is None else qk + bias[hh]

    def step(slot, j, row, nxt_slot, nxt_j, nxt_bias=None):
        kbs = [keys(nxt_j, g) for g in groups]

        def issue(hh):
            qk_ref[nxt_slot, hh] = scores(kbs[hh // TILE_HEADS], hh, nxt_bias)

        for hh in range(AHEAD):
            issue(hh)
        for hh in heads:
            if hh + AHEAD < hps:
                issue(hh + AHEAD)
            m_prev = m_ref[hh]
            s = qk_ref[slot, hh]
            r = row(hh)
            m_new = jnp.maximum(m_prev, jnp.max(s, axis=0, keepdims=True) + r)
            m_ref[hh] = m_new
            alpha = jnp.exp2(m_prev - m_new)
            pb = jnp.exp2((s - (m_new - r)).astype(BF16))
            acc_ref[hh] = alpha * acc_ref[hh] + jnp.dot(values(j, hh), pb, preferred_element_type=F32)

    jp = jnp.maximum(i - 1, 0)
    n_old = jp
    m_ref[...] = jnp.full(m_ref.shape, NEG, F32)
    acc_ref[...] = jnp.zeros(acc_ref.shape, F32)
    zero_row = jnp.zeros((1, BLOCK), F32)
    for g in groups:
        kb = keys(i, g)
        for hh in range(g * TILE_HEADS, (g + 1) * TILE_HEADS):
            qk_ref[0, hh] = scores(kb, hh, bd_ref)
    step(0, i, lambda hh: zero_row, 1, jp, bp_ref)
    step(1, jp, lambda hh: jnp.where(sel_ref[hh, pl.ds(jp, 1), :] > 0.5 * NEG, 0.0, NEG), 0, 0)

    def older_pair(t, carry):
        ja = 2 * t
        jb = ja + 1
        step(0, ja, lambda hh: sel_ref[hh, pl.ds(ja, 1), :], 1, jb)
        step(1, jb, lambda hh: jnp.where(jb < n_old, sel_ref[hh, pl.ds(jb, 1), :], NEG), 0, ja + 2)
        return carry

    lax.fori_loop(0, (n_old + 1) // 2, older_pair, 0)
    outs = [acc_ref[hh, :HEAD_DIM, :] * (1.0 / acc_ref[hh, HEAD_DIM:HEAD_DIM + 1, :]) for hh in heads]
    o_ref[0, q_cols, :] = jnp.concatenate(outs, axis=0).T.astype(BF16)


def _attention(qT, k, vT, kmean, bias_d, bias_p, bias_far, hps):
    B, _, S = qT.shape
    nb = S // BLOCK
    lanes = hps * HEAD_DIM
    qb = Q_BLOCKS_PER_STEP
    grid = (B, N_HEADS // hps, nb // qb)
    assert hps % TILE_HEADS == 0 and nb % qb == 0
    return pl.pallas_call(
        _attn_body,
        grid=grid,
        in_specs=[
            pl.BlockSpec(memory_space=pltpu.SMEM),
            pl.BlockSpec((1, lanes, qb * BLOCK), lambda b, p, i: (b, p, i)),
            pl.BlockSpec((1, S, lanes), lambda b, p, i: (b, 0, p)),
            pl.BlockSpec((1, hps * V_ROWS, S), lambda b, p, i: (b, p, 0)),
            pl.BlockSpec((1, nb, lanes), lambda b, p, i: (b, 0, p)),
            pl.BlockSpec((hps, BLOCK, BLOCK), lambda b, p, i: (p, 0, 0)),
            pl.BlockSpec((hps, BLOCK, BLOCK), lambda b, p, i: (p, 0, 0)),
        ],
        out_specs=pl.BlockSpec((1, qb * BLOCK, lanes), lambda b, p, i: (b, i, p)),
        out_shape=jax.ShapeDtypeStruct((B, S, D_ATTN), BF16),
        scratch_shapes=[pltpu.VMEM((hps, nb, BLOCK), F32), pltpu.VMEM((2, hps, BLOCK, BLOCK), F32),
                        pltpu.VMEM((hps, 1, BLOCK), F32), pltpu.VMEM((hps, V_ROWS, BLOCK), F32)],
        compiler_params=pltpu.CompilerParams(
            dimension_semantics=("parallel", "parallel", "arbitrary"), vmem_limit_bytes=VMEM_LIMIT),
        name="moba_attn",
    )(bias_far, qT, k, vT, kmean, bias_d, bias_p)


def _out_body(x_ref, oa_ref, or_ref, ga_ref, gr_ref, bg_ref, pa_ref, pr_ref, wo_ref, n2_ref,
              w1_ref, w2_ref, y_ref):
    a = jnp.dot(oa_ref[...], pa_ref[...], preferred_element_type=F32)
    r = jnp.dot(or_ref[...], pr_ref[...], preferred_element_type=F32)
    g_a = jax.nn.sigmoid(ga_ref[...].astype(F32) + bg_ref[0:1, :])
    g_r = jax.nn.sigmoid(gr_ref[...].astype(F32) + bg_ref[1:2, :])
    merged = (g_a * a + g_r * r).astype(BF16)
    x1 = x_ref[...] + jnp.dot(merged, wo_ref[...], preferred_element_type=F32)
    ms = jnp.mean(x1 * x1, axis=-1, keepdims=True)
    h2 = (x1 * lax.rsqrt(ms + EPS) * n2_ref[...]).astype(BF16)
    d_ff = w1_ref.shape[1]
    cw = 1024
    acc = x1
    for c in range(d_ff // cw):
        t = jnp.dot(h2, w1_ref[:, c * cw:(c + 1) * cw], preferred_element_type=F32)
        t = jnp.maximum(t, 0.0)
        acc = acc + jnp.dot((t * t).astype(BF16), w2_ref[c * cw:(c + 1) * cw, :],
                            preferred_element_type=F32)
    y_ref[...] = acc


def _out(x2, o_attn, o_rnn, gates, bg, pa, pr, wo, n2, w1, w2, tm):
    T, D = x2.shape
    d_rnn = o_rnn.shape[1]
    grid = (T // tm,)
    return pl.pallas_call(
        _out_body,
        grid=grid,
        in_specs=[
            pl.BlockSpec((tm, D), lambda t: (t, 0)),
            pl.BlockSpec((tm, D_ATTN), lambda t: (t, 0)),
            pl.BlockSpec((tm, d_rnn), lambda t: (t, 0)),
            pl.BlockSpec((tm, D), lambda t: (t, 0)),
            pl.BlockSpec((tm, D), lambda t: (t, 1)),
            _const_spec(bg.shape), _const_spec(pa.shape), _const_spec(pr.shape),
            _const_spec(wo.shape), _const_spec(n2.shape), _const_spec(w1.shape),
            _const_spec(w2.shape),
        ],
        out_specs=pl.BlockSpec((tm, D), lambda t: (t, 0)),
        out_shape=jax.ShapeDtypeStruct((T, D), F32),
        compiler_params=pltpu.CompilerParams(
            dimension_semantics=("parallel",), vmem_limit_bytes=VMEM_LIMIT),
        name="merge_mlp",
    )(x2, o_attn, o_rnn, gates, gates, bg, pa, pr, wo, n2, w1, w2)


def _t5_bucket(rel):
    max_exact = NUM_BUCKETS // 2
    n = jnp.maximum(rel, 0)
    nf = jnp.maximum(n, 1).astype(F32)
    large = max_exact + (jnp.log(nf / max_exact) / math.log(MAX_DISTANCE / max_exact)
                         * (NUM_BUCKETS - max_exact)).astype(jnp.int32)
    large = jnp.minimum(large, NUM_BUCKETS - 1)
    return jnp.where(n < max_exact, n, large)


def _bias_tables(rel_bias):
    offs = jnp.arange(BLOCK)
    rel_own = offs[None, :] - offs[:, None]
    buckets = jnp.arange(NUM_BUCKETS)

    def lookup(rel):
        onehot = (_t5_bucket(rel)[..., None] == buckets).astype(F32)
        return jnp.einsum('kqb,bh->hkq', onehot, rel_bias, precision=lax.Precision.HIGHEST)

    bias_d = jnp.where(rel_own >= 0, lookup(rel_own) * LOG2E, NEG)
    bias_p = lookup(rel_own + BLOCK) * LOG2E
    bias_far = rel_bias[NUM_BUCKETS - 1] * LOG2E
    return bias_d.astype(F32), bias_p.astype(F32), bias_far.astype(F32)


def _block_diag(w):
    per = GROUP // RNN_BLOCK_DIM
    w4 = w.reshape(RNN_BLOCKS // per, per, RNN_BLOCK_DIM, RNN_BLOCK_DIM)
    eye = jnp.eye(per, dtype=w.dtype)
    return jnp.einsum('gade,ab->gadbe', w4, eye).reshape(RNN_BLOCKS // per, GROUP, GROUP)


def _layer(x, norm1_w, w_in, b_gate, q_norm_w, k_norm_w, bias_tabs, conv_w, conv_b,
           w_rg_a, b_rg_a, w_rg_i, b_rg_i, lru_lambda, w_proj_attn, w_proj_rnn,
           w_out, norm2_w, w_ff1, w_ff2, tm_in, tr, tm_out):
    B, S, D = x.shape
    d_rnn = conv_w.shape[1]
    qw = (q_norm_w * (HEAD_DIM ** -0.5 * LOG2E)).reshape(1, HEAD_DIM, 1)
    kw = jnp.tile(k_norm_w, N_HEADS).reshape(1, D_ATTN)
    head_of = np.arange(D_ATTN) // HEAD_DIM
    gmat = jnp.asarray(head_of[:, None] == head_of[None, :], dtype=BF16)

    qT, k, vT, kmean, gates, o_rnn, w1b, w2b = _inproj_rnn(
        x, w_in, norm1_w.reshape(1, D), qw, kw, gmat, conv_w, conv_b.reshape(1, d_rnn),
        _block_diag(w_rg_a).astype(BF16), _block_diag(w_rg_i).astype(BF16),
        b_rg_a.reshape(1, d_rnn), b_rg_i.reshape(1, d_rnn), lru_lambda.reshape(1, d_rnn),
        w_ff1, w_ff2, tm_in, tr)
    o_attn = _attention(qT, k, vT, kmean.reshape(B, S // BLOCK, D_ATTN), *bias_tabs, HEADS_PER_STEP)
    y = _out(x.reshape(B * S, D), o_attn.reshape(B * S, D_ATTN), o_rnn.reshape(B * S, d_rnn),
             gates.reshape(B * S, -1), b_gate, w_proj_attn.astype(BF16), w_proj_rnn.astype(BF16),
             w_out.astype(BF16), norm2_w.reshape(1, D), w1b, w2b, tm_out)
    return y.reshape(B, S, D)


def kernel(x, norm1_w, w_in, b_gate, q_norm_w, k_norm_w, rel_bias, conv_w, conv_b, w_rg_a, b_rg_a,
           w_rg_i, b_rg_i, lru_lambda, w_proj_attn, w_proj_rnn, w_out, norm2_w, w_ff1, w_ff2):
    S = x.shape[1]
    assert S % BLOCK == 0
    bias_tabs = _bias_tables(rel_bias)
    tm_in = min(512, S)
    tr = min(256, S)
    tm_out = min(512, S)
    for l in range(norm1_w.shape[0]):
        x = _layer(x, norm1_w[l], w_in[l], b_gate[l], q_norm_w[l], k_norm_w[l], bias_tabs,
                   conv_w[l], conv_b[l], w_rg_a[l], b_rg_a[l], w_rg_i[l], b_rg_i[l], lru_lambda[l],
                   w_proj_attn[l], w_proj_rnn[l], w_out[l], norm2_w[l], w_ff1[l], w_ff2[l],
                   tm_in, tr, tm_out)
    return x
```

```python
import math

import numpy as np
import jax
import jax.numpy as jnp
from jax import lax
from jax.experimental import pallas as pl
from jax.experimental.pallas import tpu as pltpu

N_HEADS = 8
HEAD_DIM = 64
D_ATTN = N_HEADS * HEAD_DIM
BLOCK = 256
TOPK = 3
NUM_BUCKETS = 32
MAX_DISTANCE = 128
RNN_BLOCKS = 16
RNN_BLOCK_DIM = 64
CONV_WIDTH = 4
LRU_C = 8.0
EPS = 1e-6
NEG = -1e30
LOG2E = math.log2(math.e)
HEADS_PER_STEP = 8
TILE_HEADS = 4
AHEAD = 2
Q_BLOCKS_PER_STEP = 4
GROUP = 256
STAGE_COLS = 512
STAGE_ROWS = 512
SUBLANES = 8
BF16_ROWS = 16
V_ROWS = HEAD_DIM + BF16_ROWS
VMEM_LIMIT = 56 * 1024 * 1024

F32 = jnp.float32
BF16 = jnp.bfloat16
_NT = (((1,), (1,)), ((), ()))


def _const_spec(shape):
    nd = len(shape)
    return pl.BlockSpec(shape, lambda *_: (0,) * nd, pipeline_mode=pl.Buffered(1))


def _inproj_rnn_body(x_ref, win_ref, n1_ref, qw_ref, kw_ref, g_ref,
                     perm_ref, permT_ref, cw_ref, cb_ref, wa_ref, wi_ref, ba_ref, bi_ref, lam_ref,
                     w1_ref, w2_ref,
                     qT_ref, k_ref, vT_ref, km_ref, gates_ref, orn_ref, w1b_ref, w2b_ref,
                     xs_ref, y_ref, xc_ref, a_ref, u_ref, tail_ref, hc_ref,
                     wqT_ref, wk_ref, wvT_ref, wxy_ref, wg_ref, stage_ref, stage_sem):
    tm = x_ref.shape[1]
    d = cw_ref.shape[1]
    tr = perm_ref.shape[0]
    seg = tr // SUBLANES
    taps = CONV_WIDTH - 1

    @pl.when(pl.program_id(1) == 0)
    def _():
        tail_ref[...] = jnp.zeros_like(tail_ref)
        hc_ref[...] = jnp.zeros_like(hc_ref)

    @pl.when((pl.program_id(0) == 0) & (pl.program_id(1) == 0))
    def _():
        n_chunks = win_ref.shape[1] // STAGE_COLS
        dests = ([(wqT_ref, None), (wk_ref, None), (wvT_ref, None)]
                 + [(wxy_ref, c) for c in range(wxy_ref.shape[1] // STAGE_COLS)]
                 + [(wg_ref, c) for c in range(wg_ref.shape[1] // STAGE_COLS)])
        assert len(dests) == n_chunks and D_ATTN == STAGE_COLS

        def chunk_copy(c):
            return pltpu.make_async_copy(win_ref.at[:, pl.ds(c * STAGE_COLS, STAGE_COLS)],
                                         stage_ref.at[c % 2], stage_sem.at[c % 2])

        chunk_copy(0).start()
        for c, (dst, col) in enumerate(dests):
            if c + 1 < n_chunks:
                chunk_copy(c + 1).start()
            chunk_copy(c).wait()
            w = stage_ref[c % 2]
            if dst is wqT_ref or dst is wvT_ref:
                dst[...] = w.T.astype(BF16)
            elif col is None:
                dst[...] = w.astype(BF16)
            else:
                dst[:, col * STAGE_COLS:(col + 1) * STAGE_COLS] = w.astype(BF16)

    w1b_ref[...] = w1_ref[...].astype(BF16)
    w2b_ref[...] = w2_ref[...].astype(BF16)

    x = x_ref[0]
    ms = jnp.mean(x * x, axis=-1, keepdims=True)
    h = (x * lax.rsqrt(ms + EPS) * n1_ref[...]).astype(BF16)

    def rnn_inputs(t):
        hp = jnp.dot(perm_ref[...], h[t * tr:(t + 1) * tr], preferred_element_type=F32).astype(BF16)
        xs_ref[t] = jnp.dot(hp, wxy_ref[:, :d], preferred_element_type=F32)
        y_ref[t] = jnp.dot(hp, wxy_ref[:, d:], preferred_element_type=F32)

    def rnn_conv(t):
        def slab(k):
            return xs_ref[t, k * SUBLANES:(k + 1) * SUBLANES, :]

        first_seg = lax.broadcasted_iota(jnp.int32, (SUBLANES, d), 0) == 0
        last = [slab(seg - taps + j) for j in range(taps)]
        window = [jnp.where(first_seg, pltpu.roll(tail_ref[j], 1, 0), pltpu.roll(last[j], 1, 0))
                  for j in range(taps)]
        for j in range(taps):
            tail_ref[j] = last[j]
        cw = [0.5 * cw_ref[j:j + 1, :] for j in range(CONV_WIDTH)]
        cb = 0.5 * cb_ref[...]
        for k in range(seg):
            cur = slab(k)
            acc = cb + window[0] * cw[0]
            for j in range(1, taps):
                acc = acc + window[j] * cw[j]
            xc_ref[t, k * SUBLANES:(k + 1) * SUBLANES, :] = acc + cur * cw[taps]
            window = window[1:] + [cur]

    def rnn_gate_mm(t):
        xb = xc_ref[t].astype(BF16)
        for g in range(d // GROUP):
            cols = slice(g * GROUP, (g + 1) * GROUP)
            a_ref[t, :, cols] = jnp.dot(xb[:, cols], wa_ref[g], preferred_element_type=F32)
            u_ref[t, :, cols] = jnp.dot(xb[:, cols], wi_ref[g], preferred_element_type=F32)

    def rnn_nonlin(t):
        half_x = xc_ref[t]
        t_r = jnp.tanh(a_ref[t] + 0.5 * ba_ref[...])
        t_i = jnp.tanh(u_ref[t] + 0.5 * bi_ref[...])
        nlam = -lam_ref[...]
        softplus = jnp.maximum(nlam, 0.0) + jnp.log1p(jnp.exp(-jnp.abs(nlam)))
        half_c = softplus * (-0.5 * LRU_C * LOG2E)
        a = jnp.exp2(t_r * half_c + half_c)
        a_ref[t] = a
        gap = 1.0 - a * a
        u_ref[t] = (gap * lax.rsqrt(jnp.maximum(gap, 1e-30))) * (half_x * t_i + half_x)

    def rnn_scan(t):
        hcur = u_ref[t, 0:SUBLANES, :]
        prod = a_ref[t, 0:SUBLANES, :]
        for k in range(1, seg):
            rows = slice(k * SUBLANES, (k + 1) * SUBLANES)
            a_k = a_ref[t, rows, :]
            hcur = a_k * hcur + u_ref[t, rows, :]
            prod = a_k * prod
            u_ref[t, rows, :] = hcur
            a_ref[t, rows, :] = prod
        entry = [hc_ref[...]]
        for s in range(SUBLANES):
            entry.append(prod[s:s + 1, :] * entry[s] + hcur[s:s + 1, :])
        hc_ref[...] = entry[SUBLANES]
        h_in = jnp.concatenate(entry[:SUBLANES], axis=0)
        for k in range(seg):
            rows = slice(k * SUBLANES, (k + 1) * SUBLANES)
            u_ref[t, rows, :] = a_ref[t, rows, :] * h_in + u_ref[t, rows, :]

    def rnn_output(t):
        y = y_ref[t]
        k1 = -2.0 * math.sqrt(2.0 / math.pi) * LOG2E
        gelu = y / (1.0 + jnp.exp2(y * (k1 + (k1 * 0.044715) * (y * y))))
        o_slab = (u_ref[t] * gelu).astype(BF16)
        orn_ref[0, t * tr:(t + 1) * tr, :] = jnp.dot(
            permT_ref[...], o_slab, preferred_element_type=F32).astype(BF16)

    def proj_q():
        qT = lax.dot_general(wqT_ref[...], h, _NT, preferred_element_type=F32)
        q3 = qT.reshape(N_HEADS, HEAD_DIM, tm)
        qss = jnp.mean(q3 * q3, axis=1, keepdims=True)
        qn = q3 * lax.rsqrt(qss + EPS) * qw_ref[...]
        qT_ref[0] = qn.reshape(D_ATTN, tm).astype(BF16)

    def proj_v():
        vT = lax.dot_general(wvT_ref[...], h, _NT, preferred_element_type=F32).astype(BF16)
        ones_rows = (lax.broadcasted_iota(jnp.int32, (BF16_ROWS, tm), 0) == 0).astype(BF16)
        for hd in range(N_HEADS):
            vT_ref[0, hd * V_ROWS:hd * V_ROWS + HEAD_DIM, :] = vT[hd * HEAD_DIM:(hd + 1) * HEAD_DIM]
            vT_ref[0, hd * V_ROWS + HEAD_DIM:(hd + 1) * V_ROWS, :] = ones_rows

    def proj_k():
        k = jnp.dot(h, wk_ref[...], preferred_element_type=F32)
        k2 = k * k
        hi = k2.astype(BF16)
        lo = (k2 - hi.astype(F32)).astype(BF16)
        kss = (jnp.dot(hi, g_ref[...], preferred_element_type=F32)
               + jnp.dot(lo, g_ref[...], preferred_element_type=F32))
        kn = k * lax.rsqrt(kss * (1.0 / HEAD_DIM) + EPS) * kw_ref[...]
        k_ref[0] = kn.astype(BF16)
        for bi in range(tm // BLOCK):
            km_ref[0, bi] = jnp.mean(kn[bi * BLOCK:(bi + 1) * BLOCK], axis=0, keepdims=True)

    gate_chunk = 512

    def proj_gate(c):
        cols = slice(c * gate_chunk, (c + 1) * gate_chunk)
        gates_ref[0, :, cols] = jnp.dot(h, wg_ref[:, cols], preferred_element_type=F32).astype(BF16)

    assert tm == 2 * tr and wg_ref.shape[1] == 4 * gate_chunk
    rnn_inputs(0)
    proj_q()
    rnn_conv(0)
    rnn_gate_mm(0)
    proj_v()
    proj_gate(0)
    rnn_inputs(1)
    rnn_nonlin(0)
    rnn_scan(0)
    rnn_output(0)
    proj_k()
    rnn_conv(1)
    rnn_gate_mm(1)
    proj_gate(1)
    proj_gate(2)
    proj_gate(3)
    rnn_nonlin(1)
    rnn_scan(1)
    rnn_output(1)


def _inproj_rnn(x, w_in, n1, qw, kw, gmat, conv_w, conv_b, wa_bd, wi_bd, b_a, b_i, lam,
                w_ff1, w_ff2, tm, tr):
    B, S, D = x.shape
    d = conv_w.shape[1]
    d_g = w_in.shape[1] - 3 * D_ATTN - 2 * d
    nb = S // BLOCK
    n_sub = tm // tr
    grid = (B, S // tm)
    seg = tr // SUBLANES
    t = np.arange(tr)
    perm_np = np.zeros((tr, tr), np.float32)
    perm_np[(t % seg) * SUBLANES + t // seg, t] = 1.0
    perm = jnp.asarray(perm_np, BF16)
    permT = jnp.asarray(perm_np.T, BF16)
    consts = (n1, qw, kw, gmat, perm, permT, conv_w, conv_b, wa_bd, wi_bd, b_a, b_i, lam)
    n_s = S // tm
    n_steps = B * n_s
    r1, r2 = w_ff1.shape[0] // n_steps, w_ff2.shape[0] // n_steps
    assert r1 * n_steps == w_ff1.shape[0] and r2 * n_steps == w_ff2.shape[0]
    assert r1 % BF16_ROWS == 0 and r2 % BF16_ROWS == 0
    w1_spec = pl.BlockSpec((r1, w_ff1.shape[1]), lambda b, s: (b * n_s + s, 0))
    w2_spec = pl.BlockSpec((r2, w_ff2.shape[1]), lambda b, s: (b * n_s + s, 0))
    return pl.pallas_call(
        _inproj_rnn_body,
        grid=grid,
        in_specs=([pl.BlockSpec((1, tm, D), lambda b, s: (b, s, 0)), pl.BlockSpec(memory_space=pl.ANY)]
                  + [_const_spec(c.shape) for c in consts] + [w1_spec, w2_spec]),
        out_specs=[
            pl.BlockSpec((1, D_ATTN, tm), lambda b, s: (b, 0, s)),
            pl.BlockSpec((1, tm, D_ATTN), lambda b, s: (b, s, 0)),
            pl.BlockSpec((1, N_HEADS * V_ROWS, tm), lambda b, s: (b, 0, s)),
            pl.BlockSpec((1, tm // BLOCK, 1, D_ATTN), lambda b, s: (b, s, 0, 0)),
            pl.BlockSpec((1, tm, d_g), lambda b, s: (b, s, 0)),
            pl.BlockSpec((1, tm, d), lambda b, s: (b, s, 0)),
            w1_spec, w2_spec,
        ],
        out_shape=[
            jax.ShapeDtypeStruct((B, D_ATTN, S), BF16),
            jax.ShapeDtypeStruct((B, S, D_ATTN), BF16),
            jax.ShapeDtypeStruct((B, N_HEADS * V_ROWS, S), BF16),
            jax.ShapeDtypeStruct((B, nb, 1, D_ATTN), F32),
            jax.ShapeDtypeStruct((B, S, d_g), BF16),
            jax.ShapeDtypeStruct((B, S, d), BF16),
            jax.ShapeDtypeStruct(w_ff1.shape, BF16),
            jax.ShapeDtypeStruct(w_ff2.shape, BF16),
        ],
        scratch_shapes=[pltpu.VMEM((n_sub, tr, d), F32) for _ in range(5)] + [
            pltpu.VMEM((CONV_WIDTH - 1, SUBLANES, d), F32), pltpu.VMEM((1, d), F32),
            pltpu.VMEM((D_ATTN, D), BF16), pltpu.VMEM((D, D_ATTN), BF16), pltpu.VMEM((D_ATTN, D), BF16),
            pltpu.VMEM((D, 2 * d), BF16), pltpu.VMEM((D, d_g), BF16),
            pltpu.VMEM((2, D, STAGE_COLS), F32), pltpu.SemaphoreType.DMA((2,))],
        compiler_params=pltpu.CompilerParams(
            dimension_semantics=("arbitrary", "arbitrary"), vmem_limit_bytes=VMEM_LIMIT),
        name="inproj_rglru",
    )(x, w_in, *consts, w_ff1, w_ff2)


def _attn_body(*refs):
    for sub in range(Q_BLOCKS_PER_STEP):
        _attn_block(sub, *refs)


def _attn_block(sub, far_ref, qT_ref, k_ref, vT_ref, km_ref, bd_ref, bp_ref, o_ref, sel_ref, qk_ref, m_ref, acc_ref):
    p = pl.program_id(1)
    i = pl.program_id(2) * Q_BLOCKS_PER_STEP + sub
    q_cols = slice(sub * BLOCK, (sub + 1) * BLOCK)
    nb = km_ref.shape[1]
    hps = bd_ref.shape[0]
    heads = range(hps)
    groups = range(hps // TILE_HEADS)
    lanes = TILE_HEADS * HEAD_DIM
    row = lax.broadcasted_iota(jnp.int32, (lanes, BLOCK), 0)
    blk = lax.broadcasted_iota(jnp.int32, (nb, BLOCK), 0)
    lane_head = lax.broadcasted_iota(jnp.int32, (nb, lanes), 1) // HEAD_DIM
    blk_f = blk.astype(F32)
    past = blk < i
    qz = []
    for g in groups:
        qall = qT_ref[0, g * lanes:(g + 1) * lanes, q_cols]
        qz += [jnp.where((row >= t * HEAD_DIM) & (row < (t + 1) * HEAD_DIM), qall, jnp.zeros_like(qall))
               for t in range(TILE_HEADS)]

        km = km_ref[0, :, g * lanes:(g + 1) * lanes]
        km_hi = km.astype(BF16)
        km_r = km - km_hi.astype(F32)
        km_mid = km_r.astype(BF16)
        km_lo = (km_r - km_mid.astype(F32)).astype(BF16)
        stacked = jnp.concatenate(
            [jnp.where(lane_head == t, piece, jnp.zeros_like(piece))
             for piece in (km_hi, km_mid, km_lo) for t in range(TILE_HEADS)], axis=0)
        gates = jnp.dot(stacked, qall, preferred_element_type=F32)
        for t in range(TILE_HEADS):
            hh = g * TILE_HEADS + t
            gate = (gates[t * nb:(t + 1) * nb] + gates[(TILE_HEADS + t) * nb:(TILE_HEADS + t + 1) * nb]
                    + gates[(2 * TILE_HEADS + t) * nb:(2 * TILE_HEADS + t + 1) * nb])
            gate = jnp.where(past, gate, -jnp.inf)
            chosen = jnp.zeros((nb, BLOCK), F32)
            for _ in range(TOPK):
                best = jnp.max(gate, axis=0, keepdims=True)
                first = jnp.min(jnp.where(gate == best, blk_f, float(nb)), axis=0, keepdims=True)
                hit = blk_f == first
                gate = jnp.where(hit, -jnp.inf, gate)
                chosen = jnp.where(hit, 1.0, chosen)
            sel_ref[hh] = jnp.where(past & (chosen > 0.5), far_ref[p * hps + hh], NEG)

    def keys(j, g):
        return k_ref[0, pl.ds(pl.multiple_of(j * BLOCK, BLOCK), BLOCK), g * lanes:(g + 1) * lanes]

    def values(j, hh):
        return vT_ref[0, hh * V_ROWS:(hh + 1) * V_ROWS,
                      pl.ds(pl.multiple_of(j * BLOCK, BLOCK), BLOCK)]

    def scores(kb, hh, bias):
        qk = jnp.dot(kb, qz[hh], preferred_element_type=F32)
        return qk if bias is None else qk + bias[hh]

    def step(slot, j, row, nxt_slot, nxt_j, nxt_bias=None):
        kbs = [keys(nxt_j, g) for g in groups]

        def issue(hh):
            qk_ref[nxt_slot, hh] = scores(kbs[hh // TILE_HEADS], hh, nxt_bias)

        for hh in range(AHEAD):
            issue(hh)
        for hh in heads:
            if hh + AHEAD < hps:
                issue(hh + AHEAD)
            m_prev = m_ref[hh]
            s = qk_ref[slot, hh]
            r = row(hh)
            m_new = jnp.maximum(m_prev, jnp.max(s, axis=0, keepdims=True) + r)
            m_ref[hh] = m_new
            alpha = jnp.exp2(m_prev - m_new)
            pb = jnp.exp2((s - (m_new - r)).astype(BF16))
            acc_ref[hh] = alpha * acc_ref[hh] + jnp.dot(values(j, hh), pb, preferred_element_type=F32)

    jp = jnp.maximum(i - 1, 0)
    n_old = jp
    m_ref[...] = jnp.full(m_ref.shape, NEG, F32)
    acc_ref[...] = jnp.zeros(acc_ref.shape, F32)
    zero_row = jnp.zeros((1, BLOCK), F32)
    for g in groups:
        kb = keys(i, g)
        for hh in range(g * TILE_HEADS, (g + 1) * TILE_HEADS):
            qk_ref[0, hh] = scores(kb, hh, bd_ref)
    step(0, i, lambda hh: zero_row, 1, jp, bp_ref)
    step(1, jp, lambda hh: jnp.where(sel_ref[hh, pl.ds(jp, 1), :] > 0.5 * NEG, 0.0, NEG), 0, 0)

    def older_pair(t, carry):
        ja = 2 * t
        jb = ja + 1
        step(0, ja, lambda hh: sel_ref[hh, pl.ds(ja, 1), :], 1, jb)
        step(1, jb, lambda hh: jnp.where(jb < n_old, sel_ref[hh, pl.ds(jb, 1), :], NEG), 0, ja + 2)
        return carry

    lax.fori_loop(0, (n_old + 1) // 2, older_pair, 0)
    outs = [acc_ref[hh, :HEAD_DIM, :] * (1.0 / acc_ref[hh, HEAD_DIM:HEAD_DIM + 1, :]) for hh in heads]
    o_ref[0, q_cols, :] = jnp.concatenate(outs, axis=0).T.astype(BF16)


def _attention(qT, k, vT, kmean, bias_d, bias_p, bias_far, hps):
    B, _, S = qT.shape
    nb = S // BLOCK
    lanes = hps * HEAD_DIM
    qb = Q_BLOCKS_PER_STEP
    grid = (B, N_HEADS // hps, nb // qb)
    assert hps % TILE_HEADS == 0 and nb % qb == 0
    return pl.pallas_call(
        _attn_body,
        grid=grid,
        in_specs=[
            pl.BlockSpec(memory_space=pltpu.SMEM),
            pl.BlockSpec((1, lanes, qb * BLOCK), lambda b, p, i: (b, p, i)),
            pl.BlockSpec((1, S, lanes), lambda b, p, i: (b, 0, p)),
            pl.BlockSpec((1, hps * V_ROWS, S), lambda b, p, i: (b, p, 0)),
            pl.BlockSpec((1, nb, lanes), lambda b, p, i: (b, 0, p)),
            pl.BlockSpec((hps, BLOCK, BLOCK), lambda b, p, i: (p, 0, 0)),
            pl.BlockSpec((hps, BLOCK, BLOCK), lambda b, p, i: (p, 0, 0)),
        ],
        out_specs=pl.BlockSpec((1, qb * BLOCK, lanes), lambda b, p, i: (b, i, p)),
        out_shape=jax.ShapeDtypeStruct((B, S, D_ATTN), BF16),
        scratch_shapes=[pltpu.VMEM((hps, nb, BLOCK), F32), pltpu.VMEM((2, hps, BLOCK, BLOCK), F32),
                        pltpu.VMEM((hps, 1, BLOCK), F32), pltpu.VMEM((hps, V_ROWS, BLOCK), F32)],
        compiler_params=pltpu.CompilerParams(
            dimension_semantics=("parallel", "parallel", "arbitrary"), vmem_limit_bytes=VMEM_LIMIT),
        name="moba_attn",
    )(bias_far, qT, k, vT, kmean, bias_d, bias_p)


def _out_body(x_ref, oa_ref, or_ref, ga_ref, gr_ref, bg_ref, pa_hbm, pr_hbm, wo_hbm, n2_ref,
              w1_ref, w2_ref, y_ref, pa_ref, pr_ref, wo_ref, stage_ref, stage_sem):
    @pl.when(pl.program_id(0) == 0)
    def _():
        chunks = [(src, dst, r) for src, dst in ((pa_hbm, pa_ref), (pr_hbm, pr_ref), (wo_hbm, wo_ref))
                  for r in range(src.shape[0] // STAGE_ROWS)]

        def chunk_copy(c):
            src, _, r = chunks[c]
            return pltpu.make_async_copy(src.at[pl.ds(r * STAGE_ROWS, STAGE_ROWS), :],
                                         stage_ref.at[c % 2], stage_sem.at[c % 2])

        chunk_copy(0).start()
        for c, (_, dst, r) in enumerate(chunks):
            if c + 1 < len(chunks):
                chunk_copy(c + 1).start()
            chunk_copy(c).wait()
            dst[r * STAGE_ROWS:(r + 1) * STAGE_ROWS, :] = stage_ref[c % 2].astype(BF16)

    a = jnp.dot(oa_ref[...], pa_ref[...], preferred_element_type=F32)
    r = jnp.dot(or_ref[...], pr_ref[...], preferred_element_type=F32)
    g_a = jax.nn.sigmoid(ga_ref[...].astype(F32) + bg_ref[0:1, :])
    g_r = jax.nn.sigmoid(gr_ref[...].astype(F32) + bg_ref[1:2, :])
    merged = (g_a * a + g_r * r).astype(BF16)
    x1 = x_ref[...] + jnp.dot(merged, wo_ref[...], preferred_element_type=F32)
    ms = jnp.mean(x1 * x1, axis=-1, keepdims=True)
    h2 = (x1 * lax.rsqrt(ms + EPS) * n2_ref[...]).astype(BF16)
    d_ff = w1_ref.shape[1]
    cw = 1024
    acc = x1
    for c in range(d_ff // cw):
        t = jnp.dot(h2, w1_ref[:, c * cw:(c + 1) * cw], preferred_element_type=F32)
        t = jnp.maximum(t, 0.0)
        acc = acc + jnp.dot((t * t).astype(BF16), w2_ref[c * cw:(c + 1) * cw, :],
                            preferred_element_type=F32)
    y_ref[...] = acc


def _out(x2, o_attn, o_rnn, gates, bg, pa, pr, wo, n2, w1, w2, tm):
    T, D = x2.shape
    d_rnn = o_rnn.shape[1]
    grid = (T // tm,)
    return pl.pallas_call(
        _out_body,
        grid=grid,
        in_specs=[
            pl.BlockSpec((tm, D), lambda t: (t, 0)),
            pl.BlockSpec((tm, D_ATTN), lambda t: (t, 0)),
            pl.BlockSpec((tm, d_rnn), lambda t: (t, 0)),
            pl.BlockSpec((tm, D), lambda t: (t, 0)),
            pl.BlockSpec((tm, D), lambda t: (t, 1)),
            _const_spec(bg.shape), pl.BlockSpec(memory_space=pl.ANY), pl.BlockSpec(memory_space=pl.ANY),
            pl.BlockSpec(memory_space=pl.ANY), _const_spec(n2.shape), _const_spec(w1.shape),
            _const_spec(w2.shape),
        ],
        out_specs=pl.BlockSpec((tm, D), lambda t: (t, 0)),
        out_shape=jax.ShapeDtypeStruct((T, D), F32),
        scratch_shapes=[pltpu.VMEM(pa.shape, BF16), pltpu.VMEM(pr.shape, BF16), pltpu.VMEM(wo.shape, BF16),
                        pltpu.VMEM((2, STAGE_ROWS, D), F32), pltpu.SemaphoreType.DMA((2,))],
        compiler_params=pltpu.CompilerParams(
            dimension_semantics=("arbitrary",), vmem_limit_bytes=VMEM_LIMIT),
        name="merge_mlp",
    )(x2, o_attn, o_rnn, gates, gates, bg, pa, pr, wo, n2, w1, w2)


def _t5_bucket(rel):
    max_exact = NUM_BUCKETS // 2
    n = jnp.maximum(rel, 0)
    nf = jnp.maximum(n, 1).astype(F32)
    large = max_exact + (jnp.log(nf / max_exact) / math.log(MAX_DISTANCE / max_exact)
                         * (NUM_BUCKETS - max_exact)).astype(jnp.int32)
    large = jnp.minimum(large, NUM_BUCKETS - 1)
    return jnp.where(n < max_exact, n, large)


def _bias_tables(rel_bias):
    offs = jnp.arange(BLOCK)
    rel_own = offs[None, :] - offs[:, None]
    buckets = jnp.arange(NUM_BUCKETS)

    def lookup(rel):
        onehot = (_t5_bucket(rel)[..., None] == buckets).astype(F32)
        return jnp.einsum('kqb,bh->hkq', onehot, rel_bias, precision=lax.Precision.HIGHEST)

    bias_d = jnp.where(rel_own >= 0, lookup(rel_own) * LOG2E, NEG)
    bias_p = lookup(rel_own + BLOCK) * LOG2E
    bias_far = rel_bias[NUM_BUCKETS - 1] * LOG2E
    return bias_d.astype(F32), bias_p.astype(F32), bias_far.astype(F32)


def _block_diag(w):
    per = GROUP // RNN_BLOCK_DIM
    w4 = w.reshape(RNN_BLOCKS // per, per, RNN_BLOCK_DIM, RNN_BLOCK_DIM)
    eye = jnp.eye(per, dtype=w.dtype)
    return jnp.einsum('gade,ab->gadbe', w4, eye).reshape(RNN_BLOCKS // per, GROUP, GROUP)


def _layer(x, norm1_w, w_in, b_gate, q_norm_w, k_norm_w, bias_tabs, conv_w, conv_b,
           w_rg_a, b_rg_a, w_rg_i, b_rg_i, lru_lambda, w_proj_attn, w_proj_rnn,
           w_out, norm2_w, w_ff1, w_ff2, tm_in, tr, tm_out):
    B, S, D = x.shape
    d_rnn = conv_w.shape[1]
    qw = (q_norm_w * (HEAD_DIM ** -0.5 * LOG2E)).reshape(1, HEAD_DIM, 1)
    kw = jnp.tile(k_norm_w, N_HEADS).reshape(1, D_ATTN)
    head_of = np.arange(D_ATTN) // HEAD_DIM
    gmat = jnp.asarray(head_of[:, None] == head_of[None, :], dtype=BF16)

    qT, k, vT, kmean, gates, o_rnn, w1b, w2b = _inproj_rnn(
        x, w_in, norm1_w.reshape(1, D), qw, kw, gmat, conv_w, conv_b.reshape(1, d_rnn),
        _block_diag(w_rg_a).astype(BF16), _block_diag(w_rg_i).astype(BF16),
        b_rg_a.reshape(1, d_rnn), b_rg_i.reshape(1, d_rnn), lru_lambda.reshape(1, d_rnn),
        w_ff1, w_ff2, tm_in, tr)
    o_attn = _attention(qT, k, vT, kmean.reshape(B, S // BLOCK, D_ATTN), *bias_tabs, HEADS_PER_STEP)
    y = _out(x.reshape(B * S, D), o_attn.reshape(B * S, D_ATTN), o_rnn.reshape(B * S, d_rnn),
             gates.reshape(B * S, -1), b_gate, w_proj_attn, w_proj_rnn,
             w_out, norm2_w.reshape(1, D), w1b, w2b, tm_out)
    return y.reshape(B, S, D)


def kernel(x, norm1_w, w_in, b_gate, q_norm_w, k_norm_w, rel_bias, conv_w, conv_b, w_rg_a, b_rg_a,
           w_rg_i, b_rg_i, lru_lambda, w_proj_attn, w_proj_rnn, w_out, norm2_w, w_ff1, w_ff2):
    S = x.shape[1]
    assert S % BLOCK == 0
    bias_tabs = _bias_tables(rel_bias)
    tm_in = min(512, S)
    tr = min(256, S)
    tm_out = min(512, S)
    for l in range(norm1_w.shape[0]):
        x = _layer(x, norm1_w[l], w_in[l], b_gate[l], q_norm_w[l], k_norm_w[l], bias_tabs,
                   conv_w[l], conv_b[l], w_rg_a[l], b_rg_a[l], w_rg_i[l], b_rg_i[l], lru_lambda[l],
                   w_proj_attn[l], w_proj_rnn[l], w_out[l], norm2_w[l], w_ff1[l], w_ff2[l],
                   tm_in, tr, tm_out)
    return x
```

```python
import math

import numpy as np
import jax
import jax.numpy as jnp
from jax import lax
from jax.experimental import pallas as pl
from jax.experimental.pallas import tpu as pltpu

N_HEADS = 8
HEAD_DIM = 64
D_ATTN = N_HEADS * HEAD_DIM
BLOCK = 256
TOPK = 3
NUM_BUCKETS = 32
MAX_DISTANCE = 128
RNN_BLOCKS = 16
RNN_BLOCK_DIM = 64
CONV_WIDTH = 4
LRU_C = 8.0
EPS = 1e-6
NEG = -1e30
LOG2E = math.log2(math.e)
HEADS_PER_STEP = 8
TILE_HEADS = 4
AHEAD = 2
Q_BLOCKS_PER_STEP = 8
GROUP = 256
STAGE_COLS = 512
STAGE_ROWS = 512
SUBLANES = 8
BF16_ROWS = 16
V_ROWS = HEAD_DIM + BF16_ROWS
VMEM_LIMIT = 56 * 1024 * 1024

F32 = jnp.float32
BF16 = jnp.bfloat16
_NT = (((1,), (1,)), ((), ()))


def _const_spec(shape):
    nd = len(shape)
    return pl.BlockSpec(shape, lambda *_: (0,) * nd, pipeline_mode=pl.Buffered(1))


def _inproj_rnn_body(x_ref, win_ref, n1_ref, qw_ref, kw_ref, g_ref,
                     perm_ref, permT_ref, cw_ref, cb_ref, wa_ref, wi_ref, ba_ref, bi_ref, lam_ref,
                     w1_ref, w2_ref,
                     qT_ref, k_ref, vT_ref, km_ref, gates_ref, orn_ref, w1b_ref, w2b_ref,
                     xs_ref, y_ref, xc_ref, a_ref, u_ref, tail_ref, hc_ref,
                     wqT_ref, wk_ref, wvT_ref, wxy_ref, wg_ref, stage_ref, stage_sem):
    tm = x_ref.shape[1]
    d = cw_ref.shape[1]
    tr = perm_ref.shape[0]
    seg = tr // SUBLANES
    taps = CONV_WIDTH - 1

    @pl.when(pl.program_id(1) == 0)
    def _():
        tail_ref[...] = jnp.zeros_like(tail_ref)
        hc_ref[...] = jnp.zeros_like(hc_ref)

    @pl.when((pl.program_id(0) == 0) & (pl.program_id(1) == 0))
    def _():
        n_chunks = win_ref.shape[1] // STAGE_COLS
        dests = ([(wqT_ref, None), (wk_ref, None), (wvT_ref, None)]
                 + [(wxy_ref, c) for c in range(wxy_ref.shape[1] // STAGE_COLS)]
                 + [(wg_ref, c) for c in range(wg_ref.shape[1] // STAGE_COLS)])
        assert len(dests) == n_chunks and D_ATTN == STAGE_COLS

        def chunk_copy(c):
            return pltpu.make_async_copy(win_ref.at[:, pl.ds(c * STAGE_COLS, STAGE_COLS)],
                                         stage_ref.at[c % 2], stage_sem.at[c % 2])

        chunk_copy(0).start()
        for c, (dst, col) in enumerate(dests):
            if c + 1 < n_chunks:
                chunk_copy(c + 1).start()
            chunk_copy(c).wait()
            w = stage_ref[c % 2]
            if dst is wqT_ref or dst is wvT_ref:
                dst[...] = w.T.astype(BF16)
            elif col is None:
                dst[...] = w.astype(BF16)
            else:
                dst[:, col * STAGE_COLS:(col + 1) * STAGE_COLS] = w.astype(BF16)

    w1b_ref[...] = w1_ref[...].astype(BF16)
    w2b_ref[...] = w2_ref[...].astype(BF16)

    x = x_ref[0]
    ms = jnp.mean(x * x, axis=-1, keepdims=True)
    h = (x * lax.rsqrt(ms + EPS) * n1_ref[...]).astype(BF16)

    def rnn_inputs(t):
        hp = jnp.dot(perm_ref[...], h[t * tr:(t + 1) * tr], preferred_element_type=F32).astype(BF16)
        xs_ref[t] = jnp.dot(hp, wxy_ref[:, :d], preferred_element_type=F32)
        y_ref[t] = jnp.dot(hp, wxy_ref[:, d:], preferred_element_type=F32)

    def rnn_conv(t):
        def slab(k):
            return xs_ref[t, k * SUBLANES:(k + 1) * SUBLANES, :]

        first_seg = lax.broadcasted_iota(jnp.int32, (SUBLANES, d), 0) == 0
        last = [slab(seg - taps + j) for j in range(taps)]
        window = [jnp.where(first_seg, pltpu.roll(tail_ref[j], 1, 0), pltpu.roll(last[j], 1, 0))
                  for j in range(taps)]
        for j in range(taps):
            tail_ref[j] = last[j]
        cw = [0.5 * cw_ref[j:j + 1, :] for j in range(CONV_WIDTH)]
        cb = 0.5 * cb_ref[...]
        for k in range(seg):
            cur = slab(k)
            acc = cb + window[0] * cw[0]
            for j in range(1, taps):
                acc = acc + window[j] * cw[j]
            xc_ref[t, k * SUBLANES:(k + 1) * SUBLANES, :] = acc + cur * cw[taps]
            window = window[1:] + [cur]

    def rnn_gate_mm(t):
        xb = xc_ref[t].astype(BF16)
        for g in range(d // GROUP):
            cols = slice(g * GROUP, (g + 1) * GROUP)
            a_ref[t, :, cols] = jnp.dot(xb[:, cols], wa_ref[g], preferred_element_type=F32)
            u_ref[t, :, cols] = jnp.dot(xb[:, cols], wi_ref[g], preferred_element_type=F32)

    def rnn_nonlin(t):
        half_x = xc_ref[t]
        t_r = jnp.tanh(a_ref[t] + 0.5 * ba_ref[...])
        t_i = jnp.tanh(u_ref[t] + 0.5 * bi_ref[...])
        nlam = -lam_ref[...]
        softplus = jnp.maximum(nlam, 0.0) + jnp.log1p(jnp.exp(-jnp.abs(nlam)))
        half_c = softplus * (-0.5 * LRU_C * LOG2E)
        a = jnp.exp2(t_r * half_c + half_c)
        a_ref[t] = a
        gap = 1.0 - a * a
        u_ref[t] = (gap * lax.rsqrt(jnp.maximum(gap, 1e-30))) * (half_x * t_i + half_x)

    def rnn_scan(t):
        hcur = u_ref[t, 0:SUBLANES, :]
        prod = a_ref[t, 0:SUBLANES, :]
        for k in range(1, seg):
            rows = slice(k * SUBLANES, (k + 1) * SUBLANES)
            a_k = a_ref[t, rows, :]
            hcur = a_k * hcur + u_ref[t, rows, :]
            prod = a_k * prod
            u_ref[t, rows, :] = hcur
            a_ref[t, rows, :] = prod
        entry = [hc_ref[...]]
        for s in range(SUBLANES):
            entry.append(prod[s:s + 1, :] * entry[s] + hcur[s:s + 1, :])
        hc_ref[...] = entry[SUBLANES]
        h_in = jnp.concatenate(entry[:SUBLANES], axis=0)
        for k in range(seg):
            rows = slice(k * SUBLANES, (k + 1) * SUBLANES)
            u_ref[t, rows, :] = a_ref[t, rows, :] * h_in + u_ref[t, rows, :]

    def rnn_output(t):
        y = y_ref[t]
        k1 = -2.0 * math.sqrt(2.0 / math.pi) * LOG2E
        gelu = y / (1.0 + jnp.exp2(y * (k1 + (k1 * 0.044715) * (y * y))))
        o_slab = (u_ref[t] * gelu).astype(BF16)
        orn_ref[0, t * tr:(t + 1) * tr, :] = jnp.dot(
            permT_ref[...], o_slab, preferred_element_type=F32).astype(BF16)

    def proj_q():
        qT = lax.dot_general(wqT_ref[...], h, _NT, preferred_element_type=F32)
        q3 = qT.reshape(N_HEADS, HEAD_DIM, tm)
        qss = jnp.mean(q3 * q3, axis=1, keepdims=True)
        qn = q3 * lax.rsqrt(qss + EPS) * qw_ref[...]
        qT_ref[0] = qn.reshape(D_ATTN, tm).astype(BF16)

    def proj_v():
        vT = lax.dot_general(wvT_ref[...], h, _NT, preferred_element_type=F32).astype(BF16)
        ones_rows = (lax.broadcasted_iota(jnp.int32, (BF16_ROWS, tm), 0) == 0).astype(BF16)
        for hd in range(N_HEADS):
            vT_ref[0, hd * V_ROWS:hd * V_ROWS + HEAD_DIM, :] = vT[hd * HEAD_DIM:(hd + 1) * HEAD_DIM]
            vT_ref[0, hd * V_ROWS + HEAD_DIM:(hd + 1) * V_ROWS, :] = ones_rows

    def proj_k():
        k = jnp.dot(h, wk_ref[...], preferred_element_type=F32)
        k2 = k * k
        hi = k2.astype(BF16)
        lo = (k2 - hi.astype(F32)).astype(BF16)
        kss = (jnp.dot(hi, g_ref[...], preferred_element_type=F32)
               + jnp.dot(lo, g_ref[...], preferred_element_type=F32))
        kn = k * lax.rsqrt(kss * (1.0 / HEAD_DIM) + EPS) * kw_ref[...]
        k_ref[0] = kn.astype(BF16)
        for bi in range(tm // BLOCK):
            km_ref[0, bi] = jnp.mean(kn[bi * BLOCK:(bi + 1) * BLOCK], axis=0, keepdims=True)

    gate_chunk = 512

    def proj_gate(c):
        cols = slice(c * gate_chunk, (c + 1) * gate_chunk)
        gates_ref[0, :, cols] = jnp.dot(h, wg_ref[:, cols], preferred_element_type=F32).astype(BF16)

    assert tm == 2 * tr and wg_ref.shape[1] == 4 * gate_chunk
    rnn_inputs(0)
    proj_q()
    rnn_conv(0)
    rnn_gate_mm(0)
    proj_v()
    proj_gate(0)
    rnn_inputs(1)
    rnn_nonlin(0)
    rnn_scan(0)
    rnn_output(0)
    proj_k()
    rnn_conv(1)
    rnn_gate_mm(1)
    proj_gate(1)
    proj_gate(2)
    proj_gate(3)
    rnn_nonlin(1)
    rnn_scan(1)
    rnn_output(1)


def _inproj_rnn(x, w_in, n1, qw, kw, gmat, conv_w, conv_b, wa_bd, wi_bd, b_a, b_i, lam,
                w_ff1, w_ff2, tm, tr):
    B, S, D = x.shape
    d = conv_w.shape[1]
    d_g = w_in.shape[1] - 3 * D_ATTN - 2 * d
    nb = S // BLOCK
    n_sub = tm // tr
    grid = (B, S // tm)
    seg = tr // SUBLANES
    t = np.arange(tr)
    perm_np = np.zeros((tr, tr), np.float32)
    perm_np[(t % seg) * SUBLANES + t // seg, t] = 1.0
    perm = jnp.asarray(perm_np, BF16)
    permT = jnp.asarray(perm_np.T, BF16)
    consts = (n1, qw, kw, gmat, perm, permT, conv_w, conv_b, wa_bd, wi_bd, b_a, b_i, lam)
    n_s = S // tm
    n_steps = B * n_s
    r1, r2 = w_ff1.shape[0] // n_steps, w_ff2.shape[0] // n_steps
    assert r1 * n_steps == w_ff1.shape[0] and r2 * n_steps == w_ff2.shape[0]
    assert r1 % BF16_ROWS == 0 and r2 % BF16_ROWS == 0
    w1_spec = pl.BlockSpec((r1, w_ff1.shape[1]), lambda b, s: (b * n_s + s, 0))
    w2_spec = pl.BlockSpec((r2, w_ff2.shape[1]), lambda b, s: (b * n_s + s, 0))
    return pl.pallas_call(
        _inproj_rnn_body,
        grid=grid,
        in_specs=([pl.BlockSpec((1, tm, D), lambda b, s: (b, s, 0)), pl.BlockSpec(memory_space=pl.ANY)]
                  + [_const_spec(c.shape) for c in consts] + [w1_spec, w2_spec]),
        out_specs=[
            pl.BlockSpec((1, D_ATTN, tm), lambda b, s: (b, 0, s)),
            pl.BlockSpec((1, tm, D_ATTN), lambda b, s: (b, s, 0)),
            pl.BlockSpec((1, N_HEADS * V_ROWS, tm), lambda b, s: (b, 0, s)),
            pl.BlockSpec((1, tm // BLOCK, 1, D_ATTN), lambda b, s: (b, s, 0, 0)),
            pl.BlockSpec((1, tm, d_g), lambda b, s: (b, s, 0)),
            pl.BlockSpec((1, tm, d), lambda b, s: (b, s, 0)),
            w1_spec, w2_spec,
        ],
        out_shape=[
            jax.ShapeDtypeStruct((B, D_ATTN, S), BF16),
            jax.ShapeDtypeStruct((B, S, D_ATTN), BF16),
            jax.ShapeDtypeStruct((B, N_HEADS * V_ROWS, S), BF16),
            jax.ShapeDtypeStruct((B, nb, 1, D_ATTN), F32),
            jax.ShapeDtypeStruct((B, S, d_g), BF16),
            jax.ShapeDtypeStruct((B, S, d), BF16),
            jax.ShapeDtypeStruct(w_ff1.shape, BF16),
            jax.ShapeDtypeStruct(w_ff2.shape, BF16),
        ],
        scratch_shapes=[pltpu.VMEM((n_sub, tr, d), F32) for _ in range(5)] + [
            pltpu.VMEM((CONV_WIDTH - 1, SUBLANES, d), F32), pltpu.VMEM((1, d), F32),
            pltpu.VMEM((D_ATTN, D), BF16), pltpu.VMEM((D, D_ATTN), BF16), pltpu.VMEM((D_ATTN, D), BF16),
            pltpu.VMEM((D, 2 * d), BF16), pltpu.VMEM((D, d_g), BF16),
            pltpu.VMEM((2, D, STAGE_COLS), F32), pltpu.SemaphoreType.DMA((2,))],
        compiler_params=pltpu.CompilerParams(
            dimension_semantics=("arbitrary", "arbitrary"), vmem_limit_bytes=VMEM_LIMIT),
        name="inproj_rglru",
    )(x, w_in, *consts, w_ff1, w_ff2)


def _attn_body(*refs):
    for sub in range(Q_BLOCKS_PER_STEP):
        _attn_block(sub, *refs)


def _attn_block(sub, far_ref, qT_ref, k_ref, vT_ref, km_ref, bd_ref, bp_ref, o_ref, sel_ref, qk_ref, m_ref, acc_ref):
    p = pl.program_id(1)
    i = pl.program_id(2) * Q_BLOCKS_PER_STEP + sub
    q_cols = slice(sub * BLOCK, (sub + 1) * BLOCK)
    nb = km_ref.shape[1]
    hps = bd_ref.shape[0]
    heads = range(hps)
    groups = range(hps // TILE_HEADS)
    lanes = TILE_HEADS * HEAD_DIM
    row = lax.broadcasted_iota(jnp.int32, (lanes, BLOCK), 0)
    blk = lax.broadcasted_iota(jnp.int32, (nb, BLOCK), 0)
    lane_head = lax.broadcasted_iota(jnp.int32, (nb, lanes), 1) // HEAD_DIM
    blk_f = blk.astype(F32)
    past = blk < i
    qz = []
    for g in groups:
        qall = qT_ref[0, g * lanes:(g + 1) * lanes, q_cols]
        qz += [jnp.where((row >= t * HEAD_DIM) & (row < (t + 1) * HEAD_DIM), qall, jnp.zeros_like(qall))
               for t in range(TILE_HEADS)]

        km = km_ref[0, :, g * lanes:(g + 1) * lanes]
        km_hi = km.astype(BF16)
        km_r = km - km_hi.astype(F32)
        km_mid = km_r.astype(BF16)
        km_lo = (km_r - km_mid.astype(F32)).astype(BF16)
        stacked = jnp.concatenate(
            [jnp.where(lane_head == t, piece, jnp.zeros_like(piece))
             for piece in (km_hi, km_mid, km_lo) for t in range(TILE_HEADS)], axis=0)
        gates = jnp.dot(stacked, qall, preferred_element_type=F32)
        for t in range(TILE_HEADS):
            hh = g * TILE_HEADS + t
            gate = (gates[t * nb:(t + 1) * nb] + gates[(TILE_HEADS + t) * nb:(TILE_HEADS + t + 1) * nb]
                    + gates[(2 * TILE_HEADS + t) * nb:(2 * TILE_HEADS + t + 1) * nb])
            gate = jnp.where(past, gate, -jnp.inf)
            chosen = jnp.zeros((nb, BLOCK), F32)
            for _ in range(TOPK):
                best = jnp.max(gate, axis=0, keepdims=True)
                first = jnp.min(jnp.where(gate == best, blk_f, float(nb)), axis=0, keepdims=True)
                hit = blk_f == first
                gate = jnp.where(hit, -jnp.inf, gate)
                chosen = jnp.where(hit, 1.0, chosen)
            sel_ref[hh] = jnp.where(past & (chosen > 0.5), far_ref[p * hps + hh], NEG)

    def keys(j, g):
        return k_ref[0, pl.ds(pl.multiple_of(j * BLOCK, BLOCK), BLOCK), g * lanes:(g + 1) * lanes]

    def values(j, hh):
        return vT_ref[0, hh * V_ROWS:(hh + 1) * V_ROWS,
                      pl.ds(pl.multiple_of(j * BLOCK, BLOCK), BLOCK)]

    def scores(kb, hh, bias):
        qk = jnp.dot(kb, qz[hh], preferred_element_type=F32)
        return qk if bias is None else qk + bias[hh]

    def step(slot, j, row, nxt_slot, nxt_j, nxt_bias=None):
        kbs = [keys(nxt_j, g) for g in groups]

        def issue(hh):
            qk_ref[nxt_slot, hh] = scores(kbs[hh // TILE_HEADS], hh, nxt_bias)

        for hh in range(AHEAD):
            issue(hh)
        for hh in heads:
            if hh + AHEAD < hps:
                issue(hh + AHEAD)
            m_prev = m_ref[hh]
            s = qk_ref[slot, hh]
            r = row(hh)
            m_new = jnp.maximum(m_prev, jnp.max(s, axis=0, keepdims=True) + r)
            m_ref[hh] = m_new
            alpha = jnp.exp2(m_prev - m_new)
            pb = jnp.exp2((s - (m_new - r)).astype(BF16))
            acc_ref[hh] = alpha * acc_ref[hh] + jnp.dot(values(j, hh), pb, preferred_element_type=F32)

    jp = jnp.maximum(i - 1, 0)
    n_old = jp
    m_ref[...] = jnp.full(m_ref.shape, NEG, F32)
    acc_ref[...] = jnp.zeros(acc_ref.shape, F32)
    zero_row = jnp.zeros((1, BLOCK), F32)
    for g in groups:
        kb = keys(i, g)
        for hh in range(g * TILE_HEADS, (g + 1) * TILE_HEADS):
            qk_ref[0, hh] = scores(kb, hh, bd_ref)
    step(0, i, lambda hh: zero_row, 1, jp, bp_ref)
    step(1, jp, lambda hh: jnp.where(sel_ref[hh, pl.ds(jp, 1), :] > 0.5 * NEG, 0.0, NEG), 0, 0)

    def older_pair(t, carry):
        ja = 2 * t
        jb = ja + 1
        step(0, ja, lambda hh: sel_ref[hh, pl.ds(ja, 1), :], 1, jb)
        step(1, jb, lambda hh: jnp.where(jb < n_old, sel_ref[hh, pl.ds(jb, 1), :], NEG), 0, ja + 2)
        return carry

    lax.fori_loop(0, (n_old + 1) // 2, older_pair, 0)
    outs = [acc_ref[hh, :HEAD_DIM, :] * (1.0 / acc_ref[hh, HEAD_DIM:HEAD_DIM + 1, :]) for hh in heads]
    o_ref[0, q_cols, :] = jnp.concatenate(outs, axis=0).T.astype(BF16)


def _attention(qT, k, vT, kmean, bias_d, bias_p, bias_far, hps):
    B, _, S = qT.shape
    nb = S // BLOCK
    lanes = hps * HEAD_DIM
    qb = Q_BLOCKS_PER_STEP
    grid = (B, N_HEADS // hps, nb // qb)
    assert hps % TILE_HEADS == 0 and nb % qb == 0
    return pl.pallas_call(
        _attn_body,
        grid=grid,
        in_specs=[
            pl.BlockSpec(memory_space=pltpu.SMEM),
            pl.BlockSpec((1, lanes, qb * BLOCK), lambda b, p, i: (b, p, i)),
            pl.BlockSpec((1, S, lanes), lambda b, p, i: (b, 0, p)),
            pl.BlockSpec((1, hps * V_ROWS, S), lambda b, p, i: (b, p, 0)),
            pl.BlockSpec((1, nb, lanes), lambda b, p, i: (b, 0, p)),
            pl.BlockSpec((hps, BLOCK, BLOCK), lambda b, p, i: (p, 0, 0)),
            pl.BlockSpec((hps, BLOCK, BLOCK), lambda b, p, i: (p, 0, 0)),
        ],
        out_specs=pl.BlockSpec((1, qb * BLOCK, lanes), lambda b, p, i: (b, i, p)),
        out_shape=jax.ShapeDtypeStruct((B, S, D_ATTN), BF16),
        scratch_shapes=[pltpu.VMEM((hps, nb, BLOCK), F32), pltpu.VMEM((2, hps, BLOCK, BLOCK), F32),
                        pltpu.VMEM((hps, 1, BLOCK), F32), pltpu.VMEM((hps, V_ROWS, BLOCK), F32)],
        compiler_params=pltpu.CompilerParams(
            dimension_semantics=("parallel", "parallel", "arbitrary"), vmem_limit_bytes=VMEM_LIMIT),
        name="moba_attn",
    )(bias_far, qT, k, vT, kmean, bias_d, bias_p)


def _out_body(x_ref, oa_ref, or_ref, ga_ref, gr_ref, bg_ref, pa_hbm, pr_hbm, wo_hbm, n2_ref,
              w1_ref, w2_ref, y_ref, pa_ref, pr_ref, wo_ref, stage_ref, stage_sem):
    @pl.when(pl.program_id(0) == 0)
    def _():
        chunks = [(src, dst, r) for src, dst in ((pa_hbm, pa_ref), (pr_hbm, pr_ref), (wo_hbm, wo_ref))
                  for r in range(src.shape[0] // STAGE_ROWS)]

        def chunk_copy(c):
            src, _, r = chunks[c]
            return pltpu.make_async_copy(src.at[pl.ds(r * STAGE_ROWS, STAGE_ROWS), :],
                                         stage_ref.at[c % 2], stage_sem.at[c % 2])

        chunk_copy(0).start()
        for c, (_, dst, r) in enumerate(chunks):
            if c + 1 < len(chunks):
                chunk_copy(c + 1).start()
            chunk_copy(c).wait()
            dst[r * STAGE_ROWS:(r + 1) * STAGE_ROWS, :] = stage_ref[c % 2].astype(BF16)

    a = jnp.dot(oa_ref[...], pa_ref[...], preferred_element_type=F32)
    r = jnp.dot(or_ref[...], pr_ref[...], preferred_element_type=F32)
    g_a = jax.nn.sigmoid(ga_ref[...].astype(F32) + bg_ref[0:1, :])
    g_r = jax.nn.sigmoid(gr_ref[...].astype(F32) + bg_ref[1:2, :])
    merged = (g_a * a + g_r * r).astype(BF16)
    x1 = x_ref[...] + jnp.dot(merged, wo_ref[...], preferred_element_type=F32)
    ms = jnp.mean(x1 * x1, axis=-1, keepdims=True)
    h2 = (x1 * lax.rsqrt(ms + EPS) * n2_ref[...]).astype(BF16)
    d_ff = w1_ref.shape[1]
    cw = 1024
    acc = x1
    for c in range(d_ff // cw):
        t = jnp.dot(h2, w1_ref[:, c * cw:(c + 1) * cw], preferred_element_type=F32)
        t = jnp.maximum(t, 0.0)
        acc = acc + jnp.dot((t * t).astype(BF16), w2_ref[c * cw:(c + 1) * cw, :],
                            preferred_element_type=F32)
    y_ref[...] = acc


def _out(x2, o_attn, o_rnn, gates, bg, pa, pr, wo, n2, w1, w2, tm):
    T, D = x2.shape
    d_rnn = o_rnn.shape[1]
    grid = (T // tm,)
    return pl.pallas_call(
        _out_body,
        grid=grid,
        in_specs=[
            pl.BlockSpec((tm, D), lambda t: (t, 0)),
            pl.BlockSpec((tm, D_ATTN), lambda t: (t, 0)),
            pl.BlockSpec((tm, d_rnn), lambda t: (t, 0)),
            pl.BlockSpec((tm, D), lambda t: (t, 0)),
            pl.BlockSpec((tm, D), lambda t: (t, 1)),
            _const_spec(bg.shape), pl.BlockSpec(memory_space=pl.ANY), pl.BlockSpec(memory_space=pl.ANY),
            pl.BlockSpec(memory_space=pl.ANY), _const_spec(n2.shape), _const_spec(w1.shape),
            _const_spec(w2.shape),
        ],
        out_specs=pl.BlockSpec((tm, D), lambda t: (t, 0)),
        out_shape=jax.ShapeDtypeStruct((T, D), F32),
        scratch_shapes=[pltpu.VMEM(pa.shape, BF16), pltpu.VMEM(pr.shape, BF16), pltpu.VMEM(wo.shape, BF16),
                        pltpu.VMEM((2, STAGE_ROWS, D), F32), pltpu.SemaphoreType.DMA((2,))],
        compiler_params=pltpu.CompilerParams(
            dimension_semantics=("arbitrary",), vmem_limit_bytes=VMEM_LIMIT),
        name="merge_mlp",
    )(x2, o_attn, o_rnn, gates, gates, bg, pa, pr, wo, n2, w1, w2)


def _t5_bucket(rel):
    max_exact = NUM_BUCKETS // 2
    n = jnp.maximum(rel, 0)
    nf = jnp.maximum(n, 1).astype(F32)
    large = max_exact + (jnp.log(nf / max_exact) / math.log(MAX_DISTANCE / max_exact)
                         * (NUM_BUCKETS - max_exact)).astype(jnp.int32)
    large = jnp.minimum(large, NUM_BUCKETS - 1)
    return jnp.where(n < max_exact, n, large)


def _bias_tables(rel_bias):
    offs = jnp.arange(BLOCK)
    rel_own = offs[None, :] - offs[:, None]
    buckets = jnp.arange(NUM_BUCKETS)

    def lookup(rel):
        onehot = (_t5_bucket(rel)[..., None] == buckets).astype(F32)
        return jnp.einsum('kqb,bh->hkq', onehot, rel_bias, precision=lax.Precision.HIGHEST)

    bias_d = jnp.where(rel_own >= 0, lookup(rel_own) * LOG2E, NEG)
    bias_p = lookup(rel_own + BLOCK) * LOG2E
    bias_far = rel_bias[NUM_BUCKETS - 1] * LOG2E
    return bias_d.astype(F32), bias_p.astype(F32), bias_far.astype(F32)


def _block_diag(w):
    per = GROUP // RNN_BLOCK_DIM
    w4 = w.reshape(RNN_BLOCKS // per, per, RNN_BLOCK_DIM, RNN_BLOCK_DIM)
    eye = jnp.eye(per, dtype=w.dtype)
    return jnp.einsum('gade,ab->gadbe', w4, eye).reshape(RNN_BLOCKS // per, GROUP, GROUP)


def _layer(x, norm1_w, w_in, b_gate, q_norm_w, k_norm_w, bias_tabs, conv_w, conv_b,
           w_rg_a, b_rg_a, w_rg_i, b_rg_i, lru_lambda, w_proj_attn, w_proj_rnn,
           w_out, norm2_w, w_ff1, w_ff2, tm_in, tr, tm_out):
    B, S, D = x.shape
    d_rnn = conv_w.shape[1]
    qw = (q_norm_w * (HEAD_DIM ** -0.5 * LOG2E)).reshape(1, HEAD_DIM, 1)
    kw = jnp.tile(k_norm_w, N_HEADS).reshape(1, D_ATTN)
    head_of = np.arange(D_ATTN) // HEAD_DIM
    gmat = jnp.asarray(head_of[:, None] == head_of[None, :], dtype=BF16)

    qT, k, vT, kmean, gates, o_rnn, w1b, w2b = _inproj_rnn(
        x, w_in, norm1_w.reshape(1, D), qw, kw, gmat, conv_w, conv_b.reshape(1, d_rnn),
        _block_diag(w_rg_a).astype(BF16), _block_diag(w_rg_i).astype(BF16),
        b_rg_a.reshape(1, d_rnn), b_rg_i.reshape(1, d_rnn), lru_lambda.reshape(1, d_rnn),
        w_ff1, w_ff2, tm_in, tr)
    o_attn = _attention(qT, k, vT, kmean.reshape(B, S // BLOCK, D_ATTN), *bias_tabs, HEADS_PER_STEP)
    y = _out(x.reshape(B * S, D), o_attn.reshape(B * S, D_ATTN), o_rnn.reshape(B * S, d_rnn),
             gates.reshape(B * S, -1), b_gate, w_proj_attn, w_proj_rnn,
             w_out, norm2_w.reshape(1, D), w1b, w2b, tm_out)
    return y.reshape(B, S, D)


def kernel(x, norm1_w, w_in, b_gate, q_norm_w, k_norm_w, rel_bias, conv_w, conv_b, w_rg_a, b_rg_a,
           w_rg_i, b_rg_i, lru_lambda, w_proj_attn, w_proj_rnn, w_out, norm2_w, w_ff1, w_ff2):
    S = x.shape[1]
    assert S % BLOCK == 0
    bias_tabs = _bias_tables(rel_bias)
    tm_in = min(512, S)
    tr = min(256, S)
    tm_out = min(512, S)
    for l in range(norm1_w.shape[0]):
        x = _layer(x, norm1_w[l], w_in[l], b_gate[l], q_norm_w[l], k_norm_w[l], bias_tabs,
                   conv_w[l], conv_b[l], w_rg_a[l], b_rg_a[l], w_rg_i[l], b_rg_i[l], lru_lambda[l],
                   w_proj_attn[l], w_proj_rnn[l], w_out[l], norm2_w[l], w_ff1[l], w_ff2[l],
                   tm_in, tr, tm_out)
    return x
```
